```python
import math
import numpy as np
import jax
import jax.numpy as jnp
from jax import lax

D_MODEL = 1024
BATCH = 8
SEQ = 2048
DEPTH = 4

CTX_LEN = 256
GRID_W = 64
EPS = 1e-6
N_MOD = 6

SSD_D_INNER = 2 * D_MODEL
SSD_HEAD_DIM = 64
SSD_HEADS = SSD_D_INNER // SSD_HEAD_DIM
SSD_GROUPS = 8
SSD_HPG = SSD_HEADS // SSD_GROUPS
SSD_STATE = 128
SSD_CONV = 5
SSD_CHUNK = 128
SSD_XBC = SSD_D_INNER + 2 * SSD_GROUPS * SSD_STATE

NA_HEAD_DIM = 64
NA_HEADS = D_MODEL // NA_HEAD_DIM
NA_WIDTH = NA_HEADS * NA_HEAD_DIM
NA_MAX_KH = 8
NA_KW = 16
NA_QCB = NA_KW
NA_KCB = 2 * NA_KW
NA_NCB = GRID_W // NA_QCB

SWA_HEAD_DIM = 64
SWA_HEADS = D_MODEL // SWA_HEAD_DIM
SWA_KV_HEADS = SWA_HEADS // 4
SWA_GQA = SWA_HEADS // SWA_KV_HEADS
SWA_WINDOW = 128
SWA_Q_WIDTH = SWA_HEADS * SWA_HEAD_DIM
SWA_KV_WIDTH = SWA_KV_HEADS * SWA_HEAD_DIM
ROPE_BASE = 10000.0

D_FF = 4 * D_MODEL
N_BRANCHES = 3

XBC_OFF = 0
DT_OFF = XBC_OFF + SSD_XBC
NA_K_OFF = DT_OFF + 2 * SSD_HEADS
NA_V_OFF = NA_K_OFF + NA_WIDTH
SWA_K_OFF = NA_V_OFF + NA_WIDTH
SWA_V_OFF = SWA_K_OFF + SWA_KV_WIDTH
CTX_COLS = SWA_V_OFF + SWA_KV_WIDTH
Z_OFF = CTX_COLS
NA_Q_OFF = Z_OFF + SSD_D_INNER
SWA_Q_OFF = NA_Q_OFF + NA_WIDTH
GATE_OFF = SWA_Q_OFF + SWA_Q_WIDTH
IN_COLS = GATE_OFF + N_BRANCHES * D_MODEL

kernel_name = "hybrid_ssd_na_swa_dit_trunk"


def rmsnorm(x, g):
    xf = x.astype(jnp.float32)
    y = xf * lax.rsqrt(jnp.mean(xf * xf, axis=-1, keepdims=True) + EPS)
    return y.astype(x.dtype) * g


def group_rmsnorm(x, g, groups):
    shp = x.shape
    xf = x.astype(jnp.float32).reshape(shp[:-1] + (groups, shp[-1] // groups))
    y = xf * lax.rsqrt(jnp.mean(xf * xf, axis=-1, keepdims=True) + EPS)
    return y.reshape(shp).astype(x.dtype) * g


def rope_1d(t, pos):
    half = t.shape[-1] // 2
    inv = ROPE_BASE ** (-jnp.arange(half, dtype=jnp.float32) / half)
    ang = pos.astype(jnp.float32)[:, None] * inv[None, :]
    cos = jnp.cos(ang)[:, None, :]
    sin = jnp.sin(ang)[:, None, :]
    t1 = t[..., :half].astype(jnp.float32)
    t2 = t[..., half:].astype(jnp.float32)
    return jnp.concatenate([t1 * cos - t2 * sin, t1 * sin + t2 * cos], axis=-1).astype(t.dtype)


def axial_rope(t):
    L = t.shape[1]
    pos = jnp.arange(L)
    half = t.shape[-1] // 2
    return jnp.concatenate([rope_1d(t[..., :half], pos // GRID_W),
                            rope_1d(t[..., half:], pos % GRID_W)], axis=-1)


def depthwise_conv_centred(u, w, b):
    pad = w.shape[0] // 2
    y = lax.conv_general_dilated(u, w[:, None, :].astype(u.dtype), window_strides=(1,),
                                 padding=[(pad, pad)], dimension_numbers=('NWC', 'WIO', 'NWC'),
                                 feature_group_count=u.shape[-1])
    return y + b


def heads(p, off, n, d):
    return p[..., off:off + n * d].reshape(p.shape[:2] + (n, d))


def ssd_chunked(x, dt, A, Bm, Cm, init, return_y):
    b, L = x.shape[:2]
    Q = SSD_CHUNK
    nc = L // Q
    G, R, P, N = SSD_GROUPS, SSD_HPG, SSD_HEAD_DIM, SSD_STATE
    xc = x.reshape(b, nc, Q, G, R, P)
    dtc = dt.reshape(b, nc, Q, G, R)
    Bc = Bm.reshape(b, nc, Q, G, N)
    Cc = Cm.reshape(b, nc, Q, G, N)
    acum = jnp.cumsum(dtc * A, axis=2)
    decay_to_end = jnp.exp(acum[:, :, -1:] - acum)
    chunk_states = jnp.einsum('bcqgn,bcqgrp->bcgrpn', Bc, xc * (decay_to_end * dtc)[..., None])
    chunk_decay = jnp.exp(acum[:, :, -1])

    def step(s, inp):
        st, dec = inp
        return s * dec[..., None, None] + st, s

    final, s_in = lax.scan(step, init, (chunk_states.swapaxes(0, 1), chunk_decay.swapaxes(0, 1)))
    if not return_y:
        return None, final
    s_in = s_in.swapaxes(0, 1)
    idx = jnp.arange(Q)
    lower = idx[:, None] >= idx[None, :]
    seg = acum[:, :, :, None] - acum[:, :, None, :]
    lmat = jnp.exp(jnp.where(lower[:, :, None, None], seg, -jnp.inf))
    cb = jnp.einsum('bcign,bcjgn->bcijg', Cc, Bc)
    m = cb[..., None] * lmat * dtc[:, :, None]
    y_diag = jnp.einsum('bcijgr,bcjgrp->bcigrp', m, xc)
    y_off = jnp.einsum('bcign,bcgrpn->bcigrp', Cc, s_in) * jnp.exp(acum)[..., None]
    return (y_diag + y_off).reshape(b, L, G, R, P), final


def ssd_inputs(p, conv_w, conv_b):
    b, L = p.shape[:2]
    G, R, P, N = SSD_GROUPS, SSD_HPG, SSD_HEAD_DIM, SSD_STATE
    xbc = jax.nn.silu(depthwise_conv_centred(p[..., XBC_OFF:XBC_OFF + SSD_XBC], conv_w, conv_b))
    xs = xbc[..., :SSD_D_INNER].reshape(b, L, G, R, P)
    Bm = xbc[..., SSD_D_INNER:SSD_D_INNER + G * N].reshape(b, L, G, N)
    Cm = xbc[..., SSD_D_INNER + G * N:].reshape(b, L, G, N)
    dt_raw = p[..., DT_OFF:DT_OFF + 2 * SSD_HEADS].reshape(b, L, 2, G, R)
    return xs, Bm, Cm, dt_raw


def ssd_bidirectional(xs, Bm, Cm, dt_raw, dt_bias, A, init_f, init_b, return_y):
    f32 = jnp.float32
    xf, Bf, Cf = xs.astype(f32), Bm.astype(f32), Cm.astype(f32)
    dtb = dt_bias.astype(f32).reshape(2, SSD_GROUPS, SSD_HPG)
    dt_f = jax.nn.softplus(dt_raw[:, :, 0].astype(f32) + dtb[0])
    dt_b = jax.nn.softplus(dt_raw[:, :, 1].astype(f32) + dtb[1])
    rev = lambda t: jnp.flip(t, axis=1)
    y_f, s_f = ssd_chunked(xf, dt_f, A[0], Bf, Cf, init_f, return_y)
    y_b, s_b = ssd_chunked(rev(xf), rev(dt_b), A[1], rev(Bf), rev(Cf), init_b, return_y)
    y = (y_f + rev(y_b)) if return_y else None
    return y, s_f, s_b


def ssd_output(y, xs, d_skip, z, norm_g):
    b, L = y.shape[:2]
    y = y + d_skip.astype(jnp.float32).reshape(SSD_GROUPS, SSD_HPG)[..., None] * xs.astype(jnp.float32)
    y = y.reshape(b, L, SSD_D_INNER).astype(z.dtype)
    return group_rmsnorm(y * jax.nn.silu(z), norm_g, SSD_GROUPS)


def na_latent(q, k, v, kc, vc, rpb):
    b, L, H, d = q.shape
    rows = L // GRID_W
    kh = min(NA_MAX_KH, rows)
    scale = d ** -0.5
    kg = k.reshape(b, rows, GRID_W, H, d)
    vg = v.reshape(b, rows, GRID_W, H, d)
    qr = q.reshape(b, rows, NA_NCB, NA_QCB, H, d).swapaxes(0, 1)
    kstart = [int(np.clip(n * NA_QCB - NA_KW // 2, 0, GRID_W - NA_KCB)) for n in range(NA_NCB)]
    qcol = np.arange(NA_NCB)[:, None, None] * NA_QCB + np.arange(NA_QCB)[None, :, None]
    kcol = np.array(kstart)[:, None, None] + np.arange(NA_KCB)[None, None, :]
    cs = np.clip(qcol - NA_KW // 2, 0, GRID_W - NA_KW)
    col_valid = (kcol >= cs) & (kcol < cs + NA_KW)
    col_idx = np.clip(kcol - qcol + NA_KW - 1, 0, 2 * NA_KW - 2)
    n_loc = kh * NA_KCB

    def row_block(args):
        r, q_r = args
        rs = jnp.clip(r - kh // 2, 0, rows - kh)
        k_r = lax.dynamic_slice_in_dim(kg, rs, kh, axis=1)
        v_r = lax.dynamic_slice_in_dim(vg, rs, kh, axis=1)
        k_b = jnp.stack([k_r[:, :, s:s + NA_KCB] for s in kstart], axis=1)
        v_b = jnp.stack([v_r[:, :, s:s + NA_KCB] for s in kstart], axis=1)
        row_idx = rs + jnp.arange(kh) - r + NA_MAX_KH - 1
        bias = rpb[:, row_idx[None, None, :, None], col_idx[:, :, None, :]]
        s_loc = jnp.einsum('bnqhd,bnijhd->bhnqij', q_r, k_b) * scale + bias
        s_loc = jnp.where(col_valid[:, :, None, :], s_loc, -jnp.inf).reshape(b, H, NA_NCB, NA_QCB, n_loc)
        s_ctx = jnp.einsum('bnqhd,bchd->bhnqc', q_r, kc) * scale
        p = jax.nn.softmax(jnp.concatenate([s_loc, s_ctx], axis=-1).astype(jnp.float32), axis=-1).astype(v.dtype)
        o = (jnp.einsum('bhnqk,bnkhd->bnqhd', p[..., :n_loc], v_b.reshape(b, NA_NCB, n_loc, H, d))
             + jnp.einsum('bhnqc,bchd->bnqhd', p[..., n_loc:], vc))
        return o.reshape(b, GRID_W, H, d)

    out = lax.map(row_block, (jnp.arange(rows), qr))
    return out.swapaxes(0, 1).reshape(b, L, H, d)


def ctx_mha(q, k, v):
    s = jnp.einsum('bqhd,bkhd->bhqk', q, k) * (q.shape[-1] ** -0.5)
    p = jax.nn.softmax(s.astype(jnp.float32), axis=-1).astype(v.dtype)
    return jnp.einsum('bhqk,bkhd->bqhd', p, v)


def swa_latent(q, k, v, kc, vc, sink):
    b, L, G, R, d = q.shape
    W = SWA_WINDOW
    nb = L // W
    scale = d ** -0.5
    kp = jnp.pad(k, ((0, 0), (W, W), (0, 0), (0, 0)))
    vp = jnp.pad(v, ((0, 0), (W, W), (0, 0), (0, 0)))
    qb = q.reshape(b, nb, W, G, R, d).swapaxes(0, 1)
    rel = np.arange(3 * W)[None, :] - W - np.arange(W)[:, None]
    band = np.abs(rel) <= W
    sink_b = jnp.broadcast_to(sink.reshape(1, G, R, 1, 1), (b, G, R, W, 1))

    def block(args):
        i, q_i = args
        k_i = lax.dynamic_slice_in_dim(kp, i * W, 3 * W, axis=1)
        v_i = lax.dynamic_slice_in_dim(vp, i * W, 3 * W, axis=1)
        kpos = (i - 1) * W + jnp.arange(3 * W)
        valid = band & ((kpos >= 0) & (kpos < L))[None, :]
        s_loc = jnp.where(valid, jnp.einsum('bqgrd,bkgd->bgrqk', q_i, k_i) * scale, -jnp.inf)
        s_ctx = jnp.einsum('bqgrd,bkgd->bgrqk', q_i, kc) * scale
        logits = jnp.concatenate([s_loc, s_ctx, sink_b.astype(s_loc.dtype)], axis=-1).astype(jnp.float32)
        p = jax.nn.softmax(logits, axis=-1).astype(v.dtype)
        return (jnp.einsum('bgrqk,bkgd->bqgrd', p[..., :3 * W], v_i)
                + jnp.einsum('bgrqk,bkgd->bqgrd', p[..., 3 * W:-1], vc))

    out = lax.map(block, (jnp.arange(nb), qb))
    return out.swapaxes(0, 1).reshape(b, L, G, R, d)


def ctx_swa_sink(q, k, v, sink):
    b, C, G, R, d = q.shape
    s = jnp.einsum('bqgrd,bkgd->bgrqk', q, k) * (d ** -0.5)
    sk = jnp.broadcast_to(sink.reshape(1, G, R, 1, 1).astype(s.dtype), s.shape[:-1] + (1,))
    p = jax.nn.softmax(jnp.concatenate([s, sk], axis=-1).astype(jnp.float32), axis=-1).astype(v.dtype)
    return jnp.einsum('bgrqk,bkgd->bqgrd', p[..., :-1], v)


def merge_branches(p, y_a, y_b, y_c, w_o_ssd, w_o_na, w_o_swa, w_out):
    b, L = p.shape[:2]
    g = jax.nn.sigmoid(p[..., GATE_OFF:GATE_OFF + N_BRANCHES * D_MODEL].reshape(b, L, N_BRANCHES, D_MODEL))
    m = g[:, :, 0] * (y_a @ w_o_ssd) + g[:, :, 1] * (y_b @ w_o_na) + g[:, :, 2] * (y_c @ w_o_swa)
    return m @ w_out


def sq_relu_ffn(h, w1, w2):
    return jnp.square(jax.nn.relu(h @ w1)) @ w2


def hybrid_layer(xl, xc, c, c_ctx, ada_w, ada_b, norm1_g, norm2_g, w_in, conv_w, conv_b,
                 dt_bias, a_log, ssd_d, ssd_norm_g, na_rpb, swa_sink, w_o_ssd, w_o_na,
                 w_o_swa, w_out, w_ff1, w_ff2, last):
    b, L = xl.shape[:2]
    n_ctx = xc.shape[1]
    mod_l = jnp.split((jax.nn.silu(c) @ ada_w + ada_b)[:, None, :], N_MOD, axis=-1)
    mod_c = jnp.split((jax.nn.silu(c_ctx) @ ada_w + ada_b)[None, None, :], N_MOD, axis=-1)
    hl = rmsnorm(xl, norm1_g) * (1 + mod_l[1]) + mod_l[0]
    hc = rmsnorm(xc, norm1_g) * (1 + mod_c[1]) + mod_c[0]
    pl = hl @ w_in
    pc = hc @ (w_in[:, :CTX_COLS] if last else w_in)

    A = -jnp.exp(a_log.astype(jnp.float32)).reshape(2, SSD_GROUPS, SSD_HPG)
    xs_c, B_c, C_c, dt_c = ssd_inputs(pc, conv_w, conv_b)
    zero = jnp.zeros((b, SSD_GROUPS, SSD_HPG, SSD_HEAD_DIM, SSD_STATE), jnp.float32)
    ya_c, s_f, s_b = ssd_bidirectional(xs_c, B_c, C_c, dt_c, dt_bias, A, zero, zero, not last)
    xs_l, B_l, C_l, dt_l = ssd_inputs(pl, conv_w, conv_b)
    ya_l, _, _ = ssd_bidirectional(xs_l, B_l, C_l, dt_l, dt_bias, A, s_f, s_b, True)
    ya_l = ssd_output(ya_l, xs_l, ssd_d, pl[..., Z_OFF:Z_OFF + SSD_D_INNER], ssd_norm_g)

    na_kc = heads(pc, NA_K_OFF, NA_HEADS, NA_HEAD_DIM)
    na_vc = heads(pc, NA_V_OFF, NA_HEADS, NA_HEAD_DIM)
    yb_l = na_latent(heads(pl, NA_Q_OFF, NA_HEADS, NA_HEAD_DIM), heads(pl, NA_K_OFF, NA_HEADS, NA_HEAD_DIM),
                     heads(pl, NA_V_OFF, NA_HEADS, NA_HEAD_DIM), na_kc, na_vc, na_rpb).reshape(b, L, NA_WIDTH)

    swa_kc = heads(pc, SWA_K_OFF, SWA_KV_HEADS, SWA_HEAD_DIM)
    swa_vc = heads(pc, SWA_V_OFF, SWA_KV_HEADS, SWA_HEAD_DIM)
    swa_q = axial_rope(heads(pl, SWA_Q_OFF, SWA_HEADS, SWA_HEAD_DIM)).reshape(b, L, SWA_KV_HEADS, SWA_GQA, SWA_HEAD_DIM)
    swa_k = axial_rope(heads(pl, SWA_K_OFF, SWA_KV_HEADS, SWA_HEAD_DIM))
    yc_l = swa_latent(swa_q, swa_k, heads(pl, SWA_V_OFF, SWA_KV_HEADS, SWA_HEAD_DIM),
                      swa_kc, swa_vc, swa_sink).reshape(b, L, SWA_Q_WIDTH)

    xl = xl + mod_l[2] * merge_branches(pl, ya_l, yb_l, yc_l, w_o_ssd, w_o_na, w_o_swa, w_out)
    xl = xl + mod_l[5] * sq_relu_ffn(rmsnorm(xl, norm2_g) * (1 + mod_l[4]) + mod_l[3], w_ff1, w_ff2)

    if not last:
        ya_c = ssd_output(ya_c, xs_c, ssd_d, pc[..., Z_OFF:Z_OFF + SSD_D_INNER], ssd_norm_g)
        yb_c = ctx_mha(heads(pc, NA_Q_OFF, NA_HEADS, NA_HEAD_DIM), na_kc, na_vc).reshape(b, n_ctx, NA_WIDTH)
        q_c = heads(pc, SWA_Q_OFF, SWA_HEADS, SWA_HEAD_DIM).reshape(b, n_ctx, SWA_KV_HEADS, SWA_GQA, SWA_HEAD_DIM)
        yc_c = ctx_swa_sink(q_c, swa_kc, swa_vc, swa_sink).reshape(b, n_ctx, SWA_Q_WIDTH)
        xc = xc + mod_c[2] * merge_branches(pc, ya_c, yb_c, yc_c, w_o_ssd, w_o_na, w_o_swa, w_out)
        xc = xc + mod_c[5] * sq_relu_ffn(rmsnorm(xc, norm2_g) * (1 + mod_c[4]) + mod_c[3], w_ff1, w_ff2)
    return xl, xc


def setup_inputs(seed: int = 0) -> dict:
    key = jax.random.key(seed)
    ks = jax.random.split(key, 26)
    f32 = jnp.float32
    nrm = lambda k, shape, s: jax.random.normal(k, shape, f32) * s
    dt0 = jnp.exp(jax.random.uniform(ks[11], (DEPTH, 2, SSD_HEADS), f32)
                  * (math.log(0.1) - math.log(0.001)) + math.log(0.001))
    return {
        "x": nrm(ks[0], (BATCH, SEQ, D_MODEL), 1.0),
        "c": nrm(ks[1], (BATCH, D_MODEL), 1.0),
        "ctx": nrm(ks[2], (BATCH, CTX_LEN, D_MODEL), 1.0),
        "c_ctx": nrm(ks[3], (D_MODEL,), 1.0),
        "ada_w": nrm(ks[4], (DEPTH, D_MODEL, N_MOD * D_MODEL), 0.5 * D_MODEL ** -0.5),
        "ada_b": nrm(ks[5], (DEPTH, N_MOD * D_MODEL), 0.02),
        "norm1_g": 1.0 + nrm(ks[6], (DEPTH, D_MODEL), 0.05),
        "norm2_g": 1.0 + nrm(ks[7], (DEPTH, D_MODEL), 0.05),
        "w_in": nrm(ks[8], (DEPTH, D_MODEL, IN_COLS), D_MODEL ** -0.5),
        "conv_w": nrm(ks[9], (DEPTH, SSD_CONV, SSD_XBC), SSD_CONV ** -0.5),
        "conv_b": nrm(ks[10], (DEPTH, SSD_XBC), 0.02),
        "dt_bias": dt0 + jnp.log(-jnp.expm1(-dt0)),
        "a_log": jnp.log(jax.random.uniform(ks[12], (DEPTH, 2, SSD_HEADS), f32, 1.0, 16.0)),
        "ssd_d": 1.0 + nrm(ks[13], (DEPTH, SSD_HEADS), 0.1),
        "ssd_norm_g": 1.0 + nrm(ks[14], (DEPTH, SSD_D_INNER), 0.05),
        "na_rpb": nrm(ks[15], (DEPTH, NA_HEADS, 2 * NA_MAX_KH - 1, 2 * NA_KW - 1), 0.1),
        "swa_sink": nrm(ks[16], (DEPTH, SWA_HEADS), 0.5),
        "w_o_ssd": nrm(ks[17], (DEPTH, SSD_D_INNER, D_MODEL), SSD_D_INNER ** -0.5),
        "w_o_na": nrm(ks[18], (DEPTH, NA_WIDTH, D_MODEL), NA_WIDTH ** -0.5),
        "w_o_swa": nrm(ks[19], (DEPTH, SWA_Q_WIDTH, D_MODEL), SWA_Q_WIDTH ** -0.5),
        "w_out": nrm(ks[20], (DEPTH, D_MODEL, D_MODEL), D_MODEL ** -0.5),
        "w_ff1": nrm(ks[21], (DEPTH, D_MODEL, D_FF), D_MODEL ** -0.5),
        "w_ff2": nrm(ks[22], (DEPTH, D_FF, D_MODEL), D_FF ** -0.5),
        "final_g": 1.0 + nrm(ks[23], (D_MODEL,), 0.05),
    }


def reference(x, c, ctx, c_ctx, ada_w, ada_b, norm1_g, norm2_g, w_in, conv_w, conv_b, dt_bias,
              a_log, ssd_d, ssd_norm_g, na_rpb, swa_sink, w_o_ssd, w_o_na, w_o_swa, w_out,
              w_ff1, w_ff2, final_g):
    xl, xc = x, ctx
    for l in range(DEPTH):
        xl, xc = hybrid_layer(xl, xc, c, c_ctx, ada_w[l], ada_b[l], norm1_g[l], norm2_g[l], w_in[l],
                              conv_w[l], conv_b[l], dt_bias[l], a_log[l], ssd_d[l], ssd_norm_g[l],
                              na_rpb[l], swa_sink[l], w_o_ssd[l], w_o_na[l], w_o_swa[l], w_out[l],
                              w_ff1[l], w_ff2[l], last=(l == DEPTH - 1))
    return rmsnorm(xl, final_g)
```

```python
import functools
import math

import numpy as np
import jax
import jax.numpy as jnp
from jax import lax
from jax.experimental import pallas as pl
from jax.experimental.pallas import tpu as pltpu

F32 = jnp.float32
BF16 = jnp.bfloat16
HIGHEST = lax.Precision.HIGHEST

D = 1024
L = 2048
NC = 256
T = L + NC
GRID_W = 64
ROWS = L // GRID_W
EPS = 1e-6
N_MOD = 6

SSD_INNER = 2 * D
SSD_HEADS = 32
SSD_GROUPS = 8
SSD_HPG = 4
SSD_STATE = 128
SSD_XBC = SSD_INNER + 2 * SSD_GROUPS * SSD_STATE
Q = 128
NCH = T // Q

NA_HEADS = 16
NA_KH = 8
NA_KW = 16
NA_QROWS = 2
NA_KROWS = 10
NA_NQ = NA_QROWS * GRID_W
NA_NK = NA_KROWS * GRID_W
NA_CASES = 5

SWA_HEADS = 16
SWA_KV = 4
SWA_W = 128
ROPE_BASE = 10000.0
D_FF = 4 * D

R_XBC, R_DT, R_NAK, R_NAV, R_SWAK, R_SWAV, R_Z, R_NAQ, R_SWAQ, R_GATE, R_END = (
    0, 4096, 4160, 5184, 6208, 6464, 6720, 8768, 9792, 10816, 13888)
C_XBC, C_NAK, C_NAV, C_SWAK, C_SWAV, C_Z, C_NAQ, C_SWAQ, C_GATE, NCAT = (
    0, 4096, 5120, 6144, 6656, 7168, 9216, 10240, 11264, 14336)

NEG = -1e30
VMEM_LIMIT = 48 * 1024 * 1024


def _cparams(n_axes):
    return pltpu.CompilerParams(dimension_semantics=("parallel",) * n_axes,
                                vmem_limit_bytes=VMEM_LIMIT)


def _sigmoid(x):
    return 1.0 / (1.0 + jnp.exp(-x))


def _softplus(x):
    return jnp.maximum(x, 0.0) + jnp.log1p(jnp.exp(-jnp.abs(x)))


def _nt(a, b):
    return lax.dot_general(a, b, (((1,), (1,)), ((), ())), preferred_element_type=F32)


def _tn(a, b):
    return lax.dot_general(a, b, (((0,), (0,)), ((), ())), preferred_element_type=F32)


def _dot(a, b):
    return jnp.dot(a, b, preferred_element_type=F32)


def _dot_hi(a, b):
    return jnp.dot(a, b, preferred_element_type=F32, precision=HIGHEST)


def _ada_kernel(c_ref, w_ref, b_ref, o_ref):
    c = c_ref[...]
    s = (c * _sigmoid(c)).astype(BF16)
    o_ref[...] = _dot(s, w_ref[...].astype(BF16)) + b_ref[...]


def _ada_mod(cvec, ada_w, ada_b):
    depth = ada_w.shape[0]
    r = cvec.shape[0]
    return pl.pallas_call(
        _ada_kernel,
        out_shape=jax.ShapeDtypeStruct((depth, r, N_MOD * D), F32),
        grid=(depth, N_MOD),
        in_specs=[pl.BlockSpec((r, D), lambda l, n: (0, 0)),
                  pl.BlockSpec((None, D, D), lambda l, n: (l, 0, n)),
                  pl.BlockSpec((None, 1, D), lambda l, n: (l, 0, n))],
        out_specs=pl.BlockSpec((None, r, D), lambda l, n: (l, 0, n)),
        compiler_params=_cparams(2),
        name="ada_mod",
    )(cvec, ada_w, ada_b.reshape(depth, 1, N_MOD * D))


TR = 256
NRT = T // TR


def _mod_spec(which):
    return pl.BlockSpec((None, None, None, 1, D), lambda b, j: (b, j // (L // TR), which, 0, 0))


def _rms(x, g):
    return x * lax.rsqrt(jnp.mean(x * x, axis=-1, keepdims=True) + EPS) * g


def _normmod_kernel(x_ref, g_ref, sh_ref, sc_ref, o_ref):
    y = _rms(x_ref[...], g_ref[...])
    o_ref[...] = (y * (1.0 + sc_ref[...]) + sh_ref[...]).astype(o_ref.dtype)


def _normmod(xs, g, ms):
    b = xs.shape[0]
    return pl.pallas_call(
        _normmod_kernel,
        out_shape=jax.ShapeDtypeStruct((b, T, D), BF16),
        grid=(b, NRT),
        in_specs=[pl.BlockSpec((None, TR, D), lambda b, j: (b, j, 0)),
                  pl.BlockSpec((1, D), lambda b, j: (0, 0)),
                  _mod_spec(0), _mod_spec(1)],
        out_specs=pl.BlockSpec((None, TR, D), lambda b, j: (b, j, 0)),
        compiler_params=_cparams(2),
        name="normmod",
    )(xs, g.reshape(1, D), ms, ms)


def _mm_kernel(x_ref, w_ref, o_ref):
    o_ref[...] = _dot(x_ref[...], w_ref[...]).astype(o_ref.dtype)


def _matmul(x, w, out_dtype, tn, name):
    b, t, k = x.shape
    n = w.shape[1]
    return pl.pallas_call(
        _mm_kernel,
        out_shape=jax.ShapeDtypeStruct((b, t, n), out_dtype),
        grid=(n // tn, b),
        in_specs=[pl.BlockSpec((None, t, k), lambda n_, b_: (b_, 0, 0)),
                  pl.BlockSpec((k, tn), lambda n_, b_: (0, n_))],
        out_specs=pl.BlockSpec((None, t, tn), lambda n_, b_: (b_, 0, n_)),
        compiler_params=_cparams(2),
        name=name,
    )(x, w)


CW = 512
CONV_RC = 256
PAD_LAT = 8
PAD_CTX = 16 + L


def _conv_kernel(x_ref, w_ref, b_ref, o_ref, pad_ref):
    z8 = jnp.zeros((8, CW), F32)
    pad_ref[0:8, :] = z8
    pad_ref[PAD_LAT + L:PAD_CTX, :] = z8
    pad_ref[PAD_CTX + NC:PAD_CTX + NC + 8, :] = z8
    pad_ref[PAD_LAT:PAD_LAT + L, :] = x_ref[0:L, :].astype(F32)
    pad_ref[PAD_CTX:PAD_CTX + NC, :] = x_ref[L:T, :].astype(F32)
    w = w_ref[...]
    bias = b_ref[...]
    for c in range(T // CONV_RC):
        r0 = c * CONV_RC
        base = (PAD_LAT if r0 < L else PAD_CTX - L) + r0 - 2
        acc = bias + pad_ref[base:base + CONV_RC, :] * w[0:1, :]
        for k in range(1, 5):
            acc = acc + pad_ref[base + k:base + k + CONV_RC, :] * w[k:k + 1, :]
        o_ref[r0:r0 + CONV_RC, :] = (acc * _sigmoid(acc)).astype(o_ref.dtype)


def _conv_silu(p, conv_w, conv_b):
    b = p.shape[0]
    w8 = jnp.concatenate([conv_w, jnp.zeros((3, SSD_XBC), F32)], axis=0)
    return pl.pallas_call(
        _conv_kernel,
        out_shape=jax.ShapeDtypeStruct((b, T, SSD_XBC), BF16),
        grid=(b, SSD_XBC // CW),
        in_specs=[pl.BlockSpec((None, T, CW), lambda b_, n: (b_, 0, n)),
                  pl.BlockSpec((8, CW), lambda b_, n: (0, n)),
                  pl.BlockSpec((1, CW), lambda b_, n: (0, n))],
        out_specs=pl.BlockSpec((None, T, CW), lambda b_, n: (b_, 0, n)),
        scratch_shapes=[pltpu.VMEM((T + 24, CW), F32)],
        compiler_params=_cparams(2),
        name="conv_silu",
    )(p, w8, conv_b.reshape(1, SSD_XBC))


GW = SSD_HPG * 64


def _ssd_kernel(x_ref, b_ref, c_ref, z_ref, dt_ref, biasr_ref, biasc_ref, alogr_ref, alogc_ref,
                dsk_ref, ng_ref, o_ref, sf_ref, sb_ref, sbin_ref):
    g = pl.program_id(1)
    ii = lax.broadcasted_iota(jnp.int32, (Q, Q), 0)
    jj = lax.broadcasted_iota(jnp.int32, (Q, Q), 1)
    tril = jnp.where(ii >= jj, 1.0, 0.0).astype(F32)
    triu = jnp.where(ii <= jj, 1.0, 0.0).astype(F32)
    low = ii > jj
    up = ii < jj
    lo = jj < 64
    tgt = jnp.where(jj < 4, g * 4 + jj, 32 + g * 4 + jj - 4)
    sel = jnp.where((ii == tgt) & (jj < 8), 1.0, 0.0).astype(F32)
    i8 = lax.broadcasted_iota(jnp.int32, (8, Q), 0)
    l8 = lax.broadcasted_iota(jnp.int32, (8, Q), 1)
    tgt8 = jnp.where(i8 < 4, g * 4 + i8, 32 + g * 4 + i8 - 4)
    sel_t = jnp.where(l8 == tgt8, 1.0, 0.0).astype(F32)
    lane1 = lax.broadcasted_iota(jnp.int32, (1, Q), 1)
    lo1 = lane1 < 64
    bias_r = biasr_ref[...]
    bias_c = biasc_ref[...]
    a_r = -jnp.exp(alogr_ref[...])
    a_c = -jnp.exp(alogc_ref[...])

    def colb(v, k):
        return jnp.broadcast_to(v[:, k:k + 1], (Q, Q))

    def pairsel(v, k0, k1):
        return jnp.where(lo, colb(v, k0), colb(v, k1))

    def rowpair(v, k0):
        a = jnp.where(lo1, jnp.broadcast_to(v[:, k0:k0 + 1], (1, Q)), jnp.broadcast_to(v[:, k0 + 1:k0 + 2], (1, Q)))
        b = jnp.where(lo1, jnp.broadcast_to(v[:, k0 + 2:k0 + 3], (1, Q)), jnp.broadcast_to(v[:, k0 + 3:k0 + 4], (1, Q)))
        return jnp.concatenate([a, b], axis=1)

    def scal(t0):
        raw = dt_ref[pl.ds(t0, Q), :]
        dt = _softplus(_dot_hi(raw, sel) + bias_r)
        a = dt * a_r
        cum = _dot_hi(tril, a)
        suf = _dot_hi(triu, a)
        raw_t = lax.dot_general(sel_t, raw, (((1,), (1,)), ((), ())),
                                preferred_element_type=F32, precision=HIGHEST)
        dt_t = _softplus(raw_t + bias_c)
        a_t = dt_t * a_c
        cum_t = _dot_hi(a_t, triu)
        suf_t = _dot_hi(a_t, tril)
        return dt, cum, suf, dt_t, cum_t, suf_t

    def state_update(s_ref, bc, xs, wts, k0, tot):
        xw = jnp.concatenate([xs[:, 0:Q] * pairsel(wts, k0, k0 + 1),
                              xs[:, Q:2 * Q] * pairsel(wts, k0 + 2, k0 + 3)], axis=1)
        upd = _tn(bc, xw.astype(BF16))
        s_ref[...] = s_ref[...] * rowpair(jnp.exp(tot), k0) + upd

    sb_ref[...] = jnp.zeros((SSD_STATE, GW), F32)

    def bwd_body(i, carry):
        c = NCH - 1 - i
        t0 = pl.multiple_of(c * Q, Q)
        sbin_ref[c] = sb_ref[...].astype(BF16)
        dt, cum, suf, _, _, _ = scal(t0)
        xs = x_ref[pl.ds(t0, Q), :].astype(F32)
        tot = suf[0:1, :]
        state_update(sb_ref, b_ref[pl.ds(t0, Q), :], xs, dt * jnp.exp(tot - suf), 4, tot)
        return carry

    lax.fori_loop(0, NCH, bwd_body, 0)

    sf_ref[...] = jnp.zeros((SSD_STATE, GW), F32)
    dsk = dsk_ref[...]
    ng = ng_ref[...]

    def fwd_body(i, carry):
        c = jnp.where(i < NC // Q, i + L // Q, i - NC // Q)
        t0 = pl.multiple_of(c * Q, Q)
        dt, cum, suf, dt_t, cum_t, suf_t = scal(t0)
        bc = b_ref[pl.ds(t0, Q), :]
        cc = c_ref[pl.ds(t0, Q), :]
        xb = x_ref[pl.ds(t0, Q), :]
        xs = xb.astype(F32)
        cb = _nt(cc, bc)
        yoff_f = _dot(cc, sf_ref[...].astype(BF16))
        yoff_b = _dot(cc, sbin_ref[c])
        ecum = jnp.exp(cum)
        esuf = jnp.exp(suf)
        ys = []
        for pr in range(2):
            outs = []
            for hh in range(2):
                r = 2 * pr + hh
                dtf = dt_t[r:r + 1, :]
                dtb = dt_t[4 + r:5 + r, :]
                seg = jnp.where(low, colb(cum, r) - cum_t[r:r + 1, :],
                                jnp.where(up, colb(suf, 4 + r) - suf_t[4 + r:5 + r, :], 0.0))
                dts = jnp.where(low, dtf, jnp.where(up, dtb, dtf + dtb))
                m = (cb * jnp.exp(seg) * dts).astype(BF16)
                outs.append(_dot(m, xb[:, pr * Q:(pr + 1) * Q]))
            y = jnp.where(lo, outs[0], outs[1])
            y = (y + pairsel(ecum, 2 * pr, 2 * pr + 1) * yoff_f[:, pr * Q:(pr + 1) * Q]
                 + pairsel(esuf, 4 + 2 * pr, 5 + 2 * pr) * yoff_b[:, pr * Q:(pr + 1) * Q])
            ys.append(y)
        y = jnp.concatenate(ys, axis=1) + dsk * xs
        zf = z_ref[pl.ds(t0, Q), :].astype(F32)
        u = y * (zf * _sigmoid(zf))
        o_ref[pl.ds(t0, Q), :] = _rms(u, ng).astype(o_ref.dtype)
        tot = cum[Q - 1:Q, :]
        state_update(sf_ref, bc, xs, dt * jnp.exp(tot - cum), 0, tot)
        return carry

    lax.fori_loop(0, NCH, fwd_body, 0)


def _ssd(xa, p, dtr, dt_bias, a_log, ssd_d, ssd_norm_g):
    b = xa.shape[0]
    def slots(v):
        return v.reshape(2, SSD_GROUPS, SSD_HPG).transpose(1, 0, 2).reshape(SSD_GROUPS, 8)
    bias_s = slots(dt_bias)
    alog_s = slots(a_log)
    pad = jnp.zeros((SSD_GROUPS, Q - 8), F32)
    bias_r = jnp.concatenate([bias_s, pad], axis=1).reshape(SSD_GROUPS, 1, Q)
    alog_r = jnp.concatenate([alog_s, pad], axis=1).reshape(SSD_GROUPS, 1, Q)
    bias_c = bias_s.reshape(SSD_GROUPS, 8, 1)
    alog_c = alog_s.reshape(SSD_GROUPS, 8, 1)
    dsk = jnp.repeat(ssd_d, 64).reshape(1, SSD_INNER)
    ng = ssd_norm_g.reshape(1, SSD_INNER)
    xoff, boff, coff, zoff = 0, SSD_INNER // Q, (SSD_INNER + SSD_GROUPS * SSD_STATE) // Q, C_Z // GW
    return pl.pallas_call(
        _ssd_kernel,
        out_shape=jax.ShapeDtypeStruct((b, T, SSD_INNER), BF16),
        grid=(b, SSD_GROUPS),
        in_specs=[pl.BlockSpec((None, T, GW), lambda b_, g: (b_, 0, xoff + g)),
                  pl.BlockSpec((None, T, Q), lambda b_, g: (b_, 0, boff + g)),
                  pl.BlockSpec((None, T, Q), lambda b_, g: (b_, 0, coff + g)),
                  pl.BlockSpec((None, T, GW), lambda b_, g: (b_, 0, zoff + g)),
                  pl.BlockSpec((None, T, Q), lambda b_, g: (b_, 0, 0)),
                  pl.BlockSpec((None, 1, Q), lambda b_, g: (g, 0, 0)),
                  pl.BlockSpec((None, 8, 1), lambda b_, g: (g, 0, 0)),
                  pl.BlockSpec((None, 1, Q), lambda b_, g: (g, 0, 0)),
                  pl.BlockSpec((None, 8, 1), lambda b_, g: (g, 0, 0)),
                  pl.BlockSpec((1, GW), lambda b_, g: (0, g)),
                  pl.BlockSpec((1, GW), lambda b_, g: (0, g))],
        out_specs=pl.BlockSpec((None, T, GW), lambda b_, g: (b_, 0, g)),
        scratch_shapes=[pltpu.VMEM((SSD_STATE, GW), F32),
                        pltpu.VMEM((SSD_STATE, GW), F32),
                        pltpu.VMEM((NCH, SSD_STATE, GW), BF16)],
        compiler_params=_cparams(2),
        name="ssd",
    )(xa, xa, xa, p, dtr, bias_r, bias_c, alog_r, alog_c, dsk, ng)


def _attend(parts, extra_logit=None):
    mx = None
    for s, _ in parts:
        r = jnp.max(s, axis=-1, keepdims=True)
        mx = r if mx is None else jnp.maximum(mx, r)
    if extra_logit is not None:
        mx = jnp.maximum(mx, extra_logit)
    den = None
    acc = None
    for s, v in parts:
        e = jnp.exp(s - mx)
        d = jnp.sum(e, axis=-1, keepdims=True)
        den = d if den is None else den + d
        o = _dot(e.astype(BF16), v)
        acc = o if acc is None else acc + o
    if extra_logit is not None:
        den = den + jnp.exp(extra_logit - mx)
    return acc / den


def _na_kernel(q_ref, k_ref, v_ref, bias_ref, o_ref):
    lane = lax.broadcasted_iota(jnp.int32, (NA_NQ, Q), 1)
    lo = lane < 64
    zero = jnp.zeros((NA_NQ, Q), BF16)
    kc = k_ref[L:T, :]
    vc = v_ref[L:T, :]

    def masked_q(q0):
        q = q_ref[pl.ds(q0, NA_NQ), :] * 0.125
        return jnp.where(lo, q, zero), jnp.where(lo, zero, q)

    def body(j, carry):
        q0 = pl.multiple_of(j * NA_NQ, NA_NQ)
        start = jnp.clip(NA_QROWS * j - NA_KH // 2, 0, ROWS - NA_KROWS)
        koff = pl.multiple_of(start * GRID_W, GRID_W)
        case = ((j >= 1).astype(jnp.int32) + (j >= 2).astype(jnp.int32)
                + (j >= ROWS // NA_QROWS - 2).astype(jnp.int32) + (j >= ROWS // NA_QROWS - 1).astype(jnp.int32))
        kl = k_ref[pl.ds(koff, NA_NK), :]
        vl = v_ref[pl.ds(koff, NA_NK), :]
        outs = []
        for hh, qm in enumerate(masked_q(q0)):
            s_loc = _nt(qm, kl) + bias_ref[case, hh].astype(F32)
            s_ctx = _nt(qm, kc)
            outs.append(_attend([(s_loc, vl), (s_ctx, vc)]))
        o_ref[pl.ds(q0, NA_NQ), :] = jnp.where(lo, outs[0], outs[1]).astype(o_ref.dtype)
        return carry

    lax.fori_loop(0, L // NA_NQ, body, 0)

    for cbk in range(NC // NA_NQ):
        q0 = L + cbk * NA_NQ
        outs = [_attend([(_nt(qm, kc), vc)]) for qm in masked_q(q0)]
        o_ref[q0:q0 + NA_NQ, :] = jnp.where(lo, outs[0], outs[1]).astype(o_ref.dtype)


def _na_bias_table(rpb):
    j_rep = [0, 1, 2, ROWS // NA_QROWS - 2, ROWS // NA_QROWS - 1]
    qr = np.arange(NA_NQ) // GRID_W
    cq = np.arange(NA_NQ) % GRID_W
    ki = np.arange(NA_NK) // GRID_W
    ck = np.arange(NA_NK) % GRID_W
    ridx, valid = [], []
    cs = np.clip(cq - NA_KW // 2, 0, GRID_W - NA_KW)
    col_ok = (ck[None, :] >= cs[:, None]) & (ck[None, :] < cs[:, None] + NA_KW)
    cidx = np.clip(ck[None, :] - cq[:, None] + NA_KW - 1, 0, 2 * NA_KW - 2)
    for j in j_rep:
        start = int(np.clip(NA_QROWS * j - NA_KH // 2, 0, ROWS - NA_KROWS))
        r = NA_QROWS * j + qr
        kr = start + ki
        rs = np.clip(r - NA_KH // 2, 0, ROWS - NA_KH)
        row_ok = (kr[None, :] >= rs[:, None]) & (kr[None, :] < rs[:, None] + NA_KH)
        ridx.append(np.clip(kr[None, :] - r[:, None] + NA_KH - 1, 0, 2 * NA_KH - 2))
        valid.append(row_ok & col_ok)
    ridx = np.stack(ridx)
    valid = np.stack(valid)
    cidx = np.broadcast_to(cidx, ridx.shape)
    tab = rpb[:, ridx, cidx]
    tab = jnp.where(valid[None], tab, NEG).astype(BF16)
    return tab.reshape(NA_HEADS // 2, 2, NA_CASES, NA_NQ, NA_NK).transpose(0, 2, 1, 3, 4)


def _na(p, bias_tab):
    b = p.shape[0]
    qo, ko, vo = C_NAQ // Q, C_NAK // Q, C_NAV // Q
    return pl.pallas_call(
        _na_kernel,
        out_shape=jax.ShapeDtypeStruct((b, T, D), BF16),
        grid=(b, NA_HEADS // 2),
        in_specs=[pl.BlockSpec((None, T, Q), lambda b_, h: (b_, 0, qo + h)),
                  pl.BlockSpec((None, T, Q), lambda b_, h: (b_, 0, ko + h)),
                  pl.BlockSpec((None, T, Q), lambda b_, h: (b_, 0, vo + h)),
                  pl.BlockSpec((None, NA_CASES, 2, NA_NQ, NA_NK), lambda b_, h: (h, 0, 0, 0, 0))],
        out_specs=pl.BlockSpec((None, T, Q), lambda b_, h: (b_, 0, h)),
        compiler_params=_cparams(2),
        name="na_attn",
    )(p, p, p, bias_tab)


SWA_KWIN = 3 * SWA_W
SWA_QB = 128
SWA_STACK = 4 * SWA_QB
ROPE_RC = 256


def _swap16(t, lane):
    a = pltpu.roll(t, 112, axis=1)
    b = pltpu.roll(t, 16, axis=1)
    return jnp.where((lane % 32) < 16, a, b)


def _swa_kernel(sink_ref, q_ref, k_ref, v_ref, cos_ref, sin_ref, o_ref, krot_ref):
    kp = pl.program_id(1)
    lane = lax.broadcasted_iota(jnp.int32, (SWA_QB, Q), 1)
    lo = lane < 64
    zero = jnp.zeros((SWA_QB, Q), BF16)
    row = lax.broadcasted_iota(jnp.int32, (SWA_STACK, 1), 0)
    rel0 = (lax.broadcasted_iota(jnp.int32, (SWA_STACK, SWA_KWIN), 1)
            - lax.broadcasted_iota(jnp.int32, (SWA_STACK, SWA_KWIN), 0) % SWA_QB)

    def rope(t, r0, n):
        lane_n = lax.broadcasted_iota(jnp.int32, (n, Q), 1)
        return t * cos_ref[pl.ds(r0, n), :] + _swap16(t, lane_n) * sin_ref[pl.ds(r0, n), :]

    def krot_body(i, carry):
        r0 = pl.multiple_of(i * ROPE_RC, ROPE_RC)
        for gl in range(2):
            t = k_ref[pl.ds(r0, ROPE_RC), gl * Q:(gl + 1) * Q].astype(F32)
            krot_ref[pl.ds(r0, ROPE_RC), gl * Q:(gl + 1) * Q] = rope(t, r0, ROPE_RC).astype(BF16)
        return carry

    lax.fori_loop(0, L // ROPE_RC, krot_body, 0)

    def stack_q(qa, qb):
        return jnp.concatenate([jnp.where(lo, qa, zero), jnp.where(lo, zero, qa),
                                jnp.where(lo, qb, zero), jnp.where(lo, zero, qb)], axis=0)

    def sink_col(gl):
        base = kp * 8 + gl * 4
        return jnp.where(row < SWA_QB, sink_ref[base],
                         jnp.where(row < 2 * SWA_QB, sink_ref[base + 1],
                                   jnp.where(row < 3 * SWA_QB, sink_ref[base + 2], sink_ref[base + 3])))

    def unstack(o):
        oa = jnp.where(lo, o[0:SWA_QB], o[SWA_QB:2 * SWA_QB])
        ob = jnp.where(lo, o[2 * SWA_QB:3 * SWA_QB], o[3 * SWA_QB:4 * SWA_QB])
        return jnp.concatenate([oa, ob], axis=1)

    def body(i, carry):
        q0 = pl.multiple_of(i * SWA_QB, SWA_QB)
        ws = pl.multiple_of(jnp.clip((i - 1) * SWA_W, 0, L - SWA_KWIN), SWA_W)
        valid = jnp.abs(rel0 + (ws - q0)) <= SWA_W
        for gl in range(2):
            qa = rope(q_ref[pl.ds(q0, SWA_QB), gl * 2 * Q:gl * 2 * Q + Q].astype(F32), q0, SWA_QB) * 0.125
            qb = rope(q_ref[pl.ds(q0, SWA_QB), gl * 2 * Q + Q:(gl + 1) * 2 * Q].astype(F32), q0, SWA_QB) * 0.125
            qs = stack_q(qa.astype(BF16), qb.astype(BF16))
            kw = krot_ref[pl.ds(ws, SWA_KWIN), gl * Q:(gl + 1) * Q]
            vw = v_ref[pl.ds(ws, SWA_KWIN), gl * Q:(gl + 1) * Q]
            kc = k_ref[L:T, gl * Q:(gl + 1) * Q]
            vc = v_ref[L:T, gl * Q:(gl + 1) * Q]
            s_loc = jnp.where(valid, _nt(qs, kw), NEG)
            s_ctx = _nt(qs, kc)
            o = _attend([(s_loc, vw), (s_ctx, vc)], sink_col(gl))
            o_ref[pl.ds(q0, SWA_QB), gl * 2 * Q:(gl + 1) * 2 * Q] = unstack(o).astype(o_ref.dtype)
        return carry

    lax.fori_loop(0, L // SWA_QB, body, 0)

    for cbk in range(NC // SWA_QB):
        q0 = L + cbk * SWA_QB
        for gl in range(2):
            qa = q_ref[q0:q0 + SWA_QB, gl * 2 * Q:gl * 2 * Q + Q] * 0.125
            qb = q_ref[q0:q0 + SWA_QB, gl * 2 * Q + Q:(gl + 1) * 2 * Q] * 0.125
            qs = stack_q(qa, qb)
            kc = k_ref[L:T, gl * Q:(gl + 1) * Q]
            vc = v_ref[L:T, gl * Q:(gl + 1) * Q]
            o = _attend([(_nt(qs, kc), vc)], sink_col(gl))
            o_ref[q0:q0 + SWA_QB, gl * 2 * Q:(gl + 1) * 2 * Q] = unstack(o).astype(o_ref.dtype)


def _rope_tables():
    pos = np.arange(L)
    lane = np.arange(Q) % 64
    inv = ROPE_BASE ** (-(lane % 16).astype(np.float64) / 16.0)
    p = np.where(lane[None, :] < 32, (pos // GRID_W)[:, None], (pos % GRID_W)[:, None]).astype(np.float64)
    ang = p * inv[None, :]
    sign = np.where((lane % 32) < 16, -1.0, 1.0)[None, :]
    return jnp.asarray(np.cos(ang), F32), jnp.asarray(np.sin(ang) * sign, F32)


def _swa(p, sink, cos_t, sin_t):
    b = p.shape[0]
    qo, ko, vo = C_SWAQ // 512, C_SWAK // 256, C_SWAV // 256
    return pl.pallas_call(
        _swa_kernel,
        out_shape=jax.ShapeDtypeStruct((b, T, D), BF16),
        grid=(b, SWA_KV // 2),
        in_specs=[pl.BlockSpec(memory_space=pltpu.SMEM),
                  pl.BlockSpec((None, T, 512), lambda b_, h: (b_, 0, qo + h)),
                  pl.BlockSpec((None, T, 256), lambda b_, h: (b_, 0, ko + h)),
                  pl.BlockSpec((None, T, 256), lambda b_, h: (b_, 0, vo + h)),
                  pl.BlockSpec((L, Q), lambda b_, h: (0, 0)),
                  pl.BlockSpec((L, Q), lambda b_, h: (0, 0))],
        out_specs=pl.BlockSpec((None, T, 512), lambda b_, h: (b_, 0, h)),
        scratch_shapes=[pltpu.VMEM((L, 2 * Q), BF16)],
        compiler_params=_cparams(2),
        name="swa_attn",
    )(sink, p, p, p, cos_t, sin_t)


def _merge_kernel(x_ref, ya_ref, yb_ref, yc_ref, g0_ref, g1_ref, g2_ref, mg_ref,
                  wa_ref, wb_ref, wc_ref, wo_ref, o_ref):
    m = (_sigmoid(g0_ref[...].astype(F32)) * _dot(ya_ref[...], wa_ref[...])
         + _sigmoid(g1_ref[...].astype(F32)) * _dot(yb_ref[...], wb_ref[...])
         + _sigmoid(g2_ref[...].astype(F32)) * _dot(yc_ref[...], wc_ref[...]))
    o_ref[...] = x_ref[...] + mg_ref[...] * _dot(m.astype(BF16), wo_ref[...])


def _const_spec(shape):
    return pl.BlockSpec(shape, lambda b, j: (0,) * len(shape))


def _merge(xs, ya, yb, yc, p, ms, wa, wb, wc, wo):
    b = xs.shape[0]
    go = C_GATE // D
    row = lambda w: pl.BlockSpec((None, TR, w), lambda b_, j: (b_, j, 0))
    gate = lambda k: pl.BlockSpec((None, TR, D), lambda b_, j: (b_, j, go + k))
    return pl.pallas_call(
        _merge_kernel,
        out_shape=jax.ShapeDtypeStruct((b, T, D), F32),
        grid=(b, NRT),
        in_specs=[row(D), row(SSD_INNER), row(D), row(D), gate(0), gate(1), gate(2), _mod_spec(2),
                  _const_spec((SSD_INNER, D)), _const_spec((D, D)), _const_spec((D, D)), _const_spec((D, D))],
        out_specs=row(D),
        compiler_params=_cparams(2),
        name="merge",
    )(xs, ya, yb, yc, p, p, p, ms, wa, wb, wc, wo)


FF_CH = 1024


def _ffn_kernel(x_ref, g_ref, sh_ref, sc_ref, mg_ref, w1_ref, w2_ref, o_ref):
    x = x_ref[...]
    h = (_rms(x, g_ref[...]) * (1.0 + sc_ref[...]) + sh_ref[...]).astype(BF16)
    acc = None
    for kf in range(D_FF // FF_CH):
        a = jnp.maximum(_dot(h, w1_ref[:, kf * FF_CH:(kf + 1) * FF_CH]), 0.0)
        o = _dot((a * a).astype(BF16), w2_ref[kf * FF_CH:(kf + 1) * FF_CH, :])
        acc = o if acc is None else acc + o
    o_ref[...] = x + mg_ref[...] * acc


def _ffn(xs, g, ms, w1, w2):
    b = xs.shape[0]
    row = pl.BlockSpec((None, TR, D), lambda b_, j: (b_, j, 0))
    return pl.pallas_call(
        _ffn_kernel,
        out_shape=jax.ShapeDtypeStruct((b, T, D), F32),
        grid=(b, NRT),
        in_specs=[row, _const_spec((1, D)), _mod_spec(3), _mod_spec(4), _mod_spec(5),
                  _const_spec((D, D_FF)), _const_spec((D_FF, D))],
        out_specs=row,
        compiler_params=_cparams(2),
        name="ffn",
    )(xs, g.reshape(1, D), ms, ms, ms, w1, w2)


def _final_kernel(x_ref, g_ref, o_ref):
    o_ref[...] = _rms(x_ref[...], g_ref[...])


def _final_norm(xs, g):
    b = xs.shape[0]
    return pl.pallas_call(
        _final_kernel,
        out_shape=jax.ShapeDtypeStruct((b, L, D), F32),
        grid=(b, L // TR),
        in_specs=[pl.BlockSpec((None, TR, D), lambda b_, j: (b_, j, 0)), _const_spec((1, D))],
        out_specs=pl.BlockSpec((None, TR, D), lambda b_, j: (b_, j, 0)),
        compiler_params=_cparams(2),
        name="final_norm",
    )(xs, g.reshape(1, D))


def _prep_w_in(w):
    def dup(a):
        a = a.reshape(D, SWA_KV, 1, 64)
        return jnp.concatenate([a, a], axis=2).reshape(D, SWA_KV * 128)
    cat = jnp.concatenate([
        w[:, R_XBC:R_DT], w[:, R_NAK:R_NAV], w[:, R_NAV:R_SWAK], dup(w[:, R_SWAK:R_SWAV]),
        dup(w[:, R_SWAV:R_Z]), w[:, R_Z:R_NAQ], w[:, R_NAQ:R_SWAQ], w[:, R_SWAQ:R_GATE],
        w[:, R_GATE:R_END]], axis=1).astype(BF16)
    wdt = jnp.concatenate([w[:, R_DT:R_NAK], jnp.zeros((D, Q - 2 * SSD_HEADS), F32)], axis=1).astype(BF16)
    return cat, wdt


def kernel(x, c, ctx, c_ctx, ada_w, ada_b, norm1_g, norm2_g, w_in, conv_w, conv_b, dt_bias, a_log, ssd_d,
           ssd_norm_g, na_rpb, swa_sink, w_o_ssd, w_o_na, w_o_swa, w_out, w_ff1, w_ff2, final_g):
    b = x.shape[0]
    depth = ada_w.shape[0]
    xs = jnp.concatenate([x, ctx], axis=1)
    nrow = -(-(b + 1) // 8) * 8
    cvec = jnp.concatenate([c, c_ctx[None, :], jnp.zeros((nrow - b - 1, D), F32)], axis=0)
    mod = _ada_mod(cvec, ada_w, ada_b)
    lat = mod[:, :b].reshape(depth, b, 1, N_MOD, 1, D)
    cx = jnp.broadcast_to(mod[:, b].reshape(depth, 1, 1, N_MOD, 1, D), lat.shape)
    ms_all = jnp.concatenate([lat, cx], axis=2)
    cos_t, sin_t = _rope_tables()

    for l in range(depth):
        ms = ms_all[l]
        wcat, wdt = _prep_w_in(w_in[l])
        h = _normmod(xs, norm1_g[l], ms)
        p = _matmul(h, wcat, BF16, 1024, "in_proj")
        dtr = _matmul(h, wdt, F32, Q, "dt_proj")
        xa = _conv_silu(p, conv_w[l], conv_b[l])
        ya = _ssd(xa, p, dtr, dt_bias[l], a_log[l], ssd_d[l], ssd_norm_g[l])
        yb = _na(p, _na_bias_table(na_rpb[l]))
        yc = _swa(p, swa_sink[l], cos_t, sin_t)
        xs = _merge(xs, ya, yb, yc, p, ms, w_o_ssd[l].astype(BF16), w_o_na[l].astype(BF16),
                    w_o_swa[l].astype(BF16), w_out[l].astype(BF16))
        xs = _ffn(xs, norm2_g[l], ms, w_ff1[l].astype(BF16), w_ff2[l].astype(BF16))
    return _final_norm(xs, final_g)
```

```python
import functools
import math

import numpy as np
import jax
import jax.numpy as jnp
from jax import lax
from jax.experimental import pallas as pl
from jax.experimental.pallas import tpu as pltpu

F32 = jnp.float32
BF16 = jnp.bfloat16
HIGHEST = lax.Precision.HIGHEST

D = 1024
L = 2048
NC = 256
T = L + NC
GRID_W = 64
ROWS = L // GRID_W
EPS = 1e-6
N_MOD = 6

SSD_INNER = 2 * D
SSD_HEADS = 32
SSD_GROUPS = 8
SSD_HPG = 4
SSD_STATE = 128
SSD_XBC = SSD_INNER + 2 * SSD_GROUPS * SSD_STATE
Q = 128
NCH = T // Q

NA_HEADS = 16
NA_KH = 8
NA_KW = 16
NA_QROWS = 2
NA_KROWS = 10
NA_NQ = NA_QROWS * GRID_W
NA_NK = NA_KROWS * GRID_W
NA_CASES = 5

SWA_HEADS = 16
SWA_KV = 4
SWA_W = 128
ROPE_BASE = 10000.0
D_FF = 4 * D

R_XBC, R_DT, R_NAK, R_NAV, R_SWAK, R_SWAV, R_Z, R_NAQ, R_SWAQ, R_GATE, R_END = (
    0, 4096, 4160, 5184, 6208, 6464, 6720, 8768, 9792, 10816, 13888)
C_XBC, C_NAK, C_NAV, C_SWAK, C_SWAV, C_Z, C_NAQ, C_SWAQ, C_GATE, NCAT = (
    0, 4096, 5120, 6144, 6656, 7168, 9216, 10240, 11264, 14336)

NEG = -1e30
VMEM_LIMIT = 48 * 1024 * 1024


def _cparams(n_axes):
    return pltpu.CompilerParams(dimension_semantics=("parallel",) * n_axes,
                                vmem_limit_bytes=VMEM_LIMIT)


def _sigmoid(x):
    return 1.0 / (1.0 + jnp.exp(-x))


def _softplus(x):
    return jnp.maximum(x, 0.0) + jnp.log1p(jnp.exp(-jnp.abs(x)))


def _nt(a, b):
    return lax.dot_general(a, b, (((1,), (1,)), ((), ())), preferred_element_type=F32)


def _tn(a, b):
    return lax.dot_general(a, b, (((0,), (0,)), ((), ())), preferred_element_type=F32)


def _dot(a, b):
    return jnp.dot(a, b, preferred_element_type=F32)


def _dot_hi(a, b):
    return jnp.dot(a, b, preferred_element_type=F32, precision=HIGHEST)


def _ada_kernel(c_ref, w_ref, b_ref, o_ref):
    c = c_ref[...]
    s = (c * _sigmoid(c)).astype(BF16)
    o_ref[...] = _dot(s, w_ref[...].astype(BF16)) + b_ref[...]


def _ada_mod(cvec, ada_w, ada_b):
    depth = ada_w.shape[0]
    r = cvec.shape[0]
    return pl.pallas_call(
        _ada_kernel,
        out_shape=jax.ShapeDtypeStruct((depth, r, N_MOD * D), F32),
        grid=(depth, N_MOD),
        in_specs=[pl.BlockSpec((r, D), lambda l, n: (0, 0)),
                  pl.BlockSpec((None, D, D), lambda l, n: (l, 0, n)),
                  pl.BlockSpec((None, 1, D), lambda l, n: (l, 0, n))],
        out_specs=pl.BlockSpec((None, r, D), lambda l, n: (l, 0, n)),
        compiler_params=_cparams(2),
        name="ada_mod",
    )(cvec, ada_w, ada_b.reshape(depth, 1, N_MOD * D))


TR = 256
NRT = T // TR


def _mod_spec(which):
    return pl.BlockSpec((None, None, None, 1, D), lambda b, j: (b, j // (L // TR), which, 0, 0))


def _rms(x, g):
    return x * lax.rsqrt(jnp.mean(x * x, axis=-1, keepdims=True) + EPS) * g


def _normmod_kernel(x_ref, g_ref, sh_ref, sc_ref, o_ref):
    y = _rms(x_ref[...], g_ref[...])
    o_ref[...] = (y * (1.0 + sc_ref[...]) + sh_ref[...]).astype(o_ref.dtype)


def _normmod(xs, g, ms):
    b = xs.shape[0]
    return pl.pallas_call(
        _normmod_kernel,
        out_shape=jax.ShapeDtypeStruct((b, T, D), BF16),
        grid=(b, NRT),
        in_specs=[pl.BlockSpec((None, TR, D), lambda b, j: (b, j, 0)),
                  pl.BlockSpec((1, D), lambda b, j: (0, 0)),
                  _mod_spec(0), _mod_spec(1)],
        out_specs=pl.BlockSpec((None, TR, D), lambda b, j: (b, j, 0)),
        compiler_params=_cparams(2),
        name="normmod",
    )(xs, g.reshape(1, D), ms, ms)


def _mm_kernel(x_ref, w_ref, o_ref):
    o_ref[...] = _dot(x_ref[...], w_ref[...]).astype(o_ref.dtype)


def _matmul(x, w, out_dtype, tn, name):
    b, t, k = x.shape
    n = w.shape[1]
    return pl.pallas_call(
        _mm_kernel,
        out_shape=jax.ShapeDtypeStruct((b, t, n), out_dtype),
        grid=(n // tn, b),
        in_specs=[pl.BlockSpec((None, t, k), lambda n_, b_: (b_, 0, 0)),
                  pl.BlockSpec((k, tn), lambda n_, b_: (0, n_))],
        out_specs=pl.BlockSpec((None, t, tn), lambda n_, b_: (b_, 0, n_)),
        compiler_params=_cparams(2),
        name=name,
    )(x, w)


CW = 512
CONV_RC = 256
PAD_LAT = 8
PAD_CTX = 16 + L


def _conv_kernel(x_ref, w_ref, b_ref, o_ref, pad_ref):
    z8 = jnp.zeros((8, CW), F32)
    pad_ref[0:8, :] = z8
    pad_ref[PAD_LAT + L:PAD_CTX, :] = z8
    pad_ref[PAD_CTX + NC:PAD_CTX + NC + 8, :] = z8
    pad_ref[PAD_LAT:PAD_LAT + L, :] = x_ref[0:L, :].astype(F32)
    pad_ref[PAD_CTX:PAD_CTX + NC, :] = x_ref[L:T, :].astype(F32)
    w = w_ref[...]
    bias = b_ref[...]
    for c in range(T // CONV_RC):
        r0 = c * CONV_RC
        base = (PAD_LAT if r0 < L else PAD_CTX - L) + r0 - 2
        acc = bias + pad_ref[base:base + CONV_RC, :] * w[0:1, :]
        for k in range(1, 5):
            acc = acc + pad_ref[base + k:base + k + CONV_RC, :] * w[k:k + 1, :]
        o_ref[r0:r0 + CONV_RC, :] = (acc * _sigmoid(acc)).astype(o_ref.dtype)


def _conv_silu(p, conv_w, conv_b):
    b = p.shape[0]
    w8 = jnp.concatenate([conv_w, jnp.zeros((3, SSD_XBC), F32)], axis=0)
    return pl.pallas_call(
        _conv_kernel,
        out_shape=jax.ShapeDtypeStruct((b, T, SSD_XBC), BF16),
        grid=(b, SSD_XBC // CW),
        in_specs=[pl.BlockSpec((None, T, CW), lambda b_, n: (b_, 0, n)),
                  pl.BlockSpec((8, CW), lambda b_, n: (0, n)),
                  pl.BlockSpec((1, CW), lambda b_, n: (0, n))],
        out_specs=pl.BlockSpec((None, T, CW), lambda b_, n: (b_, 0, n)),
        scratch_shapes=[pltpu.VMEM((T + 24, CW), F32)],
        compiler_params=_cparams(2),
        name="conv_silu",
    )(p, w8, conv_b.reshape(1, SSD_XBC))


GW = SSD_HPG * 64


TS = 16


def _split3(x):
    hi = x.astype(BF16)
    r = x - hi.astype(F32)
    mid = r.astype(BF16)
    lo = (r - mid.astype(F32)).astype(BF16)
    return hi, mid, lo


def _dot_parts(parts, const, left):
    out = None
    for part in parts:
        o = _dot(const, part) if left else _dot(part, const)
        out = o if out is None else out + o
    return out


def _ssd_kernel(x_ref, b_ref, c_ref, z_ref, dt_ref, biasr_ref, biasc_ref, alogr_ref, alogc_ref,
                dsk_ref, ng_ref, o_ref, acc_ref, acct_ref, dtt_ref, ec_ref, upd_ref, dec_ref, sin_ref):
    g = pl.program_id(1)
    ii = lax.broadcasted_iota(jnp.int32, (Q, Q), 0)
    jj = lax.broadcasted_iota(jnp.int32, (Q, Q), 1)
    tril = jnp.where(ii >= jj, 1.0, 0.0).astype(BF16)
    triu = jnp.where(ii <= jj, 1.0, 0.0).astype(BF16)
    low = ii > jj
    up = ii < jj
    lo = jj < 64
    tgt = jnp.where(jj < 4, g * 4 + jj, 32 + g * 4 + jj - 4)
    sel = jnp.where((ii == tgt) & (jj < 8), 1.0, 0.0).astype(BF16)
    it = lax.broadcasted_iota(jnp.int32, (TS, Q), 0)
    lt = lax.broadcasted_iota(jnp.int32, (TS, Q), 1)
    tgt_t = jnp.where(it < 4, g * 4 + it, 32 + g * 4 + it - 4)
    sel_t = jnp.where((lt == tgt_t) & (it < 8), 1.0, 0.0).astype(BF16)
    ek = lax.broadcasted_iota(jnp.int32, (Q, 2 * GW), 0)
    ech = lax.broadcasted_iota(jnp.int32, (Q, 2 * GW), 1)
    esel = jnp.where(ek == ech // 64, 1.0, 0.0).astype(BF16)
    isf = lax.broadcasted_iota(jnp.int32, (1, Q), 1) < 4
    isf_t = it < 4
    bias_r = biasr_ref[...]
    bias_c = biasc_ref[...]
    a_r = -jnp.exp(alogr_ref[...])
    a_c = -jnp.exp(alogc_ref[...])
    zero_x = jnp.zeros((Q, Q), BF16)

    def prep(c, carry):
        t0 = pl.multiple_of(c * Q, Q)
        raw3 = _split3(dt_ref[pl.ds(t0, Q), :])
        dt = _softplus(_dot_parts(raw3, sel, False) + bias_r)
        a3 = _split3(dt * a_r)
        cum = _dot_parts(a3, tril, True)
        suf = _dot_parts(a3, triu, True)
        acc = jnp.where(isf, cum, suf)
        tot = jnp.where(isf, cum[Q - 1:Q, :], suf[0:1, :])
        wfull = _dot_parts(_split3(dt * jnp.exp(tot - acc))[:2], esel, False)
        xs = x_ref[pl.ds(t0, Q), :].astype(F32)
        xw = (jnp.concatenate([xs, xs], axis=1) * wfull).astype(BF16)
        upd_ref[c] = _tn(b_ref[pl.ds(t0, Q), :], xw)
        ec_ref[pl.ds(t0, Q), :] = _dot_parts(_split3(jnp.exp(acc)), esel, False)
        acc_ref[pl.ds(t0, Q), :] = acc
        dec_ref[c] = jnp.exp(_dot_parts(_split3(jnp.broadcast_to(tot, (8, Q))), esel, False))
        raw_t = None
        for part in raw3:
            o = _nt(sel_t, part)
            raw_t = o if raw_t is None else raw_t + o
        dt_t = _softplus(raw_t + bias_c)
        at3 = _split3(dt_t * a_c)
        acct_ref[:, pl.ds(t0, Q)] = jnp.where(isf_t, _dot_parts(at3, triu, False), _dot_parts(at3, tril, False))
        dtt_ref[:, pl.ds(t0, Q)] = dt_t
        return carry

    lax.fori_loop(0, NCH, prep, 0, unroll=2)

    s = jnp.zeros((SSD_STATE, GW), F32)
    for c in list(range(L // Q, NCH)) + list(range(L // Q)):
        sin_ref[c, :, 0:GW] = s.astype(BF16)
        s = s * dec_ref[c, 0:1, 0:GW] + upd_ref[c, :, 0:GW]
    s = jnp.zeros((SSD_STATE, GW), F32)
    for c in reversed(range(NCH)):
        sin_ref[c, :, GW:2 * GW] = s.astype(BF16)
        s = s * dec_ref[c, 0:1, GW:2 * GW] + upd_ref[c, :, GW:2 * GW]

    dsk = dsk_ref[...]
    ng = ng_ref[...]

    def colb(v, k):
        return jnp.broadcast_to(v[:, k:k + 1], (Q, Q))

    def emit(c, carry):
        t0 = pl.multiple_of(c * Q, Q)
        bc = b_ref[pl.ds(t0, Q), :]
        cc = c_ref[pl.ds(t0, Q), :]
        xb = x_ref[pl.ds(t0, Q), :]
        acc = acc_ref[pl.ds(t0, Q), :]
        acc_t = acct_ref[:, pl.ds(t0, Q)]
        dt_t = dtt_ref[:, pl.ds(t0, Q)]
        ec = ec_ref[pl.ds(t0, Q), :]
        cb = _nt(cc, bc)
        yoff = _dot(cc, sin_ref[c])
        ys = []
        for pr in range(2):
            ms = []
            for hh in range(2):
                r = 2 * pr + hh
                dtf = dt_t[r:r + 1, :]
                dtb = dt_t[4 + r:5 + r, :]
                seg = jnp.where(low, colb(acc, r) - acc_t[r:r + 1, :],
                                jnp.where(up, colb(acc, 4 + r) - acc_t[4 + r:5 + r, :], 0.0))
                dts = jnp.where(low, dtf, jnp.where(up, dtb, dtf + dtb))
                ms.append((cb * jnp.exp(seg) * dts).astype(BF16))
            xp = xb[:, pr * Q:(pr + 1) * Q]
            xstack = jnp.concatenate([jnp.where(lo, xp, zero_x), jnp.where(lo, zero_x, xp)], axis=0)
            y = _dot(jnp.concatenate(ms, axis=1), xstack)
            y = (y + ec[:, pr * Q:(pr + 1) * Q] * yoff[:, pr * Q:(pr + 1) * Q]
                 + ec[:, GW + pr * Q:GW + (pr + 1) * Q] * yoff[:, GW + pr * Q:GW + (pr + 1) * Q])
            ys.append(y)
        y = jnp.concatenate(ys, axis=1) + dsk * xb.astype(F32)
        zf = z_ref[pl.ds(t0, Q), :].astype(F32)
        u = y * (zf * _sigmoid(zf))
        o_ref[pl.ds(t0, Q), :] = _rms(u, ng).astype(o_ref.dtype)
        return carry

    lax.fori_loop(0, NCH, emit, 0, unroll=2)


def _ssd(xa, p, dtr, dt_bias, a_log, ssd_d, ssd_norm_g):
    b = xa.shape[0]
    def slots(v):
        return v.reshape(2, SSD_GROUPS, SSD_HPG).transpose(1, 0, 2).reshape(SSD_GROUPS, 8)
    bias_s = slots(dt_bias)
    alog_s = slots(a_log)
    bias_r = jnp.pad(bias_s, ((0, 0), (0, Q - 8))).reshape(SSD_GROUPS, 1, Q)
    alog_r = jnp.pad(alog_s, ((0, 0), (0, Q - 8))).reshape(SSD_GROUPS, 1, Q)
    bias_c = jnp.pad(bias_s, ((0, 0), (0, TS - 8))).reshape(SSD_GROUPS, TS, 1)
    alog_c = jnp.pad(alog_s, ((0, 0), (0, TS - 8))).reshape(SSD_GROUPS, TS, 1)
    dsk = jnp.repeat(ssd_d, 64).reshape(1, SSD_INNER)
    ng = ssd_norm_g.reshape(1, SSD_INNER)
    xoff, boff, coff, zoff = 0, SSD_INNER // Q, (SSD_INNER + SSD_GROUPS * SSD_STATE) // Q, C_Z // GW
    return pl.pallas_call(
        _ssd_kernel,
        out_shape=jax.ShapeDtypeStruct((b, T, SSD_INNER), BF16),
        grid=(b, SSD_GROUPS),
        in_specs=[pl.BlockSpec((None, T, GW), lambda b_, g: (b_, 0, xoff + g)),
                  pl.BlockSpec((None, T, Q), lambda b_, g: (b_, 0, boff + g)),
                  pl.BlockSpec((None, T, Q), lambda b_, g: (b_, 0, coff + g)),
                  pl.BlockSpec((None, T, GW), lambda b_, g: (b_, 0, zoff + g)),
                  pl.BlockSpec((None, T, Q), lambda b_, g: (b_, 0, 0)),
                  pl.BlockSpec((None, 1, Q), lambda b_, g: (g, 0, 0)),
                  pl.BlockSpec((None, TS, 1), lambda b_, g: (g, 0, 0)),
                  pl.BlockSpec((None, 1, Q), lambda b_, g: (g, 0, 0)),
                  pl.BlockSpec((None, TS, 1), lambda b_, g: (g, 0, 0)),
                  pl.BlockSpec((1, GW), lambda b_, g: (0, g)),
                  pl.BlockSpec((1, GW), lambda b_, g: (0, g))],
        out_specs=pl.BlockSpec((None, T, GW), lambda b_, g: (b_, 0, g)),
        scratch_shapes=[pltpu.VMEM((T, Q), F32),
                        pltpu.VMEM((TS, T), F32),
                        pltpu.VMEM((TS, T), F32),
                        pltpu.VMEM((T, 2 * GW), F32),
                        pltpu.VMEM((NCH, SSD_STATE, 2 * GW), F32),
                        pltpu.VMEM((NCH, 8, 2 * GW), F32),
                        pltpu.VMEM((NCH, SSD_STATE, 2 * GW), BF16)],
        compiler_params=_cparams(2),
        name="ssd",
    )(xa, xa, xa, p, dtr, bias_r, bias_c, alog_r, alog_c, dsk, ng)


def _attend(parts, extra_logit=None):
    mx = None
    for s, _ in parts:
        r = jnp.max(s, axis=-1, keepdims=True)
        mx = r if mx is None else jnp.maximum(mx, r)
    if extra_logit is not None:
        mx = jnp.maximum(mx, extra_logit)
    den = None
    acc = None
    for s, v in parts:
        e = jnp.exp(s - mx)
        d = jnp.sum(e, axis=-1, keepdims=True)
        den = d if den is None else den + d
        o = _dot(e.astype(BF16), v)
        acc = o if acc is None else acc + o
    if extra_logit is not None:
        den = den + jnp.exp(extra_logit - mx)
    return acc / den


def _na_kernel(q_ref, k_ref, v_ref, bias_ref, o_ref):
    lane = lax.broadcasted_iota(jnp.int32, (NA_NQ, Q), 1)
    lo = lane < 64
    zero = jnp.zeros((NA_NQ, Q), BF16)
    kc = k_ref[L:T, :]
    vc = v_ref[L:T, :]

    def masked_q(q0):
        q = q_ref[pl.ds(q0, NA_NQ), :] * 0.125
        return jnp.where(lo, q, zero), jnp.where(lo, zero, q)

    def body(j, carry):
        q0 = pl.multiple_of(j * NA_NQ, NA_NQ)
        start = jnp.clip(NA_QROWS * j - NA_KH // 2, 0, ROWS - NA_KROWS)
        koff = pl.multiple_of(start * GRID_W, GRID_W)
        case = ((j >= 1).astype(jnp.int32) + (j >= 2).astype(jnp.int32)
                + (j >= ROWS // NA_QROWS - 2).astype(jnp.int32) + (j >= ROWS // NA_QROWS - 1).astype(jnp.int32))
        kl = k_ref[pl.ds(koff, NA_NK), :]
        vl = v_ref[pl.ds(koff, NA_NK), :]
        outs = []
        for hh, qm in enumerate(masked_q(q0)):
            s_loc = _nt(qm, kl) + bias_ref[case, hh].astype(F32)
            s_ctx = _nt(qm, kc)
            outs.append(_attend([(s_loc, vl), (s_ctx, vc)]))
        o_ref[pl.ds(q0, NA_NQ), :] = jnp.where(lo, outs[0], outs[1]).astype(o_ref.dtype)
        return carry

    lax.fori_loop(0, L // NA_NQ, body, 0)

    for cbk in range(NC // NA_NQ):
        q0 = L + cbk * NA_NQ
        outs = [_attend([(_nt(qm, kc), vc)]) for qm in masked_q(q0)]
        o_ref[q0:q0 + NA_NQ, :] = jnp.where(lo, outs[0], outs[1]).astype(o_ref.dtype)


def _na_bias_tables(rpb_all):
    depth = rpb_all.shape[0]
    cq = np.arange(GRID_W)
    ck = np.arange(GRID_W)
    cd = ck[None, :] - cq[:, None] + NA_KW - 1
    col_oh = (cd[None] == np.arange(2 * NA_KW - 1)[:, None, None]).astype(np.float32)
    cs = np.clip(cq - NA_KW // 2, 0, GRID_W - NA_KW)
    col_ok = (ck[None, :] >= cs[:, None]) & (ck[None, :] < cs[:, None] + NA_KW)
    j_rep = [0, 1, 2, ROWS // NA_QROWS - 2, ROWS // NA_QROWS - 1]
    row_oh = np.zeros((NA_CASES, NA_QROWS, NA_KROWS, 2 * NA_KH - 1), np.float32)
    row_ok = np.zeros((NA_CASES, NA_QROWS, NA_KROWS), bool)
    for ci, j in enumerate(j_rep):
        start = int(np.clip(NA_QROWS * j - NA_KH // 2, 0, ROWS - NA_KROWS))
        for qr in range(NA_QROWS):
            r = NA_QROWS * j + qr
            rs = int(np.clip(r - NA_KH // 2, 0, ROWS - NA_KH))
            for i in range(NA_KROWS):
                kr = start + i
                if rs <= kr < rs + NA_KH:
                    row_ok[ci, qr, i] = True
                    row_oh[ci, qr, i, kr - r + NA_KH - 1] = 1.0
    colexp = jnp.einsum("lhrd,dab->lhrab", rpb_all, col_oh, precision=HIGHEST)
    tab = jnp.einsum("cqir,lhrab->lhcqaib", row_oh, colexp, precision=HIGHEST)
    valid = row_ok[:, :, None, :, None] & col_ok[None, None, :, None, :]
    tab = jnp.where(valid[None, None], tab, NEG).astype(BF16)
    tab = tab.reshape(depth, NA_HEADS // 2, 2, NA_CASES, NA_NQ, NA_NK)
    return tab.transpose(0, 1, 3, 2, 4, 5)


def _na(p, bias_tab):
    b = p.shape[0]
    qo, ko, vo = C_NAQ // Q, C_NAK // Q, C_NAV // Q
    return pl.pallas_call(
        _na_kernel,
        out_shape=jax.ShapeDtypeStruct((b, T, D), BF16),
        grid=(b, NA_HEADS // 2),
        in_specs=[pl.BlockSpec((None, T, Q), lambda b_, h: (b_, 0, qo + h)),
                  pl.BlockSpec((None, T, Q), lambda b_, h: (b_, 0, ko + h)),
                  pl.BlockSpec((None, T, Q), lambda b_, h: (b_, 0, vo + h)),
                  pl.BlockSpec((None, NA_CASES, 2, NA_NQ, NA_NK), lambda b_, h: (h, 0, 0, 0, 0))],
        out_specs=pl.BlockSpec((None, T, Q), lambda b_, h: (b_, 0, h)),
        compiler_params=_cparams(2),
        name="na_attn",
    )(p, p, p, bias_tab)


SWA_KWIN = 3 * SWA_W
SWA_QB = 128
SWA_STACK = 4 * SWA_QB
ROPE_RC = 256


def _swap16(t, lane):
    a = pltpu.roll(t, 112, axis=1)
    b = pltpu.roll(t, 16, axis=1)
    return jnp.where((lane % 32) < 16, a, b)


def _swa_kernel(sink_ref, q_ref, k_ref, v_ref, cos_ref, sin_ref, o_ref, krot_ref):
    kp = pl.program_id(1)
    lane = lax.broadcasted_iota(jnp.int32, (SWA_QB, Q), 1)
    lo = lane < 64
    zero = jnp.zeros((SWA_QB, Q), BF16)
    row = lax.broadcasted_iota(jnp.int32, (SWA_STACK, 1), 0)
    rel0 = (lax.broadcasted_iota(jnp.int32, (SWA_STACK, SWA_KWIN), 1)
            - lax.broadcasted_iota(jnp.int32, (SWA_STACK, SWA_KWIN), 0) % SWA_QB)

    def rope(t, r0, n):
        lane_n = lax.broadcasted_iota(jnp.int32, (n, Q), 1)
        return t * cos_ref[pl.ds(r0, n), :] + _swap16(t, lane_n) * sin_ref[pl.ds(r0, n), :]

    def krot_body(i, carry):
        r0 = pl.multiple_of(i * ROPE_RC, ROPE_RC)
        for gl in range(2):
            t = k_ref[pl.ds(r0, ROPE_RC), gl * Q:(gl + 1) * Q].astype(F32)
            krot_ref[pl.ds(r0, ROPE_RC), gl * Q:(gl + 1) * Q] = rope(t, r0, ROPE_RC).astype(BF16)
        return carry

    lax.fori_loop(0, L // ROPE_RC, krot_body, 0)

    def stack_q(qa, qb):
        return jnp.concatenate([jnp.where(lo, qa, zero), jnp.where(lo, zero, qa),
                                jnp.where(lo, qb, zero), jnp.where(lo, zero, qb)], axis=0)

    def sink_col(gl):
        base = kp * 8 + gl * 4
        return jnp.where(row < SWA_QB, sink_ref[base],
                         jnp.where(row < 2 * SWA_QB, sink_ref[base + 1],
                                   jnp.where(row < 3 * SWA_QB, sink_ref[base + 2], sink_ref[base + 3])))

    def unstack(o):
        oa = jnp.where(lo, o[0:SWA_QB], o[SWA_QB:2 * SWA_QB])
        ob = jnp.where(lo, o[2 * SWA_QB:3 * SWA_QB], o[3 * SWA_QB:4 * SWA_QB])
        return jnp.concatenate([oa, ob], axis=1)

    def body(i, carry):
        q0 = pl.multiple_of(i * SWA_QB, SWA_QB)
        ws = pl.multiple_of(jnp.clip((i - 1) * SWA_W, 0, L - SWA_KWIN), SWA_W)
        valid = jnp.abs(rel0 + (ws - q0)) <= SWA_W
        for gl in range(2):
            qa = rope(q_ref[pl.ds(q0, SWA_QB), gl * 2 * Q:gl * 2 * Q + Q].astype(F32), q0, SWA_QB) * 0.125
            qb = rope(q_ref[pl.ds(q0, SWA_QB), gl * 2 * Q + Q:(gl + 1) * 2 * Q].astype(F32), q0, SWA_QB) * 0.125
            qs = stack_q(qa.astype(BF16), qb.astype(BF16))
            kw = krot_ref[pl.ds(ws, SWA_KWIN), gl * Q:(gl + 1) * Q]
            vw = v_ref[pl.ds(ws, SWA_KWIN), gl * Q:(gl + 1) * Q]
            kc = k_ref[L:T, gl * Q:(gl + 1) * Q]
            vc = v_ref[L:T, gl * Q:(gl + 1) * Q]
            s_loc = jnp.where(valid, _nt(qs, kw), NEG)
            s_ctx = _nt(qs, kc)
            o = _attend([(s_loc, vw), (s_ctx, vc)], sink_col(gl))
            o_ref[pl.ds(q0, SWA_QB), gl * 2 * Q:(gl + 1) * 2 * Q] = unstack(o).astype(o_ref.dtype)
        return carry

    lax.fori_loop(0, L // SWA_QB, body, 0)

    for cbk in range(NC // SWA_QB):
        q0 = L + cbk * SWA_QB
        for gl in range(2):
            qa = q_ref[q0:q0 + SWA_QB, gl * 2 * Q:gl * 2 * Q + Q] * 0.125
            qb = q_ref[q0:q0 + SWA_QB, gl * 2 * Q + Q:(gl + 1) * 2 * Q] * 0.125
            qs = stack_q(qa, qb)
            kc = k_ref[L:T, gl * Q:(gl + 1) * Q]
            vc = v_ref[L:T, gl * Q:(gl + 1) * Q]
            o = _attend([(_nt(qs, kc), vc)], sink_col(gl))
            o_ref[q0:q0 + SWA_QB, gl * 2 * Q:(gl + 1) * 2 * Q] = unstack(o).astype(o_ref.dtype)


def _rope_tables():
    pos = np.arange(L)
    lane = np.arange(Q) % 64
    inv = ROPE_BASE ** (-(lane % 16).astype(np.float64) / 16.0)
    p = np.where(lane[None, :] < 32, (pos // GRID_W)[:, None], (pos % GRID_W)[:, None]).astype(np.float64)
    ang = p * inv[None, :]
    sign = np.where((lane % 32) < 16, -1.0, 1.0)[None, :]
    return jnp.asarray(np.cos(ang), F32), jnp.asarray(np.sin(ang) * sign, F32)


def _swa(p, sink, cos_t, sin_t):
    b = p.shape[0]
    qo, ko, vo = C_SWAQ // 512, C_SWAK // 256, C_SWAV // 256
    return pl.pallas_call(
        _swa_kernel,
        out_shape=jax.ShapeDtypeStruct((b, T, D), BF16),
        grid=(b, SWA_KV // 2),
        in_specs=[pl.BlockSpec(memory_space=pltpu.SMEM),
                  pl.BlockSpec((None, T, 512), lambda b_, h: (b_, 0, qo + h)),
                  pl.BlockSpec((None, T, 256), lambda b_, h: (b_, 0, ko + h)),
                  pl.BlockSpec((None, T, 256), lambda b_, h: (b_, 0, vo + h)),
                  pl.BlockSpec((L, Q), lambda b_, h: (0, 0)),
                  pl.BlockSpec((L, Q), lambda b_, h: (0, 0))],
        out_specs=pl.BlockSpec((None, T, 512), lambda b_, h: (b_, 0, h)),
        scratch_shapes=[pltpu.VMEM((L, 2 * Q), BF16)],
        compiler_params=_cparams(2),
        name="swa_attn",
    )(sink, p, p, p, cos_t, sin_t)


def _merge_kernel(x_ref, ya_ref, yb_ref, yc_ref, g0_ref, g1_ref, g2_ref, mg_ref,
                  wa_ref, wb_ref, wc_ref, wo_ref, o_ref):
    m = (_sigmoid(g0_ref[...].astype(F32)) * _dot(ya_ref[...], wa_ref[...])
         + _sigmoid(g1_ref[...].astype(F32)) * _dot(yb_ref[...], wb_ref[...])
         + _sigmoid(g2_ref[...].astype(F32)) * _dot(yc_ref[...], wc_ref[...]))
    o_ref[...] = x_ref[...] + mg_ref[...] * _dot(m.astype(BF16), wo_ref[...])


def _const_spec(shape):
    return pl.BlockSpec(shape, lambda b, j: (0,) * len(shape))


def _merge(xs, ya, yb, yc, p, ms, wa, wb, wc, wo):
    b = xs.shape[0]
    go = C_GATE // D
    row = lambda w: pl.BlockSpec((None, TR, w), lambda b_, j: (b_, j, 0))
    gate = lambda k: pl.BlockSpec((None, TR, D), lambda b_, j: (b_, j, go + k))
    return pl.pallas_call(
        _merge_kernel,
        out_shape=jax.ShapeDtypeStruct((b, T, D), F32),
        grid=(b, NRT),
        in_specs=[row(D), row(SSD_INNER), row(D), row(D), gate(0), gate(1), gate(2), _mod_spec(2),
                  _const_spec((SSD_INNER, D)), _const_spec((D, D)), _const_spec((D, D)), _const_spec((D, D))],
        out_specs=row(D),
        compiler_params=_cparams(2),
        name="merge",
    )(xs, ya, yb, yc, p, p, p, ms, wa, wb, wc, wo)


FF_CH = 1024


def _ffn_kernel(x_ref, g_ref, sh_ref, sc_ref, mg_ref, w1_ref, w2_ref, o_ref):
    x = x_ref[...]
    h = (_rms(x, g_ref[...]) * (1.0 + sc_ref[...]) + sh_ref[...]).astype(BF16)
    acc = None
    for kf in range(D_FF // FF_CH):
        a = jnp.maximum(_dot(h, w1_ref[:, kf * FF_CH:(kf + 1) * FF_CH]), 0.0)
        o = _dot((a * a).astype(BF16), w2_ref[kf * FF_CH:(kf + 1) * FF_CH, :])
        acc = o if acc is None else acc + o
    o_ref[...] = x + mg_ref[...] * acc


def _ffn(xs, g, ms, w1, w2):
    b = xs.shape[0]
    row = pl.BlockSpec((None, TR, D), lambda b_, j: (b_, j, 0))
    return pl.pallas_call(
        _ffn_kernel,
        out_shape=jax.ShapeDtypeStruct((b, T, D), F32),
        grid=(b, NRT),
        in_specs=[row, _const_spec((1, D)), _mod_spec(3), _mod_spec(4), _mod_spec(5),
                  _const_spec((D, D_FF)), _const_spec((D_FF, D))],
        out_specs=row,
        compiler_params=_cparams(2),
        name="ffn",
    )(xs, g.reshape(1, D), ms, ms, ms, w1, w2)


def _final_kernel(x_ref, g_ref, o_ref):
    o_ref[...] = _rms(x_ref[...], g_ref[...])


def _final_norm(xs, g):
    b = xs.shape[0]
    return pl.pallas_call(
        _final_kernel,
        out_shape=jax.ShapeDtypeStruct((b, L, D), F32),
        grid=(b, L // TR),
        in_specs=[pl.BlockSpec((None, TR, D), lambda b_, j: (b_, j, 0)), _const_spec((1, D))],
        out_specs=pl.BlockSpec((None, TR, D), lambda b_, j: (b_, j, 0)),
        compiler_params=_cparams(2),
        name="final_norm",
    )(xs, g.reshape(1, D))


def _prep_w_in(w):
    def dup(a):
        a = a.reshape(D, SWA_KV, 1, 64)
        return jnp.concatenate([a, a], axis=2).reshape(D, SWA_KV * 128)
    cat = jnp.concatenate([
        w[:, R_XBC:R_DT], w[:, R_NAK:R_NAV], w[:, R_NAV:R_SWAK], dup(w[:, R_SWAK:R_SWAV]),
        dup(w[:, R_SWAV:R_Z]), w[:, R_Z:R_NAQ], w[:, R_NAQ:R_SWAQ], w[:, R_SWAQ:R_GATE],
        w[:, R_GATE:R_END]], axis=1).astype(BF16)
    wdt = jnp.concatenate([w[:, R_DT:R_NAK], jnp.zeros((D, Q - 2 * SSD_HEADS), F32)], axis=1).astype(BF16)
    return cat, wdt


def kernel(x, c, ctx, c_ctx, ada_w, ada_b, norm1_g, norm2_g, w_in, conv_w, conv_b, dt_bias, a_log, ssd_d,
           ssd_norm_g, na_rpb, swa_sink, w_o_ssd, w_o_na, w_o_swa, w_out, w_ff1, w_ff2, final_g):
    b = x.shape[0]
    depth = ada_w.shape[0]
    xs = jnp.concatenate([x, ctx], axis=1)
    nrow = -(-(b + 1) // 8) * 8
    cvec = jnp.concatenate([c, c_ctx[None, :], jnp.zeros((nrow - b - 1, D), F32)], axis=0)
    mod = _ada_mod(cvec, ada_w, ada_b)
    lat = mod[:, :b].reshape(depth, b, 1, N_MOD, 1, D)
    cx = jnp.broadcast_to(mod[:, b].reshape(depth, 1, 1, N_MOD, 1, D), lat.shape)
    ms_all = jnp.concatenate([lat, cx], axis=2)
    cos_t, sin_t = _rope_tables()
    na_bias = _na_bias_tables(na_rpb)

    for l in range(depth):
        ms = ms_all[l]
        wcat, wdt = _prep_w_in(w_in[l])
        h = _normmod(xs, norm1_g[l], ms)
        p = _matmul(h, wcat, BF16, 1024, "in_proj")
        dtr = _matmul(h, wdt, F32, Q, "dt_proj")
        xa = _conv_silu(p, conv_w[l], conv_b[l])
        ya = _ssd(xa, p, dtr, dt_bias[l], a_log[l], ssd_d[l], ssd_norm_g[l])
        yb = _na(p, na_bias[l])
        yc = _swa(p, swa_sink[l], cos_t, sin_t)
        xs = _merge(xs, ya, yb, yc, p, ms, w_o_ssd[l].astype(BF16), w_o_na[l].astype(BF16),
                    w_o_swa[l].astype(BF16), w_out[l].astype(BF16))
        xs = _ffn(xs, norm2_g[l], ms, w_ff1[l].astype(BF16), w_ff2[l].astype(BF16))
    return _final_norm(xs, final_g)
```

```python
import numpy as np
import jax
import jax.numpy as jnp
from jax import lax
from jax.experimental import pallas as pl
from jax.experimental.pallas import tpu as pltpu

F32 = jnp.float32
BF16 = jnp.bfloat16
HIGHEST = lax.Precision.HIGHEST

D = 1024
L = 2048
NC = 256
T = L + NC
GRID_W = 64
ROWS = L // GRID_W
EPS = 1e-6
N_MOD = 6

SSD_INNER = 2 * D
SSD_HEADS = 32
SSD_GROUPS = 8
SSD_HPG = 4
SSD_STATE = 128
SSD_XBC = SSD_INNER + 2 * SSD_GROUPS * SSD_STATE
Q = 128
NCH = T // Q

NA_HEADS = 16
NA_KH = 8
NA_KW = 16
NA_QROWS = 2
NA_KROWS = 10
NA_NQ = NA_QROWS * GRID_W
NA_NK = NA_KROWS * GRID_W
NA_CASES = 5
NA_BLK = 2

SWA_HEADS = 16
SWA_KV = 4
SWA_W = 128
ROPE_BASE = 10000.0
D_FF = 4 * D

R_XBC, R_DT, R_NAK, R_NAV, R_SWAK, R_SWAV, R_Z, R_NAQ, R_SWAQ, R_GATE, R_END = (
    0, 4096, 4160, 5184, 6208, 6464, 6720, 8768, 9792, 10816, 13888)
C_XBC, C_NAK, C_NAV, C_SWAK, C_SWAV, C_Z, C_NAQ, C_SWAQ, C_GATE, NCAT = (
    0, 4096, 5120, 6144, 6656, 7168, 9216, 10240, 11264, 14336)

NEG = -1e30
LOG2E = float(np.log2(np.e))
QSCALE = 0.125 * LOG2E
VMEM_LIMIT = 48 * 1024 * 1024


def _cparams(n_axes):
    return pltpu.CompilerParams(dimension_semantics=("parallel",) * n_axes,
                                vmem_limit_bytes=VMEM_LIMIT)


def _sigmoid(x):
    return 1.0 / (1.0 + jnp.exp(-x))


def _softplus(x):
    return jnp.maximum(x, 0.0) + jnp.log1p(jnp.exp(-jnp.abs(x)))


def _nt(a, b):
    return lax.dot_general(a, b, (((1,), (1,)), ((), ())), preferred_element_type=F32)


def _tn(a, b):
    return lax.dot_general(a, b, (((0,), (0,)), ((), ())), preferred_element_type=F32)


def _dot(a, b):
    return jnp.dot(a, b, preferred_element_type=F32)


def _chunk_loop(n, unroll, body):
    def wrapped(c, carry):
        body(c, pl.multiple_of(c * Q, Q))
        return carry
    lax.fori_loop(0, n, wrapped, 0, unroll=unroll)


def _ada_kernel(c_ref, w_ref, b_ref, o_ref):
    c = c_ref[...]
    s = (c * _sigmoid(c)).astype(BF16)
    o_ref[...] = _dot(s, w_ref[...].astype(BF16)) + b_ref[...]


def _ada_mod(cvec, ada_w, ada_b):
    depth = ada_w.shape[0]
    r = cvec.shape[0]
    return pl.pallas_call(
        _ada_kernel,
        out_shape=jax.ShapeDtypeStruct((depth, r, N_MOD * D), F32),
        grid=(depth, N_MOD),
        in_specs=[pl.BlockSpec((r, D), lambda l, n: (0, 0)),
                  pl.BlockSpec((None, D, D), lambda l, n: (l, 0, n)),
                  pl.BlockSpec((None, 1, D), lambda l, n: (l, 0, n))],
        out_specs=pl.BlockSpec((None, r, D), lambda l, n: (l, 0, n)),
        compiler_params=_cparams(2),
        name="ada_mod",
    )(cvec, ada_w, ada_b.reshape(depth, 1, N_MOD * D))


TR = 256
NRT = T // TR


def _mod_spec(which):
    return pl.BlockSpec((None, None, None, 1, D), lambda b, j: (b, j // (L // TR), which, 0, 0))


def _rms(x, g):
    return x * lax.rsqrt(jnp.mean(x * x, axis=-1, keepdims=True) + EPS) * g


def _normmod_kernel(x_ref, g_ref, sh_ref, sc_ref, o_ref):
    y = _rms(x_ref[...], g_ref[...])
    o_ref[...] = (y * (1.0 + sc_ref[...]) + sh_ref[...]).astype(o_ref.dtype)


def _normmod(xs, g, ms):
    b = xs.shape[0]
    return pl.pallas_call(
        _normmod_kernel,
        out_shape=jax.ShapeDtypeStruct((b, T, D), BF16),
        grid=(b, NRT),
        in_specs=[pl.BlockSpec((None, TR, D), lambda b, j: (b, j, 0)),
                  pl.BlockSpec((1, D), lambda b, j: (0, 0)),
                  _mod_spec(0), _mod_spec(1)],
        out_specs=pl.BlockSpec((None, TR, D), lambda b, j: (b, j, 0)),
        compiler_params=_cparams(2),
        name="normmod",
    )(xs, g.reshape(1, D), ms, ms)


def _mm_kernel(x_ref, w_ref, o_ref):
    o_ref[...] = _dot(x_ref[...], w_ref[...]).astype(o_ref.dtype)


def _matmul(x, w, out_dtype, tn, name):
    b, t, k = x.shape
    n = w.shape[1]
    return pl.pallas_call(
        _mm_kernel,
        out_shape=jax.ShapeDtypeStruct((b, t, n), out_dtype),
        grid=(n // tn, b),
        in_specs=[pl.BlockSpec((None, t, k), lambda n_, b_: (b_, 0, 0)),
                  pl.BlockSpec((k, tn), lambda n_, b_: (0, n_))],
        out_specs=pl.BlockSpec((None, t, tn), lambda n_, b_: (b_, 0, n_)),
        compiler_params=_cparams(2),
        name=name,
    )(x, w)


CW = 512
CONV_RC = 256
PAD_LAT = 8
PAD_CTX = 16 + L


def _conv_kernel(x_ref, w_ref, b_ref, o_ref, pad_ref):
    z8 = jnp.zeros((8, CW), F32)
    pad_ref[0:8, :] = z8
    pad_ref[PAD_LAT + L:PAD_CTX, :] = z8
    pad_ref[PAD_CTX + NC:PAD_CTX + NC + 8, :] = z8
    pad_ref[PAD_LAT:PAD_LAT + L, :] = x_ref[0:L, :].astype(F32)
    pad_ref[PAD_CTX:PAD_CTX + NC, :] = x_ref[L:T, :].astype(F32)
    w = w_ref[...]
    bias = b_ref[...]
    for c in range(T // CONV_RC):
        r0 = c * CONV_RC
        base = (PAD_LAT if r0 < L else PAD_CTX - L) + r0 - 2
        acc = bias + pad_ref[base:base + CONV_RC, :] * w[0:1, :]
        for k in range(1, 5):
            acc = acc + pad_ref[base + k:base + k + CONV_RC, :] * w[k:k + 1, :]
        o_ref[r0:r0 + CONV_RC, :] = (acc * _sigmoid(acc)).astype(o_ref.dtype)


def _conv_silu(p, conv_w, conv_b):
    b = p.shape[0]
    w8 = jnp.concatenate([conv_w, jnp.zeros((3, SSD_XBC), F32)], axis=0)
    return pl.pallas_call(
        _conv_kernel,
        out_shape=jax.ShapeDtypeStruct((b, T, SSD_XBC), BF16),
        grid=(b, SSD_XBC // CW),
        in_specs=[pl.BlockSpec((None, T, CW), lambda b_, n: (b_, 0, n)),
                  pl.BlockSpec((8, CW), lambda b_, n: (0, n)),
                  pl.BlockSpec((1, CW), lambda b_, n: (0, n))],
        out_specs=pl.BlockSpec((None, T, CW), lambda b_, n: (b_, 0, n)),
        scratch_shapes=[pltpu.VMEM((T + 24, CW), F32)],
        compiler_params=_cparams(2),
        name="conv_silu",
    )(p, w8, conv_b.reshape(1, SSD_XBC))


GW = SSD_HPG * 64
TS = 16


def _split3(x):
    hi = x.astype(BF16)
    r = x - hi.astype(F32)
    mid = r.astype(BF16)
    lo = (r - mid.astype(F32)).astype(BF16)
    return hi, mid, lo


def _ssd_kernel(x_ref, b_ref, c_ref, z_ref, dt_ref, biasr_ref, alogr_ref, dsk_ref, ng_ref, o_ref,
                dt_s, acc_s, wts_s, eacc_s, rowt_s, dtt_s, xw_s, ec_s, big_s, sin_s, cb_s):
    g = pl.program_id(1)
    ii = lax.broadcasted_iota(jnp.int32, (Q, Q), 0)
    jj = lax.broadcasted_iota(jnp.int32, (Q, Q), 1)
    tril = jnp.where(ii >= jj, 1.0, 0.0).astype(BF16)
    triu = jnp.where(ii <= jj, 1.0, 0.0).astype(BF16)
    lcat = jnp.concatenate([jnp.concatenate([tril] * 3, axis=1),
                            jnp.concatenate([triu] * 3, axis=1)], axis=0)
    low = ii > jj
    up = ii < jj
    lo = jj < 64
    tgt = jnp.where(jj < 4, g * 4 + jj, 32 + g * 4 + jj - 4)
    sel = jnp.where((ii == tgt) & (jj < 8), 1.0, 0.0).astype(BF16)
    sel3 = jnp.concatenate([sel] * 3, axis=0)
    ek = lax.broadcasted_iota(jnp.int32, (Q, 2 * GW), 0)
    ech = lax.broadcasted_iota(jnp.int32, (Q, 2 * GW), 1)
    esel = jnp.where(ek == ech // 64, 1.0, 0.0).astype(BF16)
    esel2 = jnp.concatenate([esel] * 2, axis=0)
    isf = lax.broadcasted_iota(jnp.int32, (1, Q), 1) < 4
    bias_r = biasr_ref[...]
    a_r = -jnp.exp(alogr_ref[...])
    zero_x = jnp.zeros((Q, Q), BF16)
    rows = lambda t0: pl.ds(t0, Q)

    def stage_dt(c, t0):
        raw3 = _split3(dt_ref[rows(t0), :])
        dt_s[rows(t0), :] = _softplus(_dot(jnp.concatenate(raw3, axis=1), sel3) + bias_r)

    _chunk_loop(NCH, 6, stage_dt)

    def stage_cum(c, t0):
        dt = dt_s[rows(t0), :]
        both = _dot(lcat, jnp.concatenate(_split3(dt * a_r), axis=0))
        acc = jnp.where(isf, both[0:Q], both[Q:2 * Q])
        tot = jnp.where(isf, both[Q - 1:Q], both[Q:Q + 1])
        acc_s[rows(t0), :] = acc
        wts_s[rows(t0), :] = dt * jnp.exp(tot - acc)
        eacc_s[rows(t0), :] = jnp.exp(acc)
        dt_t = dt.T[0:TS]
        rowt_s[:, rows(t0)] = acc.T[0:TS] - jnp.log(dt_t)
        dtt_s[:, rows(t0)] = dt_t

    _chunk_loop(NCH, 3, stage_cum)

    def stage_expand(c, t0):
        w2 = _split3(wts_s[rows(t0), :])[:2]
        e2 = _split3(eacc_s[rows(t0), :])[:2]
        lhs = jnp.concatenate([jnp.concatenate(w2, axis=1), jnp.concatenate(e2, axis=1)], axis=0)
        both = _dot(lhs, esel2)
        xs = x_ref[rows(t0), :].astype(F32)
        xw_s[rows(t0), :] = (jnp.concatenate([xs, xs], axis=1) * both[0:Q]).astype(BF16)
        ec_s[rows(t0), :] = both[Q:2 * Q]

    _chunk_loop(NCH, 3, stage_expand)

    def stage_upd(c, t0):
        big_s[c] = _tn(b_ref[rows(t0), :], xw_s[rows(t0), :])

    _chunk_loop(NCH, 6, stage_upd)

    s = jnp.zeros((SSD_STATE, GW), F32)
    for c in list(range(L // Q, NCH)) + list(range(L // Q)):
        sin_s[c, :, 0:GW] = s.astype(BF16)
        s = s * ec_s[c * Q + Q - 1:c * Q + Q, 0:GW] + big_s[c, :, 0:GW]
    s = jnp.zeros((SSD_STATE, GW), F32)
    for c in reversed(range(NCH)):
        sin_s[c, :, GW:2 * GW] = s.astype(BF16)
        s = s * ec_s[c * Q:c * Q + 1, GW:2 * GW] + big_s[c, :, GW:2 * GW]

    def stage_cb(c, t0):
        cc = c_ref[rows(t0), :]
        cb_s[rows(t0), :] = _nt(cc, b_ref[rows(t0), :])
        big_s[c] = _dot(cc, sin_s[c])

    _chunk_loop(NCH, 6, stage_cb)

    dsk = dsk_ref[...]
    ng = ng_ref[...]

    def colb(v, k):
        return jnp.broadcast_to(v[:, k:k + 1], (Q, Q))

    def stage_out(c, t0):
        xb = x_ref[rows(t0), :]
        acc = acc_s[rows(t0), :]
        rowt = rowt_s[:, rows(t0)]
        dt_t = dtt_s[:, rows(t0)]
        ec = ec_s[rows(t0), :]
        cb = cb_s[rows(t0), :]
        yoff = big_s[c]
        ys = []
        for pr in range(2):
            ms = []
            for hh in range(2):
                r = 2 * pr + hh
                diag = jnp.log(dt_t[r:r + 1, :] + dt_t[4 + r:5 + r, :])
                seg = jnp.where(low, colb(acc, r) - rowt[r:r + 1, :],
                                jnp.where(up, colb(acc, 4 + r) - rowt[4 + r:5 + r, :], diag))
                ms.append((cb * jnp.exp(seg)).astype(BF16))
            xp = xb[:, pr * Q:(pr + 1) * Q]
            xstack = jnp.concatenate([jnp.where(lo, xp, zero_x), jnp.where(lo, zero_x, xp)], axis=0)
            y = _dot(jnp.concatenate(ms, axis=1), xstack)
            y = (y + ec[:, pr * Q:(pr + 1) * Q] * yoff[:, pr * Q:(pr + 1) * Q]
                 + ec[:, GW + pr * Q:GW + (pr + 1) * Q] * yoff[:, GW + pr * Q:GW + (pr + 1) * Q])
            ys.append(y)
        y = jnp.concatenate(ys, axis=1) + dsk * xb.astype(F32)
        zf = z_ref[rows(t0), :].astype(F32)
        u = y * (zf * _sigmoid(zf))
        o_ref[rows(t0), :] = _rms(u, ng).astype(o_ref.dtype)

    _chunk_loop(NCH, 2, stage_out)


def _ssd(xa, p, dtr, dt_bias, a_log, ssd_d, ssd_norm_g):
    b = xa.shape[0]
    def slots(v):
        s = v.reshape(2, SSD_GROUPS, SSD_HPG).transpose(1, 0, 2).reshape(SSD_GROUPS, 8)
        return jnp.pad(s, ((0, 0), (0, Q - 8))).reshape(SSD_GROUPS, 1, Q)
    dsk = jnp.repeat(ssd_d, 64).reshape(1, SSD_INNER)
    ng = ssd_norm_g.reshape(1, SSD_INNER)
    xoff, boff, coff, zoff = 0, SSD_INNER // Q, (SSD_INNER + SSD_GROUPS * SSD_STATE) // Q, C_Z // GW
    return pl.pallas_call(
        _ssd_kernel,
        out_shape=jax.ShapeDtypeStruct((b, T, SSD_INNER), BF16),
        grid=(b, SSD_GROUPS),
        in_specs=[pl.BlockSpec((None, T, GW), lambda b_, g: (b_, 0, xoff + g)),
                  pl.BlockSpec((None, T, Q), lambda b_, g: (b_, 0, boff + g)),
                  pl.BlockSpec((None, T, Q), lambda b_, g: (b_, 0, coff + g)),
                  pl.BlockSpec((None, T, GW), lambda b_, g: (b_, 0, zoff + g)),
                  pl.BlockSpec((None, T, Q), lambda b_, g: (b_, 0, 0)),
                  pl.BlockSpec((None, 1, Q), lambda b_, g: (g, 0, 0)),
                  pl.BlockSpec((None, 1, Q), lambda b_, g: (g, 0, 0)),
                  pl.BlockSpec((1, GW), lambda b_, g: (0, g)),
                  pl.BlockSpec((1, GW), lambda b_, g: (0, g))],
        out_specs=pl.BlockSpec((None, T, GW), lambda b_, g: (b_, 0, g)),
        scratch_shapes=[pltpu.VMEM((T, Q), F32),
                        pltpu.VMEM((T, Q), F32),
                        pltpu.VMEM((T, Q), F32),
                        pltpu.VMEM((T, Q), F32),
                        pltpu.VMEM((TS, T), F32),
                        pltpu.VMEM((TS, T), F32),
                        pltpu.VMEM((T, 2 * GW), BF16),
                        pltpu.VMEM((T, 2 * GW), F32),
                        pltpu.VMEM((NCH, SSD_STATE, 2 * GW), F32),
                        pltpu.VMEM((NCH, SSD_STATE, 2 * GW), BF16),
                        pltpu.VMEM((T, Q), F32)],
        compiler_params=_cparams(2),
        name="ssd",
    )(xa, xa, xa, p, dtr, slots(dt_bias), slots(a_log), dsk, ng)


def _softmax_t(parts, extra=None):
    mx = None
    for s in parts:
        r = jnp.max(s, axis=0, keepdims=True)
        mx = r if mx is None else jnp.maximum(mx, r)
    if extra is not None:
        mx = jnp.maximum(mx, extra)
    den = None
    es = []
    for s in parts:
        e = jnp.exp2(s - mx)
        d = jnp.sum(e, axis=0, keepdims=True)
        den = d if den is None else den + d
        es.append(e.astype(BF16))
    if extra is not None:
        den = den + jnp.exp2(extra - mx)
    return es, 1.0 / den


def _pv_t(vts, es):
    out = None
    for vt, e in zip(vts, es):
        o = _dot(vt, e)
        out = o if out is None else out + o
    return out


def _na_kernel(q_ref, k_ref, v_ref, bias_ref, o_ref, vt_ref):
    lane = lax.broadcasted_iota(jnp.int32, (NA_NQ, Q), 1)
    lo = lane < 64
    top = lax.broadcasted_iota(jnp.int32, (Q, NA_NQ), 0) < 64
    zero = jnp.zeros((NA_NQ, Q), BF16)

    def vt_body(c, t0):
        vt_ref[:, pl.ds(t0, Q)] = v_ref[pl.ds(t0, Q), :].astype(F32).T.astype(BF16)

    _chunk_loop(NCH, 6, vt_body)

    kc = k_ref[L:T, :]
    vtc = vt_ref[:, L:T]

    def qpair(q0):
        q = (q_ref[pl.ds(q0, NA_NQ), :].astype(F32) * QSCALE).astype(BF16)
        return jnp.concatenate([jnp.where(lo, q, zero), jnp.where(lo, zero, q)], axis=0)

    def finish(q0, ot, rden):
        ot = ot * rden
        w = jnp.where(top, ot[:, 0:NA_NQ], ot[:, NA_NQ:2 * NA_NQ])
        o_ref[pl.ds(q0, NA_NQ), :] = w.T.astype(o_ref.dtype)

    def body(jb, carry):
        blocks = []
        for u in range(NA_BLK):
            j = jb * NA_BLK + u
            q0 = pl.multiple_of(j * NA_NQ, NA_NQ)
            start = jnp.clip(NA_QROWS * j - NA_KH // 2, 0, ROWS - NA_KROWS)
            koff = pl.multiple_of(start * GRID_W, 2 * GRID_W)
            last = ROWS // NA_QROWS - 1
            case = (jnp.where(j >= 1, 1, 0) + jnp.where(j >= 2, 1, 0)
                    + jnp.where(j >= last - 1, 1, 0) + jnp.where(j >= last, 1, 0))
            qp = qpair(q0)
            s_loc = _nt(k_ref[pl.ds(koff, NA_NK), :], qp) + bias_ref[case]
            s_ctx = _nt(kc, qp)
            blocks.append((q0, koff, s_loc, s_ctx))
        soft = [(q0, koff) + _softmax_t([s_loc, s_ctx]) for q0, koff, s_loc, s_ctx in blocks]
        for q0, koff, es, rden in soft:
            finish(q0, _pv_t([vt_ref[:, pl.ds(koff, NA_NK)], vtc], es), rden)
        return carry

    lax.fori_loop(0, L // NA_NQ // NA_BLK, body, 0)

    ctx_blocks = []
    for cbk in range(NC // NA_NQ):
        q0 = L + cbk * NA_NQ
        ctx_blocks.append((q0, _nt(kc, qpair(q0))))
    ctx_soft = [(q0,) + _softmax_t([s]) for q0, s in ctx_blocks]
    for q0, es, rden in ctx_soft:
        finish(q0, _pv_t([vtc], es), rden)


def _na_bias_tables(rpb_all):
    depth = rpb_all.shape[0]
    cq = np.arange(GRID_W)
    ck = np.arange(GRID_W)
    cd = ck[None, :] - cq[:, None] + NA_KW - 1
    col_oh = (cd[None] == np.arange(2 * NA_KW - 1)[:, None, None]).astype(np.float32)
    cs = np.clip(cq - NA_KW // 2, 0, GRID_W - NA_KW)
    col_ok = (ck[None, :] >= cs[:, None]) & (ck[None, :] < cs[:, None] + NA_KW)
    j_rep = [0, 1, 2, ROWS // NA_QROWS - 2, ROWS // NA_QROWS - 1]
    row_oh = np.zeros((NA_CASES, NA_QROWS, NA_KROWS, 2 * NA_KH - 1), np.float32)
    row_ok = np.zeros((NA_CASES, NA_QROWS, NA_KROWS), bool)
    for ci, j in enumerate(j_rep):
        start = int(np.clip(NA_QROWS * j - NA_KH // 2, 0, ROWS - NA_KROWS))
        for qr in range(NA_QROWS):
            r = NA_QROWS * j + qr
            rs = int(np.clip(r - NA_KH // 2, 0, ROWS - NA_KH))
            for i in range(NA_KROWS):
                kr = start + i
                if rs <= kr < rs + NA_KH:
                    row_ok[ci, qr, i] = True
                    row_oh[ci, qr, i, kr - r + NA_KH - 1] = 1.0
    colexp = jnp.einsum("lhrd,dab->lhrab", rpb_all, col_oh, precision=HIGHEST)
    tab = jnp.einsum("cqir,lhrab->lhcibqa", row_oh, colexp, precision=HIGHEST)
    valid = (row_ok.transpose(0, 2, 1)[:, :, None, :, None]
             & col_ok.T[None, None, :, None, :])
    tab = jnp.where(valid[None, None], tab * LOG2E, NEG)
    tab = tab.reshape(depth, NA_HEADS // 2, 2, NA_CASES, NA_NK, NA_NQ)
    return tab.transpose(0, 1, 3, 4, 2, 5).reshape(depth, NA_HEADS // 2, NA_CASES, NA_NK, 2 * NA_NQ)


def _na(p, bias_tab):
    b = p.shape[0]
    qo, ko, vo = C_NAQ // Q, C_NAK // Q, C_NAV // Q
    return pl.pallas_call(
        _na_kernel,
        out_shape=jax.ShapeDtypeStruct((b, T, D), BF16),
        grid=(b, NA_HEADS // 2),
        in_specs=[pl.BlockSpec((None, T, Q), lambda b_, h: (b_, 0, qo + h)),
                  pl.BlockSpec((None, T, Q), lambda b_, h: (b_, 0, ko + h)),
                  pl.BlockSpec((None, T, Q), lambda b_, h: (b_, 0, vo + h)),
                  pl.BlockSpec((None, NA_CASES, NA_NK, 2 * NA_NQ), lambda b_, h: (h, 0, 0, 0))],
        out_specs=pl.BlockSpec((None, T, Q), lambda b_, h: (b_, 0, h)),
        scratch_shapes=[pltpu.VMEM((Q, T), BF16)],
        compiler_params=_cparams(2),
        name="na_attn",
    )(p, p, p, bias_tab)


SWA_KWIN = 3 * SWA_W
SWA_QB = 128
SWA_STACK = 4 * SWA_QB
ROPE_RC = 256


def _swap16(t, lane):
    a = pltpu.roll(t, 112, axis=1)
    b = pltpu.roll(t, 16, axis=1)
    return jnp.where((lane % 32) < 16, a, b)


def _swa_kernel(sink_ref, q_ref, k_ref, v_ref, cos_ref, sin_ref, mask_ref, o_ref, krot_ref, vt_ref):
    kp = pl.program_id(1)
    lane = lax.broadcasted_iota(jnp.int32, (SWA_QB, Q), 1)
    lo = lane < 64
    zero = jnp.zeros((SWA_QB, Q), BF16)
    col = lax.broadcasted_iota(jnp.int32, (1, SWA_STACK), 1)
    eye4 = jnp.where(lax.broadcasted_iota(jnp.int32, (SWA_STACK, Q), 0) % SWA_QB
                     == lax.broadcasted_iota(jnp.int32, (SWA_STACK, Q), 1), 1.0, 0.0).astype(BF16)

    def rope(t, r0, n):
        lane_n = lax.broadcasted_iota(jnp.int32, (n, Q), 1)
        return t * cos_ref[pl.ds(r0, n), :] + _swap16(t, lane_n) * sin_ref[pl.ds(r0, n), :]

    def krot_body(i, carry):
        r0 = pl.multiple_of(i * ROPE_RC, ROPE_RC)
        for gl in range(2):
            t = k_ref[pl.ds(r0, ROPE_RC), gl * Q:(gl + 1) * Q].astype(F32)
            krot_ref[pl.ds(r0, ROPE_RC), gl * Q:(gl + 1) * Q] = rope(t, r0, ROPE_RC).astype(BF16)
        return carry

    lax.fori_loop(0, L // ROPE_RC, krot_body, 0, unroll=2)

    def vt_body(c, t0):
        for gl in range(2):
            vt = v_ref[pl.ds(t0, Q), gl * Q:(gl + 1) * Q].astype(F32).T
            vt_ref[gl, :, pl.ds(t0, Q)] = vt[0:64].astype(BF16)

    _chunk_loop(NCH, 6, vt_body)

    def stack_q(qa, qb):
        return jnp.concatenate([jnp.where(lo, qa, zero), jnp.where(lo, zero, qa),
                                jnp.where(lo, qb, zero), jnp.where(lo, zero, qb)], axis=0)

    def sink_row(gl):
        base = kp * 8 + gl * 4
        row = jnp.where(col < SWA_QB, sink_ref[base],
                        jnp.where(col < 2 * SWA_QB, sink_ref[base + 1],
                                  jnp.where(col < 3 * SWA_QB, sink_ref[base + 2], sink_ref[base + 3])))
        return row * LOG2E

    def finish(q0, gl, ot, rden):
        ot = ot * rden
        pa = jnp.concatenate([ot[:, 0:SWA_QB], ot[:, SWA_QB:2 * SWA_QB]], axis=0).T
        pb = jnp.concatenate([ot[:, 2 * SWA_QB:3 * SWA_QB], ot[:, 3 * SWA_QB:4 * SWA_QB]], axis=0).T
        o_ref[pl.ds(q0, SWA_QB), gl * 2 * Q:(gl + 1) * 2 * Q] = jnp.concatenate([pa, pb], axis=1).astype(o_ref.dtype)

    def body(i, carry):
        q0 = pl.multiple_of(i * SWA_QB, SWA_QB)
        ws = pl.multiple_of(jnp.clip((i - 1) * SWA_W, 0, L - SWA_KWIN), SWA_W)
        case = jnp.where(i >= 1, 1, 0) + jnp.where(i >= L // SWA_QB - 1, 1, 0)
        scores = []
        for gl in range(2):
            qa = rope(q_ref[pl.ds(q0, SWA_QB), gl * 2 * Q:gl * 2 * Q + Q].astype(F32), q0, SWA_QB) * QSCALE
            qb = rope(q_ref[pl.ds(q0, SWA_QB), gl * 2 * Q + Q:(gl + 1) * 2 * Q].astype(F32), q0, SWA_QB) * QSCALE
            qs = stack_q(qa.astype(BF16), qb.astype(BF16))
            a_loc = jnp.concatenate([krot_ref[pl.ds(ws, SWA_KWIN), gl * Q:(gl + 1) * Q], mask_ref[case]], axis=1)
            s_loc = _nt(a_loc, jnp.concatenate([qs, eye4], axis=1))
            s_ctx = _nt(k_ref[L:T, gl * Q:(gl + 1) * Q], qs)
            scores.append((s_loc, s_ctx))
        soft = [_softmax_t([s_loc, s_ctx], sink_row(gl)) for gl, (s_loc, s_ctx) in enumerate(scores)]
        for gl, (es, rden) in enumerate(soft):
            finish(q0, gl, _pv_t([vt_ref[gl, :, pl.ds(ws, SWA_KWIN)], vt_ref[gl, :, L:T]], es), rden)
        return carry

    lax.fori_loop(0, L // SWA_QB, body, 0)

    for cbk in range(NC // SWA_QB):
        q0 = L + cbk * SWA_QB
        scores = []
        for gl in range(2):
            qa = q_ref[q0:q0 + SWA_QB, gl * 2 * Q:gl * 2 * Q + Q].astype(F32) * QSCALE
            qb = q_ref[q0:q0 + SWA_QB, gl * 2 * Q + Q:(gl + 1) * 2 * Q].astype(F32) * QSCALE
            scores.append(_nt(k_ref[L:T, gl * Q:(gl + 1) * Q], stack_q(qa.astype(BF16), qb.astype(BF16))))
        soft = [_softmax_t([s], sink_row(gl)) for gl, s in enumerate(scores)]
        for gl, (es, rden) in enumerate(soft):
            finish(q0, gl, _pv_t([vt_ref[gl, :, L:T]], es), rden)


def _rope_tables():
    pos = np.arange(L)
    lane = np.arange(Q) % 64
    inv = ROPE_BASE ** (-(lane % 16).astype(np.float64) / 16.0)
    p = np.where(lane[None, :] < 32, (pos // GRID_W)[:, None], (pos % GRID_W)[:, None]).astype(np.float64)
    ang = p * inv[None, :]
    sign = np.where((lane % 32) < 16, -1.0, 1.0)[None, :]
    return jnp.asarray(np.cos(ang), F32), jnp.asarray(np.sin(ang) * sign, F32)


def _swa_mask_table():
    key = np.arange(SWA_KWIN)[:, None]
    qry = np.arange(SWA_QB)[None, :]
    tabs = [np.where(np.abs(key - qry + delta) <= SWA_W, 0.0, NEG) for delta in (0, -SWA_W, -2 * SWA_W)]
    return jnp.asarray(np.stack(tabs), BF16)


def _swa(p, sink, cos_t, sin_t, mask_t):
    b = p.shape[0]
    qo, ko, vo = C_SWAQ // 512, C_SWAK // 256, C_SWAV // 256
    return pl.pallas_call(
        _swa_kernel,
        out_shape=jax.ShapeDtypeStruct((b, T, D), BF16),
        grid=(b, SWA_KV // 2),
        in_specs=[pl.BlockSpec(memory_space=pltpu.SMEM),
                  pl.BlockSpec((None, T, 512), lambda b_, h: (b_, 0, qo + h)),
                  pl.BlockSpec((None, T, 256), lambda b_, h: (b_, 0, ko + h)),
                  pl.BlockSpec((None, T, 256), lambda b_, h: (b_, 0, vo + h)),
                  pl.BlockSpec((L, Q), lambda b_, h: (0, 0)),
                  pl.BlockSpec((L, Q), lambda b_, h: (0, 0)),
                  pl.BlockSpec((3, SWA_KWIN, SWA_QB), lambda b_, h: (0, 0, 0))],
        out_specs=pl.BlockSpec((None, T, 512), lambda b_, h: (b_, 0, h)),
        scratch_shapes=[pltpu.VMEM((L, 2 * Q), BF16),
                        pltpu.VMEM((2, 64, T), BF16)],
        compiler_params=_cparams(2),
        name="swa_attn",
    )(sink, p, p, p, cos_t, sin_t, mask_t)


def _merge_kernel(x_ref, ya_ref, yb_ref, yc_ref, g0_ref, g1_ref, g2_ref, mg_ref,
                  wa_ref, wb_ref, wc_ref, wo_ref, o_ref):
    m = (_sigmoid(g0_ref[...].astype(F32)) * _dot(ya_ref[...], wa_ref[...])
         + _sigmoid(g1_ref[...].astype(F32)) * _dot(yb_ref[...], wb_ref[...])
         + _sigmoid(g2_ref[...].astype(F32)) * _dot(yc_ref[...], wc_ref[...]))
    o_ref[...] = x_ref[...] + mg_ref[...] * _dot(m.astype(BF16), wo_ref[...])


def _const_spec(shape):
    return pl.BlockSpec(shape, lambda b, j: (0,) * len(shape))


def _merge(xs, ya, yb, yc, p, ms, wa, wb, wc, wo):
    b = xs.shape[0]
    go = C_GATE // D
    row = lambda w: pl.BlockSpec((None, TR, w), lambda b_, j: (b_, j, 0))
    gate = lambda k: pl.BlockSpec((None, TR, D), lambda b_, j: (b_, j, go + k))
    return pl.pallas_call(
        _merge_kernel,
        out_shape=jax.ShapeDtypeStruct((b, T, D), F32),
        grid=(b, NRT),
        in_specs=[row(D), row(SSD_INNER), row(D), row(D), gate(0), gate(1), gate(2), _mod_spec(2),
                  _const_spec((SSD_INNER, D)), _const_spec((D, D)), _const_spec((D, D)), _const_spec((D, D))],
        out_specs=row(D),
        compiler_params=_cparams(2),
        name="merge",
    )(xs, ya, yb, yc, p, p, p, ms, wa, wb, wc, wo)


FF_CH = 1024


def _ffn_kernel(x_ref, g_ref, sh_ref, sc_ref, mg_ref, w1_ref, w2_ref, o_ref):
    x = x_ref[...]
    h = (_rms(x, g_ref[...]) * (1.0 + sc_ref[...]) + sh_ref[...]).astype(BF16)
    acc = None
    for kf in range(D_FF // FF_CH):
        a = jnp.maximum(_dot(h, w1_ref[:, kf * FF_CH:(kf + 1) * FF_CH]), 0.0)
        o = _dot((a * a).astype(BF16), w2_ref[kf * FF_CH:(kf + 1) * FF_CH, :])
        acc = o if acc is None else acc + o
    o_ref[...] = x + mg_ref[...] * acc


def _ffn(xs, g, ms, w1, w2):
    b = xs.shape[0]
    row = pl.BlockSpec((None, TR, D), lambda b_, j: (b_, j, 0))
    return pl.pallas_call(
        _ffn_kernel,
        out_shape=jax.ShapeDtypeStruct((b, T, D), F32),
        grid=(b, NRT),
        in_specs=[row, _const_spec((1, D)), _mod_spec(3), _mod_spec(4), _mod_spec(5),
                  _const_spec((D, D_FF)), _const_spec((D_FF, D))],
        out_specs=row,
        compiler_params=_cparams(2),
        name="ffn",
    )(xs, g.reshape(1, D), ms, ms, ms, w1, w2)


def _final_kernel(x_ref, g_ref, o_ref):
    o_ref[...] = _rms(x_ref[...], g_ref[...])


def _final_norm(xs, g):
    b = xs.shape[0]
    return pl.pallas_call(
        _final_kernel,
        out_shape=jax.ShapeDtypeStruct((b, L, D), F32),
        grid=(b, L // TR),
        in_specs=[pl.BlockSpec((None, TR, D), lambda b_, j: (b_, j, 0)), _const_spec((1, D))],
        out_specs=pl.BlockSpec((None, TR, D), lambda b_, j: (b_, j, 0)),
        compiler_params=_cparams(2),
        name="final_norm",
    )(xs, g.reshape(1, D))


def _prep_w_in(w):
    def dup(a):
        a = a.reshape(D, SWA_KV, 1, 64)
        return jnp.concatenate([a, a], axis=2).reshape(D, SWA_KV * 128)
    cat = jnp.concatenate([
        w[:, R_XBC:R_DT], w[:, R_NAK:R_NAV], w[:, R_NAV:R_SWAK], dup(w[:, R_SWAK:R_SWAV]),
        dup(w[:, R_SWAV:R_Z]), w[:, R_Z:R_NAQ], w[:, R_NAQ:R_SWAQ], w[:, R_SWAQ:R_GATE],
        w[:, R_GATE:R_END]], axis=1).astype(BF16)
    wdt = jnp.concatenate([w[:, R_DT:R_NAK], jnp.zeros((D, Q - 2 * SSD_HEADS), F32)], axis=1).astype(BF16)
    return cat, wdt


def kernel(x, c, ctx, c_ctx, ada_w, ada_b, norm1_g, norm2_g, w_in, conv_w, conv_b, dt_bias, a_log, ssd_d,
           ssd_norm_g, na_rpb, swa_sink, w_o_ssd, w_o_na, w_o_swa, w_out, w_ff1, w_ff2, final_g):
    b = x.shape[0]
    depth = ada_w.shape[0]
    xs = jnp.concatenate([x, ctx], axis=1)
    nrow = -(-(b + 1) // 8) * 8
    cvec = jnp.concatenate([c, c_ctx[None, :], jnp.zeros((nrow - b - 1, D), F32)], axis=0)
    mod = _ada_mod(cvec, ada_w, ada_b)
    lat = mod[:, :b].reshape(depth, b, 1, N_MOD, 1, D)
    cx = jnp.broadcast_to(mod[:, b].reshape(depth, 1, 1, N_MOD, 1, D), lat.shape)
    ms_all = jnp.concatenate([lat, cx], axis=2)
    cos_t, sin_t = _rope_tables()
    mask_t = _swa_mask_table()
    na_bias = _na_bias_tables(na_rpb)

    for l in range(depth):
        ms = ms_all[l]
        wcat, wdt = _prep_w_in(w_in[l])
        h = _normmod(xs, norm1_g[l], ms)
        p = _matmul(h, wcat, BF16, 1024, "in_proj")
        dtr = _matmul(h, wdt, F32, Q, "dt_proj")
        xa = _conv_silu(p, conv_w[l], conv_b[l])
        ya = _ssd(xa, p, dtr, dt_bias[l], a_log[l], ssd_d[l], ssd_norm_g[l])
        yb = _na(p, na_bias[l])
        yc = _swa(p, swa_sink[l], cos_t, sin_t, mask_t)
        xs = _merge(xs, ya, yb, yc, p, ms, w_o_ssd[l].astype(BF16), w_o_na[l].astype(BF16),
                    w_o_swa[l].astype(BF16), w_out[l].astype(BF16))
        xs = _ffn(xs, norm2_g[l], ms, w_ff1[l].astype(BF16), w_ff2[l].astype(BF16))
    return _final_norm(xs, final_g)
```

```python
import numpy as np
import jax
import jax.numpy as jnp
from jax import lax
from jax.experimental import pallas as pl
from jax.experimental.pallas import tpu as pltpu

F32 = jnp.float32
BF16 = jnp.bfloat16
HIGHEST = lax.Precision.HIGHEST

D = 1024
L = 2048
NC = 256
T = L + NC
GRID_W = 64
ROWS = L // GRID_W
EPS = 1e-6
N_MOD = 6

SSD_INNER = 2 * D
SSD_HEADS = 32
SSD_GROUPS = 8
SSD_HPG = 4
SSD_STATE = 128
SSD_XBC = SSD_INNER + 2 * SSD_GROUPS * SSD_STATE
Q = 128
NCH = T // Q

NA_HEADS = 16
NA_KH = 8
NA_KW = 16
NA_QROWS = 2
NA_KROWS = 10
NA_NQ = NA_QROWS * GRID_W
NA_NK = NA_KROWS * GRID_W
NA_CASES = 5
NA_BLK = 2

SWA_HEADS = 16
SWA_KV = 4
SWA_W = 128
ROPE_BASE = 10000.0
D_FF = 4 * D

R_XBC, R_DT, R_NAK, R_NAV, R_SWAK, R_SWAV, R_Z, R_NAQ, R_SWAQ, R_GATE, R_END = (
    0, 4096, 4160, 5184, 6208, 6464, 6720, 8768, 9792, 10816, 13888)
C_XBC, C_NAK, C_NAV, C_SWAK, C_SWAV, C_Z, C_NAQ, C_SWAQ, C_GATE, NCAT = (
    0, 4096, 5120, 6144, 6656, 7168, 9216, 10240, 11264, 14336)

NEG = -1e30
LOG2E = float(np.log2(np.e))
QSCALE = 0.125 * LOG2E
VMEM_LIMIT = 48 * 1024 * 1024


def _cparams(n_axes):
    return pltpu.CompilerParams(dimension_semantics=("parallel",) * n_axes,
                                vmem_limit_bytes=VMEM_LIMIT)


def _sigmoid(x):
    return 1.0 / (1.0 + jnp.exp(-x))


def _softplus(x):
    return jnp.maximum(x, 0.0) + jnp.log1p(jnp.exp(-jnp.abs(x)))


def _nt(a, b):
    return lax.dot_general(a, b, (((1,), (1,)), ((), ())), preferred_element_type=F32)


def _tn(a, b):
    return lax.dot_general(a, b, (((0,), (0,)), ((), ())), preferred_element_type=F32)


def _dot(a, b):
    return jnp.dot(a, b, preferred_element_type=F32)


def _chunk_loop(n, unroll, body):
    def wrapped(c, carry):
        body(c, pl.multiple_of(c * Q, Q))
        return carry
    lax.fori_loop(0, n, wrapped, 0, unroll=unroll)


def _ada_kernel(c_ref, w_ref, b_ref, o_ref):
    c = c_ref[...]
    s = (c * _sigmoid(c)).astype(BF16)
    o_ref[...] = _dot(s, w_ref[...].astype(BF16)) + b_ref[...]


def _ada_mod(cvec, ada_w, ada_b):
    depth = ada_w.shape[0]
    r = cvec.shape[0]
    return pl.pallas_call(
        _ada_kernel,
        out_shape=jax.ShapeDtypeStruct((depth, r, N_MOD * D), F32),
        grid=(depth, N_MOD),
        in_specs=[pl.BlockSpec((r, D), lambda l, n: (0, 0)),
                  pl.BlockSpec((None, D, D), lambda l, n: (l, 0, n)),
                  pl.BlockSpec((None, 1, D), lambda l, n: (l, 0, n))],
        out_specs=pl.BlockSpec((None, r, D), lambda l, n: (l, 0, n)),
        compiler_params=_cparams(2),
        name="ada_mod",
    )(cvec, ada_w, ada_b.reshape(depth, 1, N_MOD * D))


TR = 256
NRT = T // TR


def _mod_spec(which):
    return pl.BlockSpec((None, None, None, 1, D), lambda b, j: (b, j // (L // TR), which, 0, 0))


def _rms(x, g):
    return x * lax.rsqrt(jnp.mean(x * x, axis=-1, keepdims=True) + EPS) * g


def _normmod_kernel(x_ref, g_ref, sh_ref, sc_ref, o_ref):
    y = _rms(x_ref[...], g_ref[...])
    o_ref[...] = (y * (1.0 + sc_ref[...]) + sh_ref[...]).astype(o_ref.dtype)


def _normmod(xs, g, ms):
    b = xs.shape[0]
    return pl.pallas_call(
        _normmod_kernel,
        out_shape=jax.ShapeDtypeStruct((b, T, D), BF16),
        grid=(b, NRT),
        in_specs=[pl.BlockSpec((None, TR, D), lambda b, j: (b, j, 0)),
                  pl.BlockSpec((1, D), lambda b, j: (0, 0)),
                  _mod_spec(0), _mod_spec(1)],
        out_specs=pl.BlockSpec((None, TR, D), lambda b, j: (b, j, 0)),
        compiler_params=_cparams(2),
        name="normmod",
    )(xs, g.reshape(1, D), ms, ms)


def _mm_kernel(x_ref, w_ref, o_ref):
    o_ref[...] = _dot(x_ref[...], w_ref[...]).astype(o_ref.dtype)


def _matmul(x, w, out_dtype, tn, name):
    b, t, k = x.shape
    n = w.shape[1]
    return pl.pallas_call(
        _mm_kernel,
        out_shape=jax.ShapeDtypeStruct((b, t, n), out_dtype),
        grid=(n // tn, b),
        in_specs=[pl.BlockSpec((None, t, k), lambda n_, b_: (b_, 0, 0)),
                  pl.BlockSpec((k, tn), lambda n_, b_: (0, n_))],
        out_specs=pl.BlockSpec((None, t, tn), lambda n_, b_: (b_, 0, n_)),
        compiler_params=_cparams(2),
        name=name,
    )(x, w)


CW = 512
CONV_RC = 256
PAD_LAT = 8
PAD_CTX = 16 + L


def _conv_kernel(x_ref, w_ref, b_ref, o_ref, pad_ref):
    z8 = jnp.zeros((8, CW), F32)
    pad_ref[0:8, :] = z8
    pad_ref[PAD_LAT + L:PAD_CTX, :] = z8
    pad_ref[PAD_CTX + NC:PAD_CTX + NC + 8, :] = z8
    pad_ref[PAD_LAT:PAD_LAT + L, :] = x_ref[0:L, :].astype(F32)
    pad_ref[PAD_CTX:PAD_CTX + NC, :] = x_ref[L:T, :].astype(F32)
    w = w_ref[...]
    bias = b_ref[...]
    for c in range(T // CONV_RC):
        r0 = c * CONV_RC
        base = (PAD_LAT if r0 < L else PAD_CTX - L) + r0
        win = pad_ref[base - 8:base + CONV_RC + 8, :]
        acc = bias + win[8:8 + CONV_RC] * w[2:3, :]
        for k in (0, 1, 3, 4):
            shifted = pltpu.roll(win, (2 - k) % (CONV_RC + 16), axis=0)
            acc = acc + shifted[8:8 + CONV_RC] * w[k:k + 1, :]
        o_ref[r0:r0 + CONV_RC, :] = (acc * _sigmoid(acc)).astype(o_ref.dtype)


def _conv_silu(p, conv_w, conv_b):
    b = p.shape[0]
    w8 = jnp.concatenate([conv_w, jnp.zeros((3, SSD_XBC), F32)], axis=0)
    return pl.pallas_call(
        _conv_kernel,
        out_shape=jax.ShapeDtypeStruct((b, T, SSD_XBC), BF16),
        grid=(b, SSD_XBC // CW),
        in_specs=[pl.BlockSpec((None, T, CW), lambda b_, n: (b_, 0, n)),
                  pl.BlockSpec((8, CW), lambda b_, n: (0, n)),
                  pl.BlockSpec((1, CW), lambda b_, n: (0, n))],
        out_specs=pl.BlockSpec((None, T, CW), lambda b_, n: (b_, 0, n)),
        scratch_shapes=[pltpu.VMEM((T + 24, CW), F32)],
        compiler_params=_cparams(2),
        name="conv_silu",
    )(p, w8, conv_b.reshape(1, SSD_XBC))


GW = SSD_HPG * 64
TS = 16


def _split3(x):
    hi = x.astype(BF16)
    r = x - hi.astype(F32)
    mid = r.astype(BF16)
    lo = (r - mid.astype(F32)).astype(BF16)
    return hi, mid, lo


def _ssd_kernel(x_ref, b_ref, c_ref, z_ref, dt_ref, biasr_ref, alogr_ref, dsk_ref, ng_ref, o_ref,
                dt_s, acc_s, wts_s, eacc_s, rowt_s, dtt_s, xw_s, ec_s, big_s, sin_s, cb_s, y_s):
    g = pl.program_id(1)
    ii = lax.broadcasted_iota(jnp.int32, (Q, Q), 0)
    jj = lax.broadcasted_iota(jnp.int32, (Q, Q), 1)
    tril = jnp.where(ii >= jj, 1.0, 0.0).astype(BF16)
    triu = jnp.where(ii <= jj, 1.0, 0.0).astype(BF16)
    lcat = jnp.concatenate([jnp.concatenate([tril] * 3, axis=1),
                            jnp.concatenate([triu] * 3, axis=1)], axis=0)
    low = ii > jj
    up = ii < jj
    lo = jj < 64
    tgt = jnp.where(jj < 4, g * 4 + jj, 32 + g * 4 + jj - 4)
    sel = jnp.where((ii == tgt) & (jj < 8), 1.0, 0.0).astype(BF16)
    sel3 = jnp.concatenate([sel] * 3, axis=0)
    ek = lax.broadcasted_iota(jnp.int32, (Q, 2 * GW), 0)
    ech = lax.broadcasted_iota(jnp.int32, (Q, 2 * GW), 1)
    esel = jnp.where(ek == ech // 64, 1.0, 0.0).astype(BF16)
    esel2 = jnp.concatenate([esel] * 2, axis=0)
    isf = lax.broadcasted_iota(jnp.int32, (1, Q), 1) < 4
    bias_r = biasr_ref[...]
    a_r = -jnp.exp(alogr_ref[...])
    zero_x = jnp.zeros((Q, Q), BF16)
    rows = lambda t0: pl.ds(t0, Q)

    def stage_dt(c, t0):
        raw3 = _split3(dt_ref[rows(t0), :])
        dt_s[rows(t0), :] = _softplus(_dot(jnp.concatenate(raw3, axis=1), sel3) + bias_r)

    _chunk_loop(NCH, 6, stage_dt)

    def stage_cum(c, t0):
        dt = dt_s[rows(t0), :]
        both = _dot(lcat, jnp.concatenate(_split3(dt * a_r), axis=0))
        acc = jnp.where(isf, both[0:Q], both[Q:2 * Q])
        tot = jnp.where(isf, both[Q - 1:Q], both[Q:Q + 1])
        acc_s[rows(t0), :] = acc
        wts_s[rows(t0), :] = dt * jnp.exp(tot - acc)
        eacc_s[rows(t0), :] = jnp.exp(acc)
        dt_t = dt.T[0:TS]
        rowt_s[:, rows(t0)] = acc.T[0:TS] - jnp.log(dt_t)
        dtt_s[:, rows(t0)] = dt_t

    _chunk_loop(NCH, 3, stage_cum)

    def stage_expand(c, t0):
        w2 = _split3(wts_s[rows(t0), :])[:2]
        e2 = _split3(eacc_s[rows(t0), :])[:2]
        lhs = jnp.concatenate([jnp.concatenate(w2, axis=1), jnp.concatenate(e2, axis=1)], axis=0)
        both = _dot(lhs, esel2)
        xs = x_ref[rows(t0), :].astype(F32)
        xw_s[rows(t0), :] = (jnp.concatenate([xs, xs], axis=1) * both[0:Q]).astype(BF16)
        ec_s[rows(t0), :] = both[Q:2 * Q]

    _chunk_loop(NCH, 3, stage_expand)

    def stage_upd(c, t0):
        big_s[c] = _tn(b_ref[rows(t0), :], xw_s[rows(t0), :])

    _chunk_loop(NCH, 6, stage_upd)

    s = jnp.zeros((SSD_STATE, GW), F32)
    for c in list(range(L // Q, NCH)) + list(range(L // Q)):
        sin_s[c, :, 0:GW] = s.astype(BF16)
        s = s * ec_s[c * Q + Q - 1:c * Q + Q, 0:GW] + big_s[c, :, 0:GW]
    s = jnp.zeros((SSD_STATE, GW), F32)
    for c in reversed(range(NCH)):
        sin_s[c, :, GW:2 * GW] = s.astype(BF16)
        s = s * ec_s[c * Q:c * Q + 1, GW:2 * GW] + big_s[c, :, GW:2 * GW]

    def stage_cb(c, t0):
        cc = c_ref[rows(t0), :]
        cb_s[rows(t0), :] = _nt(cc, b_ref[rows(t0), :])
        big_s[c] = _dot(cc, sin_s[c])

    _chunk_loop(NCH, 6, stage_cb)

    dsk = dsk_ref[...]
    ng = ng_ref[...]

    def colb(v, k):
        return jnp.broadcast_to(v[:, k:k + 1], (Q, Q))

    def stage_y(c, t0):
        xb = x_ref[rows(t0), :]
        acc = acc_s[rows(t0), :]
        rowt = rowt_s[:, rows(t0)]
        dt_t = dtt_s[:, rows(t0)]
        ec = ec_s[rows(t0), :]
        cb = cb_s[rows(t0), :]
        yoff = big_s[c]
        ys = []
        for pr in range(2):
            ms = []
            for hh in range(2):
                r = 2 * pr + hh
                diag = jnp.log(dt_t[r:r + 1, :] + dt_t[4 + r:5 + r, :])
                seg = jnp.where(low, colb(acc, r) - rowt[r:r + 1, :],
                                jnp.where(up, colb(acc, 4 + r) - rowt[4 + r:5 + r, :], diag))
                ms.append((cb * jnp.exp(seg)).astype(BF16))
            xp = xb[:, pr * Q:(pr + 1) * Q]
            xstack = jnp.concatenate([jnp.where(lo, xp, zero_x), jnp.where(lo, zero_x, xp)], axis=0)
            y = _dot(jnp.concatenate(ms, axis=1), xstack)
            y = (y + ec[:, pr * Q:(pr + 1) * Q] * yoff[:, pr * Q:(pr + 1) * Q]
                 + ec[:, GW + pr * Q:GW + (pr + 1) * Q] * yoff[:, GW + pr * Q:GW + (pr + 1) * Q])
            ys.append(y)
        y_s[rows(t0), :] = jnp.concatenate(ys, axis=1)

    _chunk_loop(NCH, 2, stage_y)

    def stage_out(c, t0):
        y = y_s[rows(t0), :] + dsk * x_ref[rows(t0), :].astype(F32)
        zf = z_ref[rows(t0), :].astype(F32)
        u = y * (zf * _sigmoid(zf))
        o_ref[rows(t0), :] = _rms(u, ng).astype(o_ref.dtype)

    _chunk_loop(NCH, 3, stage_out)


def _ssd(xa, p, dtr, dt_bias, a_log, ssd_d, ssd_norm_g):
    b = xa.shape[0]
    def slots(v):
        s = v.reshape(2, SSD_GROUPS, SSD_HPG).transpose(1, 0, 2).reshape(SSD_GROUPS, 8)
        return jnp.pad(s, ((0, 0), (0, Q - 8))).reshape(SSD_GROUPS, 1, Q)
    dsk = jnp.repeat(ssd_d, 64).reshape(1, SSD_INNER)
    ng = ssd_norm_g.reshape(1, SSD_INNER)
    xoff, boff, coff, zoff = 0, SSD_INNER // Q, (SSD_INNER + SSD_GROUPS * SSD_STATE) // Q, C_Z // GW
    return pl.pallas_call(
        _ssd_kernel,
        out_shape=jax.ShapeDtypeStruct((b, T, SSD_INNER), BF16),
        grid=(b, SSD_GROUPS),
        in_specs=[pl.BlockSpec((None, T, GW), lambda b_, g: (b_, 0, xoff + g)),
                  pl.BlockSpec((None, T, Q), lambda b_, g: (b_, 0, boff + g)),
                  pl.BlockSpec((None, T, Q), lambda b_, g: (b_, 0, coff + g)),
                  pl.BlockSpec((None, T, GW), lambda b_, g: (b_, 0, zoff + g)),
                  pl.BlockSpec((None, T, Q), lambda b_, g: (b_, 0, 0)),
                  pl.BlockSpec((None, 1, Q), lambda b_, g: (g, 0, 0)),
                  pl.BlockSpec((None, 1, Q), lambda b_, g: (g, 0, 0)),
                  pl.BlockSpec((1, GW), lambda b_, g: (0, g)),
                  pl.BlockSpec((1, GW), lambda b_, g: (0, g))],
        out_specs=pl.BlockSpec((None, T, GW), lambda b_, g: (b_, 0, g)),
        scratch_shapes=[pltpu.VMEM((T, Q), F32),
                        pltpu.VMEM((T, Q), F32),
                        pltpu.VMEM((T, Q), F32),
                        pltpu.VMEM((T, Q), F32),
                        pltpu.VMEM((TS, T), F32),
                        pltpu.VMEM((TS, T), F32),
                        pltpu.VMEM((T, 2 * GW), BF16),
                        pltpu.VMEM((T, 2 * GW), F32),
                        pltpu.VMEM((NCH, SSD_STATE, 2 * GW), F32),
                        pltpu.VMEM((NCH, SSD_STATE, 2 * GW), BF16),
                        pltpu.VMEM((T, Q), F32),
                        pltpu.VMEM((T, GW), F32)],
        compiler_params=_cparams(2),
        name="ssd",
    )(xa, xa, xa, p, dtr, slots(dt_bias), slots(a_log), dsk, ng)


def _softmax_t(parts, extra=None):
    mx = None
    for s in parts:
        r = jnp.max(s, axis=0, keepdims=True)
        mx = r if mx is None else jnp.maximum(mx, r)
    if extra is not None:
        mx = jnp.maximum(mx, extra)
    den = None
    es = []
    for s in parts:
        e = jnp.exp2(s - mx)
        d = jnp.sum(e, axis=0, keepdims=True)
        den = d if den is None else den + d
        es.append(e.astype(BF16))
    if extra is not None:
        den = den + jnp.exp2(extra - mx)
    return es, 1.0 / den


def _pv_t(vts, es):
    out = None
    for vt, e in zip(vts, es):
        o = _dot(vt, e)
        out = o if out is None else out + o
    return out


def _na_kernel(q_ref, k_ref, v_ref, bias_ref, o_ref, vt_ref, sa_ref, sb_ref, pa_ref, pb_ref, ra_ref, rb_ref):
    lane = lax.broadcasted_iota(jnp.int32, (NA_NQ, Q), 1)
    lo = lane < 64
    top = lax.broadcasted_iota(jnp.int32, (Q, NA_NQ), 0) < 64
    zero = jnp.zeros((NA_NQ, Q), BF16)

    def vt_body(c, t0):
        vt_ref[:, pl.ds(t0, Q)] = v_ref[pl.ds(t0, Q), :].astype(F32).T.astype(BF16)

    _chunk_loop(NCH, 6, vt_body)

    kc = k_ref[L:T, :]
    vtc = vt_ref[:, L:T]

    def qpair(q0):
        q = (q_ref[pl.ds(q0, NA_NQ), :].astype(F32) * QSCALE).astype(BF16)
        return jnp.concatenate([jnp.where(lo, q, zero), jnp.where(lo, zero, q)], axis=0)

    def finish(q0, ot, rden):
        ot = ot * rden
        w = jnp.where(top, ot[:, 0:NA_NQ], ot[:, NA_NQ:2 * NA_NQ])
        o_ref[pl.ds(q0, NA_NQ), :] = w.T.astype(o_ref.dtype)

    def block_params(j):
        q0 = pl.multiple_of(j * NA_NQ, NA_NQ)
        start = jnp.clip(NA_QROWS * j - NA_KH // 2, 0, ROWS - NA_KROWS)
        koff = pl.multiple_of(start * GRID_W, 2 * GRID_W)
        last = ROWS // NA_QROWS - 1
        case = (jnp.where(j >= 1, 1, 0) + jnp.where(j >= 2, 1, 0)
                + jnp.where(j >= last - 1, 1, 0) + jnp.where(j >= last, 1, 0))
        return q0, koff, case

    def qk(j, s_ref):
        q0, koff, case = block_params(j)
        qp = qpair(q0)
        s_ref[0:NA_NK, :] = _nt(k_ref[pl.ds(koff, NA_NK), :], qp) + bias_ref[case]
        s_ref[NA_NK:NA_NK + NC, :] = _nt(kc, qp)

    def soft(s_ref, p_ref, r_ref):
        s = s_ref[...]
        e = jnp.exp2(s - jnp.max(s, axis=0, keepdims=True))
        r_ref[...] = jnp.broadcast_to(1.0 / jnp.sum(e, axis=0, keepdims=True), r_ref.shape)
        p_ref[...] = e.astype(BF16)

    def pv(j, p_ref, r_ref):
        q0, koff, _ = block_params(j)
        ot = (_dot(vt_ref[:, pl.ds(koff, NA_NK)], p_ref[0:NA_NK, :])
              + _dot(vtc, p_ref[NA_NK:NA_NK + NC, :]))
        finish(q0, ot, r_ref[0:1, :])

    nblk = L // NA_NQ
    qk(0, sa_ref)
    qk(1, sb_ref)
    soft(sa_ref, pa_ref, ra_ref)

    def body(m, carry):
        pv(2 * m - 2, pa_ref, ra_ref)
        qk(2 * m, sa_ref)
        soft(sb_ref, pb_ref, rb_ref)
        pv(2 * m - 1, pb_ref, rb_ref)
        qk(2 * m + 1, sb_ref)
        soft(sa_ref, pa_ref, ra_ref)
        return carry

    lax.fori_loop(1, nblk // 2, body, 0)
    pv(nblk - 2, pa_ref, ra_ref)
    soft(sb_ref, pb_ref, rb_ref)
    pv(nblk - 1, pb_ref, rb_ref)

    ctx_blocks = []
    for cbk in range(NC // NA_NQ):
        q0 = L + cbk * NA_NQ
        ctx_blocks.append((q0, _nt(kc, qpair(q0))))
    ctx_soft = [(q0,) + _softmax_t([s]) for q0, s in ctx_blocks]
    for q0, es, rden in ctx_soft:
        finish(q0, _pv_t([vtc], es), rden)


def _na_bias_tables(rpb_all):
    depth = rpb_all.shape[0]
    cq = np.arange(GRID_W)
    ck = np.arange(GRID_W)
    cd = ck[None, :] - cq[:, None] + NA_KW - 1
    col_oh = (cd[None] == np.arange(2 * NA_KW - 1)[:, None, None]).astype(np.float32)
    cs = np.clip(cq - NA_KW // 2, 0, GRID_W - NA_KW)
    col_ok = (ck[None, :] >= cs[:, None]) & (ck[None, :] < cs[:, None] + NA_KW)
    j_rep = [0, 1, 2, ROWS // NA_QROWS - 2, ROWS // NA_QROWS - 1]
    n_off = 2 * NA_KH - 1
    rp = rpb_all.reshape(depth, NA_HEADS // 2, 2, n_off, 2 * NA_KW - 1)
    colexp = jnp.einsum("lperd,dab->lprbea", rp, col_oh, precision=HIGHEST)
    colexp = jnp.where(col_ok.T[None, None, None, :, None, :], colexp * LOG2E, NEG)
    colexp = jnp.concatenate([colexp, jnp.full_like(colexp[:, :, :1], NEG)], axis=2)
    slabs = []
    for j in j_rep:
        start = int(np.clip(NA_QROWS * j - NA_KH // 2, 0, ROWS - NA_KROWS))
        for i in range(NA_KROWS):
            kr = start + i
            per_qr = []
            for qr in range(NA_QROWS):
                r = NA_QROWS * j + qr
                rs = int(np.clip(r - NA_KH // 2, 0, ROWS - NA_KH))
                off = kr - r + NA_KH - 1 if rs <= kr < rs + NA_KH else n_off
                per_qr.append(colexp[:, :, off])
            slabs.append(jnp.stack(per_qr, axis=4))
    tab = jnp.stack(slabs, axis=2)
    return tab.reshape(depth, NA_HEADS // 2, NA_CASES, NA_NK, 2 * NA_NQ)


def _na(p, bias_tab):
    b = p.shape[0]
    qo, ko, vo = C_NAQ // Q, C_NAK // Q, C_NAV // Q
    return pl.pallas_call(
        _na_kernel,
        out_shape=jax.ShapeDtypeStruct((b, T, D), BF16),
        grid=(b, NA_HEADS // 2),
        in_specs=[pl.BlockSpec((None, T, Q), lambda b_, h: (b_, 0, qo + h)),
                  pl.BlockSpec((None, T, Q), lambda b_, h: (b_, 0, ko + h)),
                  pl.BlockSpec((None, T, Q), lambda b_, h: (b_, 0, vo + h)),
                  pl.BlockSpec((None, NA_CASES, NA_NK, 2 * NA_NQ), lambda b_, h: (h, 0, 0, 0))],
        out_specs=pl.BlockSpec((None, T, Q), lambda b_, h: (b_, 0, h)),
        scratch_shapes=[pltpu.VMEM((Q, T), BF16),
                        pltpu.VMEM((NA_NK + NC, 2 * NA_NQ), F32),
                        pltpu.VMEM((NA_NK + NC, 2 * NA_NQ), F32),
                        pltpu.VMEM((NA_NK + NC, 2 * NA_NQ), BF16),
                        pltpu.VMEM((NA_NK + NC, 2 * NA_NQ), BF16),
                        pltpu.VMEM((8, 2 * NA_NQ), F32),
                        pltpu.VMEM((8, 2 * NA_NQ), F32)],
        compiler_params=_cparams(2),
        name="na_attn",
    )(p, p, p, bias_tab)


SWA_KWIN = 3 * SWA_W
SWA_QB = 128
SWA_STACK = 4 * SWA_QB
ROPE_RC = 256


def _swap16(t, lane):
    a = pltpu.roll(t, 112, axis=1)
    b = pltpu.roll(t, 16, axis=1)
    return jnp.where((lane % 32) < 16, a, b)


def _swa_kernel(sink_ref, q_ref, k_ref, v_ref, cos_ref, sin_ref, mask_ref, o_ref, krot_ref, vt_ref,
                sa_ref, sb_ref, pa_ref, pb_ref, ra_ref, rb_ref):
    kp = pl.program_id(1)
    lane = lax.broadcasted_iota(jnp.int32, (SWA_QB, Q), 1)
    lo = lane < 64
    zero = jnp.zeros((SWA_QB, Q), BF16)
    col = lax.broadcasted_iota(jnp.int32, (1, SWA_STACK), 1)
    eye4 = jnp.where(lax.broadcasted_iota(jnp.int32, (SWA_STACK, Q), 0) % SWA_QB
                     == lax.broadcasted_iota(jnp.int32, (SWA_STACK, Q), 1), 1.0, 0.0).astype(BF16)

    def rope(t, r0, n):
        lane_n = lax.broadcasted_iota(jnp.int32, (n, Q), 1)
        return t * cos_ref[pl.ds(r0, n), :] + _swap16(t, lane_n) * sin_ref[pl.ds(r0, n), :]

    def krot_body(i, carry):
        r0 = pl.multiple_of(i * ROPE_RC, ROPE_RC)
        for gl in range(2):
            t = k_ref[pl.ds(r0, ROPE_RC), gl * Q:(gl + 1) * Q].astype(F32)
            krot_ref[pl.ds(r0, ROPE_RC), gl * Q:(gl + 1) * Q] = rope(t, r0, ROPE_RC).astype(BF16)
        return carry

    lax.fori_loop(0, L // ROPE_RC, krot_body, 0, unroll=2)

    def vt_body(c, t0):
        for gl in range(2):
            vt = v_ref[pl.ds(t0, Q), gl * Q:(gl + 1) * Q].astype(F32).T
            vt_ref[gl, :, pl.ds(t0, Q)] = vt[0:64].astype(BF16)

    _chunk_loop(NCH, 6, vt_body)

    def stack_q(qa, qb):
        return jnp.concatenate([jnp.where(lo, qa, zero), jnp.where(lo, zero, qa),
                                jnp.where(lo, qb, zero), jnp.where(lo, zero, qb)], axis=0)

    def sink_row(gl):
        base = kp * 8 + gl * 4
        row = jnp.where(col < SWA_QB, sink_ref[base],
                        jnp.where(col < 2 * SWA_QB, sink_ref[base + 1],
                                  jnp.where(col < 3 * SWA_QB, sink_ref[base + 2], sink_ref[base + 3])))
        return row * LOG2E

    def finish(q0, gl, ot, rden):
        ot = ot * rden
        pa = jnp.concatenate([ot[:, 0:SWA_QB], ot[:, SWA_QB:2 * SWA_QB]], axis=0).T
        pb = jnp.concatenate([ot[:, 2 * SWA_QB:3 * SWA_QB], ot[:, 3 * SWA_QB:4 * SWA_QB]], axis=0).T
        o_ref[pl.ds(q0, SWA_QB), gl * 2 * Q:(gl + 1) * 2 * Q] = jnp.concatenate([pa, pb], axis=1).astype(o_ref.dtype)

    def block_params(i):
        q0 = pl.multiple_of(i * SWA_QB, SWA_QB)
        ws = pl.multiple_of(jnp.clip((i - 1) * SWA_W, 0, L - SWA_KWIN), SWA_W)
        case = jnp.where(i >= 1, 1, 0) + jnp.where(i >= L // SWA_QB - 1, 1, 0)
        return q0, ws, case

    def qk(i, gl, s_ref):
        q0, ws, case = block_params(i)
        qa = rope(q_ref[pl.ds(q0, SWA_QB), gl * 2 * Q:gl * 2 * Q + Q].astype(F32), q0, SWA_QB) * QSCALE
        qb = rope(q_ref[pl.ds(q0, SWA_QB), gl * 2 * Q + Q:(gl + 1) * 2 * Q].astype(F32), q0, SWA_QB) * QSCALE
        qs = stack_q(qa.astype(BF16), qb.astype(BF16))
        a_loc = jnp.concatenate([krot_ref[pl.ds(ws, SWA_KWIN), gl * Q:(gl + 1) * Q], mask_ref[case]], axis=1)
        s_ref[0:SWA_KWIN, :] = _nt(a_loc, jnp.concatenate([qs, eye4], axis=1))
        s_ref[SWA_KWIN:SWA_KWIN + NC, :] = _nt(k_ref[L:T, gl * Q:(gl + 1) * Q], qs)

    def soft(gl, s_ref, p_ref, r_ref):
        s = s_ref[...]
        sink = sink_row(gl)
        mx = jnp.maximum(jnp.max(s, axis=0, keepdims=True), sink)
        e = jnp.exp2(s - mx)
        den = jnp.sum(e, axis=0, keepdims=True) + jnp.exp2(sink - mx)
        r_ref[...] = jnp.broadcast_to(1.0 / den, r_ref.shape)
        p_ref[...] = e.astype(BF16)

    def pv(i, gl, p_ref, r_ref):
        q0, ws, _ = block_params(i)
        ot = (_dot(vt_ref[gl, :, pl.ds(ws, SWA_KWIN)], p_ref[0:SWA_KWIN, :])
              + _dot(vt_ref[gl, :, L:T], p_ref[SWA_KWIN:SWA_KWIN + NC, :]))
        finish(q0, gl, ot, r_ref[0:1, :])

    nblk = L // SWA_QB
    qk(0, 0, sa_ref)
    qk(0, 1, sb_ref)
    soft(0, sa_ref, pa_ref, ra_ref)

    def body(i, carry):
        pv(i - 1, 0, pa_ref, ra_ref)
        qk(i, 0, sa_ref)
        soft(1, sb_ref, pb_ref, rb_ref)
        pv(i - 1, 1, pb_ref, rb_ref)
        qk(i, 1, sb_ref)
        soft(0, sa_ref, pa_ref, ra_ref)
        return carry

    lax.fori_loop(1, nblk, body, 0)
    pv(nblk - 1, 0, pa_ref, ra_ref)
    soft(1, sb_ref, pb_ref, rb_ref)
    pv(nblk - 1, 1, pb_ref, rb_ref)

    for cbk in range(NC // SWA_QB):
        q0 = L + cbk * SWA_QB
        scores = []
        for gl in range(2):
            qa = q_ref[q0:q0 + SWA_QB, gl * 2 * Q:gl * 2 * Q + Q].astype(F32) * QSCALE
            qb = q_ref[q0:q0 + SWA_QB, gl * 2 * Q + Q:(gl + 1) * 2 * Q].astype(F32) * QSCALE
            scores.append(_nt(k_ref[L:T, gl * Q:(gl + 1) * Q], stack_q(qa.astype(BF16), qb.astype(BF16))))
        soft = [_softmax_t([s], sink_row(gl)) for gl, s in enumerate(scores)]
        for gl, (es, rden) in enumerate(soft):
            finish(q0, gl, _pv_t([vt_ref[gl, :, L:T]], es), rden)


def _rope_tables():
    pos = np.arange(L)
    lane = np.arange(Q) % 64
    inv = ROPE_BASE ** (-(lane % 16).astype(np.float64) / 16.0)
    p = np.where(lane[None, :] < 32, (pos // GRID_W)[:, None], (pos % GRID_W)[:, None]).astype(np.float64)
    ang = p * inv[None, :]
    sign = np.where((lane % 32) < 16, -1.0, 1.0)[None, :]
    return jnp.asarray(np.cos(ang), F32), jnp.asarray(np.sin(ang) * sign, F32)


def _swa_mask_table():
    key = np.arange(SWA_KWIN)[:, None]
    qry = np.arange(SWA_QB)[None, :]
    tabs = [np.where(np.abs(key - qry + delta) <= SWA_W, 0.0, NEG) for delta in (0, -SWA_W, -2 * SWA_W)]
    return jnp.asarray(np.stack(tabs), BF16)


def _swa(p, sink, cos_t, sin_t, mask_t):
    b = p.shape[0]
    qo, ko, vo = C_SWAQ // 512, C_SWAK // 256, C_SWAV // 256
    return pl.pallas_call(
        _swa_kernel,
        out_shape=jax.ShapeDtypeStruct((b, T, D), BF16),
        grid=(b, SWA_KV // 2),
        in_specs=[pl.BlockSpec(memory_space=pltpu.SMEM),
                  pl.BlockSpec((None, T, 512), lambda b_, h: (b_, 0, qo + h)),
                  pl.BlockSpec((None, T, 256), lambda b_, h: (b_, 0, ko + h)),
                  pl.BlockSpec((None, T, 256), lambda b_, h: (b_, 0, vo + h)),
                  pl.BlockSpec((L, Q), lambda b_, h: (0, 0)),
                  pl.BlockSpec((L, Q), lambda b_, h: (0, 0)),
                  pl.BlockSpec((3, SWA_KWIN, SWA_QB), lambda b_, h: (0, 0, 0))],
        out_specs=pl.BlockSpec((None, T, 512), lambda b_, h: (b_, 0, h)),
        scratch_shapes=[pltpu.VMEM((L, 2 * Q), BF16),
                        pltpu.VMEM((2, 64, T), BF16),
                        pltpu.VMEM((SWA_KWIN + NC, SWA_STACK), F32),
                        pltpu.VMEM((SWA_KWIN + NC, SWA_STACK), F32),
                        pltpu.VMEM((SWA_KWIN + NC, SWA_STACK), BF16),
                        pltpu.VMEM((SWA_KWIN + NC, SWA_STACK), BF16),
                        pltpu.VMEM((8, SWA_STACK), F32),
                        pltpu.VMEM((8, SWA_STACK), F32)],
        compiler_params=_cparams(2),
        name="swa_attn",
    )(sink, p, p, p, cos_t, sin_t, mask_t)


def _merge_kernel(x_ref, ya_ref, yb_ref, yc_ref, g0_ref, g1_ref, g2_ref, mg_ref,
                  wa_ref, wb_ref, wc_ref, wo_ref, o_ref):
    m = (_sigmoid(g0_ref[...].astype(F32)) * _dot(ya_ref[...], wa_ref[...])
         + _sigmoid(g1_ref[...].astype(F32)) * _dot(yb_ref[...], wb_ref[...])
         + _sigmoid(g2_ref[...].astype(F32)) * _dot(yc_ref[...], wc_ref[...]))
    o_ref[...] = x_ref[...] + mg_ref[...] * _dot(m.astype(BF16), wo_ref[...])


def _const_spec(shape):
    return pl.BlockSpec(shape, lambda b, j: (0,) * len(shape))


def _merge(xs, ya, yb, yc, p, ms, wa, wb, wc, wo):
    b = xs.shape[0]
    go = C_GATE // D
    row = lambda w: pl.BlockSpec((None, TR, w), lambda b_, j: (b_, j, 0))
    gate = lambda k: pl.BlockSpec((None, TR, D), lambda b_, j: (b_, j, go + k))
    return pl.pallas_call(
        _merge_kernel,
        out_shape=jax.ShapeDtypeStruct((b, T, D), F32),
        grid=(b, NRT),
        in_specs=[row(D), row(SSD_INNER), row(D), row(D), gate(0), gate(1), gate(2), _mod_spec(2),
                  _const_spec((SSD_INNER, D)), _const_spec((D, D)), _const_spec((D, D)), _const_spec((D, D))],
        out_specs=row(D),
        compiler_params=_cparams(2),
        name="merge",
    )(xs, ya, yb, yc, p, p, p, ms, wa, wb, wc, wo)


FF_CH = 1024


def _ffn_kernel(x_ref, g_ref, sh_ref, sc_ref, mg_ref, w1_ref, w2_ref, o_ref):
    x = x_ref[...]
    h = (_rms(x, g_ref[...]) * (1.0 + sc_ref[...]) + sh_ref[...]).astype(BF16)
    acc = None
    for kf in range(D_FF // FF_CH):
        a = jnp.maximum(_dot(h, w1_ref[:, kf * FF_CH:(kf + 1) * FF_CH]), 0.0)
        o = _dot((a * a).astype(BF16), w2_ref[kf * FF_CH:(kf + 1) * FF_CH, :])
        acc = o if acc is None else acc + o
    o_ref[...] = x + mg_ref[...] * acc


def _ffn(xs, g, ms, w1, w2):
    b = xs.shape[0]
    row = pl.BlockSpec((None, TR, D), lambda b_, j: (b_, j, 0))
    return pl.pallas_call(
        _ffn_kernel,
        out_shape=jax.ShapeDtypeStruct((b, T, D), F32),
        grid=(b, NRT),
        in_specs=[row, _const_spec((1, D)), _mod_spec(3), _mod_spec(4), _mod_spec(5),
                  _const_spec((D, D_FF)), _const_spec((D_FF, D))],
        out_specs=row,
        compiler_params=_cparams(2),
        name="ffn",
    )(xs, g.reshape(1, D), ms, ms, ms, w1, w2)


def _final_kernel(x_ref, g_ref, o_ref):
    o_ref[...] = _rms(x_ref[...], g_ref[...])


def _final_norm(xs, g):
    b = xs.shape[0]
    return pl.pallas_call(
        _final_kernel,
        out_shape=jax.ShapeDtypeStruct((b, L, D), F32),
        grid=(b, L // TR),
        in_specs=[pl.BlockSpec((None, TR, D), lambda b_, j: (b_, j, 0)), _const_spec((1, D))],
        out_specs=pl.BlockSpec((None, TR, D), lambda b_, j: (b_, j, 0)),
        compiler_params=_cparams(2),
        name="final_norm",
    )(xs, g.reshape(1, D))


def _prep_w_in(w):
    def dup(a):
        a = a.reshape(D, SWA_KV, 1, 64)
        return jnp.concatenate([a, a], axis=2).reshape(D, SWA_KV * 128)
    cat = jnp.concatenate([
        w[:, R_XBC:R_DT], w[:, R_NAK:R_NAV], w[:, R_NAV:R_SWAK], dup(w[:, R_SWAK:R_SWAV]),
        dup(w[:, R_SWAV:R_Z]), w[:, R_Z:R_NAQ], w[:, R_NAQ:R_SWAQ], w[:, R_SWAQ:R_GATE],
        w[:, R_GATE:R_END]], axis=1).astype(BF16)
    wdt = jnp.concatenate([w[:, R_DT:R_NAK], jnp.zeros((D, Q - 2 * SSD_HEADS), F32)], axis=1).astype(BF16)
    return cat, wdt


def kernel(x, c, ctx, c_ctx, ada_w, ada_b, norm1_g, norm2_g, w_in, conv_w, conv_b, dt_bias, a_log, ssd_d,
           ssd_norm_g, na_rpb, swa_sink, w_o_ssd, w_o_na, w_o_swa, w_out, w_ff1, w_ff2, final_g):
    b = x.shape[0]
    depth = ada_w.shape[0]
    xs = jnp.concatenate([x, ctx], axis=1)
    nrow = -(-(b + 1) // 8) * 8
    cvec = jnp.concatenate([c, c_ctx[None, :], jnp.zeros((nrow - b - 1, D), F32)], axis=0)
    mod = _ada_mod(cvec, ada_w, ada_b)
    lat = mod[:, :b].reshape(depth, b, 1, N_MOD, 1, D)
    cx = jnp.broadcast_to(mod[:, b].reshape(depth, 1, 1, N_MOD, 1, D), lat.shape)
    ms_all = jnp.concatenate([lat, cx], axis=2)
    cos_t, sin_t = _rope_tables()
    mask_t = _swa_mask_table()
    na_bias = _na_bias_tables(na_rpb)

    for l in range(depth):
        ms = ms_all[l]
        wcat, wdt = _prep_w_in(w_in[l])
        h = _normmod(xs, norm1_g[l], ms)
        p = _matmul(h, wcat, BF16, 1024, "in_proj")
        dtr = _matmul(h, wdt, F32, Q, "dt_proj")
        xa = _conv_silu(p, conv_w[l], conv_b[l])
        ya = _ssd(xa, p, dtr, dt_bias[l], a_log[l], ssd_d[l], ssd_norm_g[l])
        yb = _na(p, na_bias[l])
        yc = _swa(p, swa_sink[l], cos_t, sin_t, mask_t)
        xs = _merge(xs, ya, yb, yc, p, ms, w_o_ssd[l].astype(BF16), w_o_na[l].astype(BF16),
                    w_o_swa[l].astype(BF16), w_out[l].astype(BF16))
        xs = _ffn(xs, norm2_g[l], ms, w_ff1[l].astype(BF16), w_ff2[l].astype(BF16))
    return _final_norm(xs, final_g)
```

```python
import functools

import numpy as np
import jax
import jax.numpy as jnp
from jax import lax
from jax.experimental import pallas as pl
from jax.experimental.pallas import tpu as pltpu

F32 = jnp.float32
BF16 = jnp.bfloat16
HIGHEST = lax.Precision.HIGHEST

D = 1024
L = 2048
NC = 256
T = L + NC
GRID_W = 64
ROWS = L // GRID_W
EPS = 1e-6
N_MOD = 6

SSD_INNER = 2 * D
SSD_HEADS = 32
SSD_GROUPS = 8
SSD_HPG = 4
SSD_STATE = 128
SSD_XBC = SSD_INNER + 2 * SSD_GROUPS * SSD_STATE
Q = 128
NCH = T // Q

NA_HEADS = 16
NA_KH = 8
NA_KW = 16
NA_QROWS = 2
NA_KROWS = 10
NA_NQ = NA_QROWS * GRID_W
NA_NK = NA_KROWS * GRID_W
NA_CASES = 5
NA_NOFF = 2 * NA_KH - 1

SWA_HEADS = 16
SWA_KV = 4
SWA_W = 128
ROPE_BASE = 10000.0
D_FF = 4 * D

R_XBC, R_DT, R_NAK, R_NAV, R_SWAK, R_SWAV, R_Z, R_NAQ, R_SWAQ, R_GATE, R_END = (
    0, 4096, 4160, 5184, 6208, 6464, 6720, 8768, 9792, 10816, 13888)
C_XBC, C_NAK, C_NAV, C_SWAK, C_SWAV, C_Z, C_NAQ, C_SWAQ, C_GATE, NCAT = (
    0, 4096, 5120, 6144, 6656, 7168, 9216, 10240, 11264, 14336)

NEG = -1e30
LOG2E = float(np.log2(np.e))
QSCALE = 0.125 * LOG2E
VMEM_LIMIT = 48 * 1024 * 1024


def _cparams(n_axes):
    return pltpu.CompilerParams(dimension_semantics=("parallel",) * n_axes,
                                vmem_limit_bytes=VMEM_LIMIT)


def _sigmoid(x):
    return 1.0 / (1.0 + jnp.exp(-x))


def _softplus(x):
    return jnp.maximum(x, 0.0) + jnp.log1p(jnp.exp(-jnp.abs(x)))


def _nt(a, b):
    return lax.dot_general(a, b, (((1,), (1,)), ((), ())), preferred_element_type=F32)


def _tn(a, b):
    return lax.dot_general(a, b, (((0,), (0,)), ((), ())), preferred_element_type=F32)


def _dot(a, b):
    return jnp.dot(a, b, preferred_element_type=F32)


def _chunk_loop(n, unroll, body):
    def wrapped(c, carry):
        body(c, pl.multiple_of(c * Q, Q))
        return carry
    lax.fori_loop(0, n, wrapped, 0, unroll=unroll)


def _ada_kernel(c_ref, w_ref, b_ref, o_ref):
    c = c_ref[...]
    s = (c * _sigmoid(c)).astype(BF16)
    o_ref[...] = _dot(s, w_ref[...].astype(BF16)) + b_ref[...]


def _ada_mod(cvec, ada_w, ada_b):
    depth = ada_w.shape[0]
    r = cvec.shape[0]
    return pl.pallas_call(
        _ada_kernel,
        out_shape=jax.ShapeDtypeStruct((depth, r, N_MOD * D), F32),
        grid=(depth, N_MOD),
        in_specs=[pl.BlockSpec((r, D), lambda l, n: (0, 0)),
                  pl.BlockSpec((None, D, D), lambda l, n: (l, 0, n)),
                  pl.BlockSpec((None, 1, D), lambda l, n: (l, 0, n))],
        out_specs=pl.BlockSpec((None, r, D), lambda l, n: (l, 0, n)),
        compiler_params=_cparams(2),
        name="ada_mod",
    )(cvec, ada_w, ada_b.reshape(depth, 1, N_MOD * D))


TR = 256
NRT = T // TR


def _mod_spec(which):
    return pl.BlockSpec((None, None, None, 1, D), lambda b, j: (b, j // (L // TR), which, 0, 0))


def _rms(x, g):
    return x * lax.rsqrt(jnp.mean(x * x, axis=-1, keepdims=True) + EPS) * g


def _normmod_kernel(x_ref, g_ref, sh_ref, sc_ref, o_ref):
    y = _rms(x_ref[...], g_ref[...])
    o_ref[...] = (y * (1.0 + sc_ref[...]) + sh_ref[...]).astype(o_ref.dtype)


def _normmod(xs, g, ms):
    b = xs.shape[0]
    return pl.pallas_call(
        _normmod_kernel,
        out_shape=jax.ShapeDtypeStruct((b, T, D), BF16),
        grid=(b, NRT),
        in_specs=[pl.BlockSpec((None, TR, D), lambda b, j: (b, j, 0)),
                  pl.BlockSpec((1, D), lambda b, j: (0, 0)),
                  _mod_spec(0), _mod_spec(1)],
        out_specs=pl.BlockSpec((None, TR, D), lambda b, j: (b, j, 0)),
        compiler_params=_cparams(2),
        name="normmod",
    )(xs, g.reshape(1, D), ms, ms)


def _mm_kernel(x_ref, w_ref, o_ref):
    o_ref[...] = _dot(x_ref[...], w_ref[...]).astype(o_ref.dtype)


def _matmul(x, w, out_dtype, tn, name):
    b, t, k = x.shape
    n = w.shape[1]
    return pl.pallas_call(
        _mm_kernel,
        out_shape=jax.ShapeDtypeStruct((b, t, n), out_dtype),
        grid=(n // tn, b),
        in_specs=[pl.BlockSpec((None, t, k), lambda n_, b_: (b_, 0, 0)),
                  pl.BlockSpec((k, tn), lambda n_, b_: (0, n_))],
        out_specs=pl.BlockSpec((None, t, tn), lambda n_, b_: (b_, 0, n_)),
        compiler_params=_cparams(2),
        name=name,
    )(x, w)


CW = 512
CONV_RC = 256
PAD_LAT = 8
PAD_CTX = 16 + L


def _conv_kernel(x_ref, w_ref, b_ref, o_ref, pad_ref):
    z8 = jnp.zeros((8, CW), F32)
    pad_ref[0:8, :] = z8
    pad_ref[PAD_LAT + L:PAD_CTX, :] = z8
    pad_ref[PAD_CTX + NC:PAD_CTX + NC + 8, :] = z8
    pad_ref[PAD_LAT:PAD_LAT + L, :] = x_ref[0:L, :].astype(F32)
    pad_ref[PAD_CTX:PAD_CTX + NC, :] = x_ref[L:T, :].astype(F32)
    w = w_ref[...]
    bias = b_ref[...]
    for c in range(T // CONV_RC):
        r0 = c * CONV_RC
        base = (PAD_LAT if r0 < L else PAD_CTX - L) + r0
        win = pad_ref[base - 8:base + CONV_RC + 8, :]
        acc = bias + win[8:8 + CONV_RC] * w[2:3, :]
        for k in (0, 1, 3, 4):
            shifted = pltpu.roll(win, (2 - k) % (CONV_RC + 16), axis=0)
            acc = acc + shifted[8:8 + CONV_RC] * w[k:k + 1, :]
        o_ref[r0:r0 + CONV_RC, :] = (acc * _sigmoid(acc)).astype(o_ref.dtype)


def _conv_silu(p, conv_w, conv_b):
    b = p.shape[0]
    w8 = jnp.concatenate([conv_w, jnp.zeros((3, SSD_XBC), F32)], axis=0)
    return pl.pallas_call(
        _conv_kernel,
        out_shape=jax.ShapeDtypeStruct((b, T, SSD_XBC), BF16),
        grid=(b, SSD_XBC // CW),
        in_specs=[pl.BlockSpec((None, T, CW), lambda b_, n: (b_, 0, n)),
                  pl.BlockSpec((8, CW), lambda b_, n: (0, n)),
                  pl.BlockSpec((1, CW), lambda b_, n: (0, n))],
        out_specs=pl.BlockSpec((None, T, CW), lambda b_, n: (b_, 0, n)),
        scratch_shapes=[pltpu.VMEM((T + 24, CW), F32)],
        compiler_params=_cparams(2),
        name="conv_silu",
    )(p, w8, conv_b.reshape(1, SSD_XBC))


GW = SSD_HPG * 64


def _split3(x):
    hi = x.astype(BF16)
    r = x - hi.astype(F32)
    mid = r.astype(BF16)
    lo = (r - mid.astype(F32)).astype(BF16)
    return hi, mid, lo


def _cum_constants():
    ii = lax.broadcasted_iota(jnp.int32, (Q, Q), 0)
    jj = lax.broadcasted_iota(jnp.int32, (Q, Q), 1)
    tril = jnp.where(ii >= jj, 1.0, 0.0).astype(BF16)
    triu = jnp.where(ii <= jj, 1.0, 0.0).astype(BF16)
    return jnp.concatenate([jnp.concatenate([tril] * 3, axis=1),
                            jnp.concatenate([triu] * 3, axis=1)], axis=0)


def _ssd_prep_kernel(dt_ref, bias_ref, alog_ref, acc_o, wts_o, eacc_o, rowt_o, dtt_o):
    lcat = _cum_constants()
    isf = (lax.broadcasted_iota(jnp.int32, (1, Q), 1) % 8) < 4
    bias = bias_ref[...]
    a_r = -jnp.exp(alog_ref[...])

    def body(c, t0):
        rows = pl.ds(t0, Q)
        dt = _softplus(dt_ref[rows, :] + bias)
        both = _dot(lcat, jnp.concatenate(_split3(dt * a_r), axis=0))
        acc = jnp.where(isf, both[0:Q], both[Q:2 * Q])
        tot = jnp.where(isf, both[Q - 1:Q], both[Q:Q + 1])
        acc_o[rows, :] = acc
        wts_o[rows, :] = dt * jnp.exp(tot - acc)
        eacc_o[rows, :] = jnp.exp(acc)
        dt_t = dt.T
        rowt_o[:, rows] = acc.T - jnp.log(dt_t)
        dtt_o[:, rows] = dt_t

    _chunk_loop(NCH, 3, body)


def _ssd_kernel(x_ref, b_ref, c_ref, z_ref, acc_ref, wts_ref, eacc_ref, rowt_ref, dtt_ref, dsk_ref, ng_ref,
                o_ref, xw_s, ec_s, big_s, sin_s, cb_s, y_s):
    g = pl.program_id(1)
    ii = lax.broadcasted_iota(jnp.int32, (Q, Q), 0)
    jj = lax.broadcasted_iota(jnp.int32, (Q, Q), 1)
    low = ii > jj
    up = ii < jj
    lo = jj < 64
    ek = lax.broadcasted_iota(jnp.int32, (Q, 2 * GW), 0)
    ech = lax.broadcasted_iota(jnp.int32, (Q, 2 * GW), 1)
    esel = jnp.where(ek == g * 8 + ech // 64, 1.0, 0.0).astype(BF16)
    esel2 = jnp.concatenate([esel] * 2, axis=0)
    zero_x = jnp.zeros((Q, Q), BF16)
    rows = lambda t0: pl.ds(t0, Q)

    def stage_expand(c, t0):
        w2 = _split3(wts_ref[rows(t0), :])[:2]
        e2 = _split3(eacc_ref[rows(t0), :])[:2]
        lhs = jnp.concatenate([jnp.concatenate(w2, axis=1), jnp.concatenate(e2, axis=1)], axis=0)
        both = _dot(lhs, esel2)
        xs = x_ref[rows(t0), :].astype(F32)
        xw_s[rows(t0), :] = (jnp.concatenate([xs, xs], axis=1) * both[0:Q]).astype(BF16)
        ec_s[rows(t0), :] = both[Q:2 * Q]

    _chunk_loop(NCH, 3, stage_expand)

    def stage_upd(c, t0):
        big_s[c] = _tn(b_ref[rows(t0), :], xw_s[rows(t0), :])

    _chunk_loop(NCH, 6, stage_upd)

    s = jnp.zeros((SSD_STATE, GW), F32)
    for c in list(range(L // Q, NCH)) + list(range(L // Q)):
        sin_s[c, :, 0:GW] = s.astype(BF16)
        s = s * ec_s[c * Q + Q - 1:c * Q + Q, 0:GW] + big_s[c, :, 0:GW]
    s = jnp.zeros((SSD_STATE, GW), F32)
    for c in reversed(range(NCH)):
        sin_s[c, :, GW:2 * GW] = s.astype(BF16)
        s = s * ec_s[c * Q:c * Q + 1, GW:2 * GW] + big_s[c, :, GW:2 * GW]

    def stage_cb(c, t0):
        cc = c_ref[rows(t0), :]
        cb_s[rows(t0), :] = _nt(cc, b_ref[rows(t0), :])
        big_s[c] = _dot(cc, sin_s[c])

    _chunk_loop(NCH, 6, stage_cb)

    dsk = dsk_ref[...]
    ng = ng_ref[...]
    to_slot0 = (Q - g * 8) % Q

    def colb(v, k):
        return jnp.broadcast_to(v[:, k:k + 1], (Q, Q))

    def stage_y(c, t0):
        xb = x_ref[rows(t0), :]
        acc = pltpu.roll(acc_ref[rows(t0), :], to_slot0, axis=1)
        rowt = rowt_ref[:, rows(t0)]
        dt_t = dtt_ref[:, rows(t0)]
        ec = ec_s[rows(t0), :]
        cb = cb_s[rows(t0), :]
        yoff = big_s[c]
        ys = []
        for pr in range(2):
            ms = []
            for hh in range(2):
                r = 2 * pr + hh
                diag = jnp.log(dt_t[r:r + 1, :] + dt_t[4 + r:5 + r, :])
                seg = jnp.where(low, colb(acc, r) - rowt[r:r + 1, :],
                                jnp.where(up, colb(acc, 4 + r) - rowt[4 + r:5 + r, :], diag))
                ms.append((cb * jnp.exp(seg)).astype(BF16))
            xp = xb[:, pr * Q:(pr + 1) * Q]
            xstack = jnp.concatenate([jnp.where(lo, xp, zero_x), jnp.where(lo, zero_x, xp)], axis=0)
            y = _dot(jnp.concatenate(ms, axis=1), xstack)
            y = (y + ec[:, pr * Q:(pr + 1) * Q] * yoff[:, pr * Q:(pr + 1) * Q]
                 + ec[:, GW + pr * Q:GW + (pr + 1) * Q] * yoff[:, GW + pr * Q:GW + (pr + 1) * Q])
            ys.append(y)
        y_s[rows(t0), :] = jnp.concatenate(ys, axis=1)

    _chunk_loop(NCH, 2, stage_y)

    def stage_out(c, t0):
        y = y_s[rows(t0), :] + dsk * x_ref[rows(t0), :].astype(F32)
        zf = z_ref[rows(t0), :].astype(F32)
        u = y * (zf * _sigmoid(zf))
        o_ref[rows(t0), :] = _rms(u, ng).astype(o_ref.dtype)

    _chunk_loop(NCH, 3, stage_out)


def _slot_order(v):
    s = v.reshape(2, SSD_GROUPS, SSD_HPG).transpose(1, 0, 2).reshape(1, 2 * SSD_HEADS)
    return jnp.pad(s, ((0, 0), (0, Q - 2 * SSD_HEADS)))


def _ssd(xa, p, dtr, dt_bias, a_log, ssd_d, ssd_norm_g):
    b = xa.shape[0]
    row = pl.BlockSpec((None, T, Q), lambda b_: (b_, 0, 0))
    col = pl.BlockSpec((None, Q, T), lambda b_: (b_, 0, 0))
    one = pl.BlockSpec((1, Q), lambda b_: (0, 0))
    acc, wts, eacc, rowt, dtt = pl.pallas_call(
        _ssd_prep_kernel,
        out_shape=[jax.ShapeDtypeStruct((b, T, Q), F32)] * 3 + [jax.ShapeDtypeStruct((b, Q, T), F32)] * 2,
        grid=(b,),
        in_specs=[row, one, one],
        out_specs=[row, row, row, col, col],
        compiler_params=_cparams(1),
        name="ssd_prep",
    )(dtr, _slot_order(dt_bias), _slot_order(a_log))

    dsk = jnp.repeat(ssd_d, 64).reshape(1, SSD_INNER)
    ng = ssd_norm_g.reshape(1, SSD_INNER)
    xoff, boff, coff, zoff = 0, SSD_INNER // Q, (SSD_INNER + SSD_GROUPS * SSD_STATE) // Q, C_Z // GW
    full = pl.BlockSpec((None, T, Q), lambda b_, g: (b_, 0, 0))
    slot = pl.BlockSpec((None, 8, T), lambda b_, g: (b_, g, 0))
    return pl.pallas_call(
        _ssd_kernel,
        out_shape=jax.ShapeDtypeStruct((b, T, SSD_INNER), BF16),
        grid=(b, SSD_GROUPS),
        in_specs=[pl.BlockSpec((None, T, GW), lambda b_, g: (b_, 0, xoff + g)),
                  pl.BlockSpec((None, T, Q), lambda b_, g: (b_, 0, boff + g)),
                  pl.BlockSpec((None, T, Q), lambda b_, g: (b_, 0, coff + g)),
                  pl.BlockSpec((None, T, GW), lambda b_, g: (b_, 0, zoff + g)),
                  full, full, full, slot, slot,
                  pl.BlockSpec((1, GW), lambda b_, g: (0, g)),
                  pl.BlockSpec((1, GW), lambda b_, g: (0, g))],
        out_specs=pl.BlockSpec((None, T, GW), lambda b_, g: (b_, 0, g)),
        scratch_shapes=[pltpu.VMEM((T, 2 * GW), BF16),
                        pltpu.VMEM((T, 2 * GW), F32),
                        pltpu.VMEM((NCH, SSD_STATE, 2 * GW), F32),
                        pltpu.VMEM((NCH, SSD_STATE, 2 * GW), BF16),
                        pltpu.VMEM((T, Q), F32),
                        pltpu.VMEM((T, GW), F32)],
        compiler_params=_cparams(2),
        name="ssd",
    )(xa, xa, xa, p, acc, wts, eacc, rowt, dtt, dsk, ng)


def _softmax_t(parts, extra=None):
    mx = None
    for s in parts:
        r = jnp.max(s, axis=0, keepdims=True)
        mx = r if mx is None else jnp.maximum(mx, r)
    if extra is not None:
        mx = jnp.maximum(mx, extra)
    den = None
    es = []
    for s in parts:
        e = jnp.exp2(s - mx)
        d = jnp.sum(e, axis=0, keepdims=True)
        den = d if den is None else den + d
        es.append(e.astype(BF16))
    if extra is not None:
        den = den + jnp.exp2(extra - mx)
    return es, 1.0 / den


def _pv_t(vts, es):
    out = None
    for vt, e in zip(vts, es):
        o = _dot(vt, e)
        out = o if out is None else out + o
    return out


def _na_row_offsets():
    table = []
    for j in (0, 1, 2, ROWS // NA_QROWS - 2, ROWS // NA_QROWS - 1):
        start = int(np.clip(NA_QROWS * j - NA_KH // 2, 0, ROWS - NA_KROWS))
        per_row = []
        for i in range(NA_KROWS):
            kr = start + i
            offs = []
            for qr in range(NA_QROWS):
                r = NA_QROWS * j + qr
                rs = int(np.clip(r - NA_KH // 2, 0, ROWS - NA_KH))
                offs.append(kr - r + NA_KH - 1 if rs <= kr < rs + NA_KH else NA_NOFF)
            per_row.append(tuple(offs))
        table.append(per_row)
    return table


def _na_kernel(q_ref, k_ref, v_ref, src_ref, o_ref, vt_ref, bias_ref, sa_ref, sb_ref, pa_ref, pb_ref, ra_ref,
               rb_ref):
    lo_tile = lax.broadcasted_iota(jnp.int32, (GRID_W, Q), 1) < 64
    for ci, per_row in enumerate(_na_row_offsets()):
        for i, (off0, off1) in enumerate(per_row):
            for hh in range(2):
                bias_ref[ci, i * GRID_W:(i + 1) * GRID_W, hh * Q:(hh + 1) * Q] = jnp.where(
                    lo_tile, src_ref[hh, off0], src_ref[hh, off1])

    lane = lax.broadcasted_iota(jnp.int32, (NA_NQ, Q), 1)
    lo = lane < 64
    top = lax.broadcasted_iota(jnp.int32, (Q, NA_NQ), 0) < 64
    zero = jnp.zeros((NA_NQ, Q), BF16)

    def vt_body(c, t0):
        vt_ref[:, pl.ds(t0, Q)] = v_ref[pl.ds(t0, Q), :].astype(F32).T.astype(BF16)

    _chunk_loop(NCH, 6, vt_body)

    kc = k_ref[L:T, :]
    vtc = vt_ref[:, L:T]

    def qpair(q0):
        q = (q_ref[pl.ds(q0, NA_NQ), :].astype(F32) * QSCALE).astype(BF16)
        return jnp.concatenate([jnp.where(lo, q, zero), jnp.where(lo, zero, q)], axis=0)

    def finish(q0, ot, rden):
        ot = ot * rden
        w = jnp.where(top, ot[:, 0:NA_NQ], ot[:, NA_NQ:2 * NA_NQ])
        o_ref[pl.ds(q0, NA_NQ), :] = w.T.astype(o_ref.dtype)

    def block_params(j):
        q0 = pl.multiple_of(j * NA_NQ, NA_NQ)
        start = jnp.clip(NA_QROWS * j - NA_KH // 2, 0, ROWS - NA_KROWS)
        koff = pl.multiple_of(start * GRID_W, 2 * GRID_W)
        last = ROWS // NA_QROWS - 1
        case = (jnp.where(j >= 1, 1, 0) + jnp.where(j >= 2, 1, 0)
                + jnp.where(j >= last - 1, 1, 0) + jnp.where(j >= last, 1, 0))
        return q0, koff, case

    def qk(j, s_ref):
        q0, koff, case = block_params(j)
        qp = qpair(q0)
        s_ref[0:NA_NK, :] = _nt(k_ref[pl.ds(koff, NA_NK), :], qp) + bias_ref[case]
        s_ref[NA_NK:NA_NK + NC, :] = _nt(kc, qp)

    def soft(s_ref, p_ref, r_ref):
        s = s_ref[...]
        e = jnp.exp2(s - jnp.max(s, axis=0, keepdims=True))
        r_ref[...] = jnp.broadcast_to(1.0 / jnp.sum(e, axis=0, keepdims=True), r_ref.shape)
        p_ref[...] = e.astype(BF16)

    def pv(j, p_ref, r_ref):
        q0, koff, _ = block_params(j)
        ot = (_dot(vt_ref[:, pl.ds(koff, NA_NK)], p_ref[0:NA_NK, :])
              + _dot(vtc, p_ref[NA_NK:NA_NK + NC, :]))
        finish(q0, ot, r_ref[0:1, :])

    nblk = L // NA_NQ
    qk(0, sa_ref)
    qk(1, sb_ref)
    soft(sa_ref, pa_ref, ra_ref)

    def body(m, carry):
        pv(2 * m - 2, pa_ref, ra_ref)
        qk(2 * m, sa_ref)
        soft(sb_ref, pb_ref, rb_ref)
        pv(2 * m - 1, pb_ref, rb_ref)
        qk(2 * m + 1, sb_ref)
        soft(sa_ref, pa_ref, ra_ref)
        return carry

    lax.fori_loop(1, nblk // 2, body, 0)
    pv(nblk - 2, pa_ref, ra_ref)
    soft(sb_ref, pb_ref, rb_ref)
    pv(nblk - 1, pb_ref, rb_ref)

    ctx_blocks = []
    for cbk in range(NC // NA_NQ):
        q0 = L + cbk * NA_NQ
        ctx_blocks.append((q0, _nt(kc, qpair(q0))))
    ctx_soft = [(q0,) + _softmax_t([s]) for q0, s in ctx_blocks]
    for q0, es, rden in ctx_soft:
        finish(q0, _pv_t([vtc], es), rden)


def _na_bias_source(rpb_all):
    depth = rpb_all.shape[0]
    ck = np.arange(GRID_W)[:, None]
    cq = (np.arange(Q) % GRID_W)[None, :]
    col_oh = ((ck - cq + NA_KW - 1)[None] == np.arange(2 * NA_KW - 1)[:, None, None]).astype(np.float32)
    cs = np.clip(cq - NA_KW // 2, 0, GRID_W - NA_KW)
    col_ok = (ck >= cs) & (ck < cs + NA_KW)
    rp = rpb_all.reshape(depth, NA_HEADS // 2, 2, NA_NOFF, 2 * NA_KW - 1)
    src = jnp.einsum("lperd,dbz->lperbz", rp, col_oh, precision=HIGHEST)
    src = jnp.where(col_ok, src * LOG2E, NEG)
    return jnp.concatenate([src, jnp.full_like(src[:, :, :, :1], NEG)], axis=3)


def _na(p, bias_src):
    b = p.shape[0]
    qo, ko, vo = C_NAQ // Q, C_NAK // Q, C_NAV // Q
    return pl.pallas_call(
        _na_kernel,
        out_shape=jax.ShapeDtypeStruct((b, T, D), BF16),
        grid=(b, NA_HEADS // 2),
        in_specs=[pl.BlockSpec((None, T, Q), lambda b_, h: (b_, 0, qo + h)),
                  pl.BlockSpec((None, T, Q), lambda b_, h: (b_, 0, ko + h)),
                  pl.BlockSpec((None, T, Q), lambda b_, h: (b_, 0, vo + h)),
                  pl.BlockSpec((None, 2, NA_NOFF + 1, GRID_W, Q), lambda b_, h: (h, 0, 0, 0, 0))],
        out_specs=pl.BlockSpec((None, T, Q), lambda b_, h: (b_, 0, h)),
        scratch_shapes=[pltpu.VMEM((Q, T), BF16),
                        pltpu.VMEM((NA_CASES, NA_NK, 2 * NA_NQ), F32),
                        pltpu.VMEM((NA_NK + NC, 2 * NA_NQ), F32),
                        pltpu.VMEM((NA_NK + NC, 2 * NA_NQ), F32),
                        pltpu.VMEM((NA_NK + NC, 2 * NA_NQ), BF16),
                        pltpu.VMEM((NA_NK + NC, 2 * NA_NQ), BF16),
                        pltpu.VMEM((8, 2 * NA_NQ), F32),
                        pltpu.VMEM((8, 2 * NA_NQ), F32)],
        compiler_params=_cparams(2),
        name="na_attn",
    )(p, p, p, bias_src)


SWA_KWIN = 3 * SWA_W
SWA_QB = 128
SWA_STACK = 4 * SWA_QB
ROPE_RC = 256


def _swap16(t, lane):
    a = pltpu.roll(t, 112, axis=1)
    b = pltpu.roll(t, 16, axis=1)
    return jnp.where((lane % 32) < 16, a, b)


def _swa_kernel(sink_ref, q_ref, k_ref, v_ref, cos_ref, sin_ref, mask_ref, o_ref, krot_ref, vt_ref,
                sa_ref, sb_ref, pa_ref, pb_ref, ra_ref, rb_ref):
    kp = pl.program_id(1)
    lane = lax.broadcasted_iota(jnp.int32, (SWA_QB, Q), 1)
    lo = lane < 64
    zero = jnp.zeros((SWA_QB, Q), BF16)
    col = lax.broadcasted_iota(jnp.int32, (1, SWA_STACK), 1)
    eye4 = jnp.where(lax.broadcasted_iota(jnp.int32, (SWA_STACK, Q), 0) % SWA_QB
                     == lax.broadcasted_iota(jnp.int32, (SWA_STACK, Q), 1), 1.0, 0.0).astype(BF16)

    def rope(t, r0, n):
        lane_n = lax.broadcasted_iota(jnp.int32, (n, Q), 1)
        return t * cos_ref[pl.ds(r0, n), :] + _swap16(t, lane_n) * sin_ref[pl.ds(r0, n), :]

    def krot_body(i, carry):
        r0 = pl.multiple_of(i * ROPE_RC, ROPE_RC)
        for gl in range(2):
            t = k_ref[pl.ds(r0, ROPE_RC), gl * Q:(gl + 1) * Q].astype(F32)
            krot_ref[pl.ds(r0, ROPE_RC), gl * Q:(gl + 1) * Q] = rope(t, r0, ROPE_RC).astype(BF16)
        return carry

    lax.fori_loop(0, L // ROPE_RC, krot_body, 0, unroll=2)

    def vt_body(c, t0):
        for gl in range(2):
            vt = v_ref[pl.ds(t0, Q), gl * Q:(gl + 1) * Q].astype(F32).T
            vt_ref[gl, :, pl.ds(t0, Q)] = vt[0:64].astype(BF16)

    _chunk_loop(NCH, 6, vt_body)

    def stack_q(qa, qb):
        return jnp.concatenate([jnp.where(lo, qa, zero), jnp.where(lo, zero, qa),
                                jnp.where(lo, qb, zero), jnp.where(lo, zero, qb)], axis=0)

    def sink_row(gl):
        base = kp * 8 + gl * 4
        row = jnp.where(col < SWA_QB, sink_ref[base],
                        jnp.where(col < 2 * SWA_QB, sink_ref[base + 1],
                                  jnp.where(col < 3 * SWA_QB, sink_ref[base + 2], sink_ref[base + 3])))
        return row * LOG2E

    def finish(q0, gl, ot, rden):
        ot = ot * rden
        pa = jnp.concatenate([ot[:, 0:SWA_QB], ot[:, SWA_QB:2 * SWA_QB]], axis=0).T
        pb = jnp.concatenate([ot[:, 2 * SWA_QB:3 * SWA_QB], ot[:, 3 * SWA_QB:4 * SWA_QB]], axis=0).T
        o_ref[pl.ds(q0, SWA_QB), gl * 2 * Q:(gl + 1) * 2 * Q] = jnp.concatenate([pa, pb], axis=1).astype(o_ref.dtype)

    def block_params(i):
        q0 = pl.multiple_of(i * SWA_QB, SWA_QB)
        ws = pl.multiple_of(jnp.clip((i - 1) * SWA_W, 0, L - SWA_KWIN), SWA_W)
        case = jnp.where(i >= 1, 1, 0) + jnp.where(i >= L // SWA_QB - 1, 1, 0)
        return q0, ws, case

    def qk(i, gl, s_ref):
        q0, ws, case = block_params(i)
        qa = rope(q_ref[pl.ds(q0, SWA_QB), gl * 2 * Q:gl * 2 * Q + Q].astype(F32), q0, SWA_QB) * QSCALE
        qb = rope(q_ref[pl.ds(q0, SWA_QB), gl * 2 * Q + Q:(gl + 1) * 2 * Q].astype(F32), q0, SWA_QB) * QSCALE
        qs = stack_q(qa.astype(BF16), qb.astype(BF16))
        a_loc = jnp.concatenate([krot_ref[pl.ds(ws, SWA_KWIN), gl * Q:(gl + 1) * Q], mask_ref[case]], axis=1)
        s_ref[0:SWA_KWIN, :] = _nt(a_loc, jnp.concatenate([qs, eye4], axis=1))
        s_ref[SWA_KWIN:SWA_KWIN + NC, :] = _nt(k_ref[L:T, gl * Q:(gl + 1) * Q], qs)

    def soft(gl, s_ref, p_ref, r_ref):
        s = s_ref[...]
        sink = sink_row(gl)
        mx = jnp.maximum(jnp.max(s, axis=0, keepdims=True), sink)
        e = jnp.exp2(s - mx)
        den = jnp.sum(e, axis=0, keepdims=True) + jnp.exp2(sink - mx)
        r_ref[...] = jnp.broadcast_to(1.0 / den, r_ref.shape)
        p_ref[...] = e.astype(BF16)

    def pv(i, gl, p_ref, r_ref):
        q0, ws, _ = block_params(i)
        ot = (_dot(vt_ref[gl, :, pl.ds(ws, SWA_KWIN)], p_ref[0:SWA_KWIN, :])
              + _dot(vt_ref[gl, :, L:T], p_ref[SWA_KWIN:SWA_KWIN + NC, :]))
        finish(q0, gl, ot, r_ref[0:1, :])

    nblk = L // SWA_QB
    qk(0, 0, sa_ref)
    qk(0, 1, sb_ref)
    soft(0, sa_ref, pa_ref, ra_ref)

    def body(i, carry):
        pv(i - 1, 0, pa_ref, ra_ref)
        qk(i, 0, sa_ref)
        soft(1, sb_ref, pb_ref, rb_ref)
        pv(i - 1, 1, pb_ref, rb_ref)
        qk(i, 1, sb_ref)
        soft(0, sa_ref, pa_ref, ra_ref)
        return carry

    lax.fori_loop(1, nblk, body, 0)
    pv(nblk - 1, 0, pa_ref, ra_ref)
    soft(1, sb_ref, pb_ref, rb_ref)
    pv(nblk - 1, 1, pb_ref, rb_ref)

    for cbk in range(NC // SWA_QB):
        q0 = L + cbk * SWA_QB
        scores = []
        for gl in range(2):
            qa = q_ref[q0:q0 + SWA_QB, gl * 2 * Q:gl * 2 * Q + Q].astype(F32) * QSCALE
            qb = q_ref[q0:q0 + SWA_QB, gl * 2 * Q + Q:(gl + 1) * 2 * Q].astype(F32) * QSCALE
            scores.append(_nt(k_ref[L:T, gl * Q:(gl + 1) * Q], stack_q(qa.astype(BF16), qb.astype(BF16))))
        soft = [_softmax_t([s], sink_row(gl)) for gl, s in enumerate(scores)]
        for gl, (es, rden) in enumerate(soft):
            finish(q0, gl, _pv_t([vt_ref[gl, :, L:T]], es), rden)


def _rope_tables():
    pos = np.arange(L)
    lane = np.arange(Q) % 64
    inv = ROPE_BASE ** (-(lane % 16).astype(np.float64) / 16.0)
    p = np.where(lane[None, :] < 32, (pos // GRID_W)[:, None], (pos % GRID_W)[:, None]).astype(np.float64)
    ang = p * inv[None, :]
    sign = np.where((lane % 32) < 16, -1.0, 1.0)[None, :]
    return jnp.asarray(np.cos(ang), F32), jnp.asarray(np.sin(ang) * sign, F32)


def _swa_mask_table():
    key = np.arange(SWA_KWIN)[:, None]
    qry = np.arange(SWA_QB)[None, :]
    tabs = [np.where(np.abs(key - qry + delta) <= SWA_W, 0.0, NEG) for delta in (0, -SWA_W, -2 * SWA_W)]
    return jnp.asarray(np.stack(tabs), BF16)


def _swa(p, sink, cos_t, sin_t, mask_t):
    b = p.shape[0]
    qo, ko, vo = C_SWAQ // 512, C_SWAK // 256, C_SWAV // 256
    return pl.pallas_call(
        _swa_kernel,
        out_shape=jax.ShapeDtypeStruct((b, T, D), BF16),
        grid=(b, SWA_KV // 2),
        in_specs=[pl.BlockSpec(memory_space=pltpu.SMEM),
                  pl.BlockSpec((None, T, 512), lambda b_, h: (b_, 0, qo + h)),
                  pl.BlockSpec((None, T, 256), lambda b_, h: (b_, 0, ko + h)),
                  pl.BlockSpec((None, T, 256), lambda b_, h: (b_, 0, vo + h)),
                  pl.BlockSpec((L, Q), lambda b_, h: (0, 0)),
                  pl.BlockSpec((L, Q), lambda b_, h: (0, 0)),
                  pl.BlockSpec((3, SWA_KWIN, SWA_QB), lambda b_, h: (0, 0, 0))],
        out_specs=pl.BlockSpec((None, T, 512), lambda b_, h: (b_, 0, h)),
        scratch_shapes=[pltpu.VMEM((L, 2 * Q), BF16),
                        pltpu.VMEM((2, 64, T), BF16),
                        pltpu.VMEM((SWA_KWIN + NC, SWA_STACK), F32),
                        pltpu.VMEM((SWA_KWIN + NC, SWA_STACK), F32),
                        pltpu.VMEM((SWA_KWIN + NC, SWA_STACK), BF16),
                        pltpu.VMEM((SWA_KWIN + NC, SWA_STACK), BF16),
                        pltpu.VMEM((8, SWA_STACK), F32),
                        pltpu.VMEM((8, SWA_STACK), F32)],
        compiler_params=_cparams(2),
        name="swa_attn",
    )(sink, p, p, p, cos_t, sin_t, mask_t)


def _merge_kernel(x_ref, ya_ref, yb_ref, yc_ref, g0_ref, g1_ref, g2_ref, mg_ref,
                  wa_ref, wb_ref, wc_ref, wo_ref, o_ref):
    m = (_sigmoid(g0_ref[...].astype(F32)) * _dot(ya_ref[...], wa_ref[...])
         + _sigmoid(g1_ref[...].astype(F32)) * _dot(yb_ref[...], wb_ref[...])
         + _sigmoid(g2_ref[...].astype(F32)) * _dot(yc_ref[...], wc_ref[...]))
    o_ref[...] = x_ref[...] + mg_ref[...] * _dot(m.astype(BF16), wo_ref[...])


def _const_spec(shape):
    return pl.BlockSpec(shape, lambda b, j: (0,) * len(shape))


def _merge(xs, ya, yb, yc, p, ms, wa, wb, wc, wo):
    b = xs.shape[0]
    go = C_GATE // D
    row = lambda w: pl.BlockSpec((None, TR, w), lambda b_, j: (b_, j, 0))
    gate = lambda k: pl.BlockSpec((None, TR, D), lambda b_, j: (b_, j, go + k))
    return pl.pallas_call(
        _merge_kernel,
        out_shape=jax.ShapeDtypeStruct((b, T, D), F32),
        grid=(b, NRT),
        in_specs=[row(D), row(SSD_INNER), row(D), row(D), gate(0), gate(1), gate(2), _mod_spec(2),
                  _const_spec((SSD_INNER, D)), _const_spec((D, D)), _const_spec((D, D)), _const_spec((D, D))],
        out_specs=row(D),
        compiler_params=_cparams(2),
        name="merge",
    )(xs, ya, yb, yc, p, p, p, ms, wa, wb, wc, wo)


FF_CH = 1024


def _ffn_kernel(*refs, emit_next):
    if emit_next:
        x_ref, g_ref, sh_ref, sc_ref, mg_ref, w1_ref, w2_ref, gn_ref, shn_ref, scn_ref, o_ref, h_ref = refs
    else:
        x_ref, g_ref, sh_ref, sc_ref, mg_ref, w1_ref, w2_ref, o_ref = refs
    x = x_ref[...]
    h = (_rms(x, g_ref[...]) * (1.0 + sc_ref[...]) + sh_ref[...]).astype(BF16)
    acc = None
    for kf in range(D_FF // FF_CH):
        a = jnp.maximum(_dot(h, w1_ref[:, kf * FF_CH:(kf + 1) * FF_CH]), 0.0)
        o = _dot((a * a).astype(BF16), w2_ref[kf * FF_CH:(kf + 1) * FF_CH, :])
        acc = o if acc is None else acc + o
    y = x + mg_ref[...] * acc
    o_ref[...] = y
    if emit_next:
        h_ref[...] = (_rms(y, gn_ref[...]) * (1.0 + scn_ref[...]) + shn_ref[...]).astype(h_ref.dtype)


def _ffn(xs, g, ms, w1, w2, g_next=None, ms_next=None):
    b = xs.shape[0]
    emit_next = g_next is not None
    row = pl.BlockSpec((None, TR, D), lambda b_, j: (b_, j, 0))
    in_specs = [row, _const_spec((1, D)), _mod_spec(3), _mod_spec(4), _mod_spec(5),
                _const_spec((D, D_FF)), _const_spec((D_FF, D))]
    args = [xs, g.reshape(1, D), ms, ms, ms, w1, w2]
    out_shape = jax.ShapeDtypeStruct((b, T, D), F32)
    out_specs = row
    if emit_next:
        in_specs += [_const_spec((1, D)), _mod_spec(0), _mod_spec(1)]
        args += [g_next.reshape(1, D), ms_next, ms_next]
        out_shape = [out_shape, jax.ShapeDtypeStruct((b, T, D), BF16)]
        out_specs = [row, row]
    return pl.pallas_call(
        functools.partial(_ffn_kernel, emit_next=emit_next),
        out_shape=out_shape,
        grid=(b, NRT),
        in_specs=in_specs,
        out_specs=out_specs,
        compiler_params=_cparams(2),
        name="ffn",
    )(*args)


def _final_kernel(x_ref, g_ref, o_ref):
    o_ref[...] = _rms(x_ref[...], g_ref[...])


def _final_norm(xs, g):
    b = xs.shape[0]
    return pl.pallas_call(
        _final_kernel,
        out_shape=jax.ShapeDtypeStruct((b, L, D), F32),
        grid=(b, L // TR),
        in_specs=[pl.BlockSpec((None, TR, D), lambda b_, j: (b_, j, 0)), _const_spec((1, D))],
        out_specs=pl.BlockSpec((None, TR, D), lambda b_, j: (b_, j, 0)),
        compiler_params=_cparams(2),
        name="final_norm",
    )(xs, g.reshape(1, D))


def _prep_w_in(w):
    def dup(a):
        a = a.reshape(D, SWA_KV, 1, 64)
        return jnp.concatenate([a, a], axis=2).reshape(D, SWA_KV * 128)
    cat = jnp.concatenate([
        w[:, R_XBC:R_DT], w[:, R_NAK:R_NAV], w[:, R_NAV:R_SWAK], dup(w[:, R_SWAK:R_SWAV]),
        dup(w[:, R_SWAV:R_Z]), w[:, R_Z:R_NAQ], w[:, R_NAQ:R_SWAQ], w[:, R_SWAQ:R_GATE],
        w[:, R_GATE:R_END]], axis=1).astype(BF16)
    wd = w[:, R_DT:R_NAK].reshape(D, 2, SSD_GROUPS, SSD_HPG).transpose(0, 2, 1, 3).reshape(D, 2 * SSD_HEADS)
    wdt = jnp.concatenate([wd, jnp.zeros((D, Q - 2 * SSD_HEADS), F32)], axis=1).astype(BF16)
    return cat, wdt


def kernel(x, c, ctx, c_ctx, ada_w, ada_b, norm1_g, norm2_g, w_in, conv_w, conv_b, dt_bias, a_log, ssd_d,
           ssd_norm_g, na_rpb, swa_sink, w_o_ssd, w_o_na, w_o_swa, w_out, w_ff1, w_ff2, final_g):
    b = x.shape[0]
    depth = ada_w.shape[0]
    xs = jnp.concatenate([x, ctx], axis=1)
    nrow = -(-(b + 1) // 8) * 8
    cvec = jnp.concatenate([c, c_ctx[None, :], jnp.zeros((nrow - b - 1, D), F32)], axis=0)
    mod = _ada_mod(cvec, ada_w, ada_b)
    lat = mod[:, :b].reshape(depth, b, 1, N_MOD, 1, D)
    cx = jnp.broadcast_to(mod[:, b].reshape(depth, 1, 1, N_MOD, 1, D), lat.shape)
    ms_all = jnp.concatenate([lat, cx], axis=2)
    cos_t, sin_t = _rope_tables()
    mask_t = _swa_mask_table()
    na_bias = _na_bias_source(na_rpb)

    h = _normmod(xs, norm1_g[0], ms_all[0])
    for l in range(depth):
        ms = ms_all[l]
        wcat, wdt = _prep_w_in(w_in[l])
        p = _matmul(h, wcat, BF16, 1024, "in_proj")
        dtr = _matmul(h, wdt, F32, Q, "dt_proj")
        xa = _conv_silu(p, conv_w[l], conv_b[l])
        ya = _ssd(xa, p, dtr, dt_bias[l], a_log[l], ssd_d[l], ssd_norm_g[l])
        yb = _na(p, na_bias[l])
        yc = _swa(p, swa_sink[l], cos_t, sin_t, mask_t)
        xs = _merge(xs, ya, yb, yc, p, ms, w_o_ssd[l].astype(BF16), w_o_na[l].astype(BF16),
                    w_o_swa[l].astype(BF16), w_out[l].astype(BF16))
        w1, w2 = w_ff1[l].astype(BF16), w_ff2[l].astype(BF16)
        if l + 1 < depth:
            xs, h = _ffn(xs, norm2_g[l], ms, w1, w2, norm1_g[l + 1], ms_all[l + 1])
        else:
            xs = _ffn(xs, norm2_g[l], ms, w1, w2)
    return _final_norm(xs, final_g)
```

```python
import functools

import numpy as np
import jax
import jax.numpy as jnp
from jax import lax
from jax.experimental import pallas as pl
from jax.experimental.pallas import tpu as pltpu

F32 = jnp.float32
BF16 = jnp.bfloat16
HIGHEST = lax.Precision.HIGHEST

D = 1024
L = 2048
NC = 256
T = L + NC
GRID_W = 64
ROWS = L // GRID_W
EPS = 1e-6
N_MOD = 6

SSD_INNER = 2 * D
SSD_HEADS = 32
SSD_GROUPS = 8
SSD_HPG = 4
SSD_STATE = 128
SSD_XBC = SSD_INNER + 2 * SSD_GROUPS * SSD_STATE
Q = 128
NCH = T // Q

NA_HEADS = 16
NA_KH = 8
NA_KW = 16
NA_QROWS = 2
NA_KROWS = 10
NA_NQ = NA_QROWS * GRID_W
NA_NK = NA_KROWS * GRID_W
NA_CASES = 5
NA_NOFF = 2 * NA_KH - 1

SWA_HEADS = 16
SWA_KV = 4
SWA_W = 128
ROPE_BASE = 10000.0
D_FF = 4 * D

R_XBC, R_DT, R_NAK, R_NAV, R_SWAK, R_SWAV, R_Z, R_NAQ, R_SWAQ, R_GATE, R_END = (
    0, 4096, 4160, 5184, 6208, 6464, 6720, 8768, 9792, 10816, 13888)
C_XBC, C_NAK, C_NAV, C_SWAK, C_SWAV, C_Z, C_NAQ, C_SWAQ, C_GATE, NCAT = (
    0, 4096, 5120, 6144, 6656, 7168, 9216, 10240, 11264, 14336)

NEG = -1e30
LOG2E = float(np.log2(np.e))
QSCALE = 0.125 * LOG2E
VMEM_LIMIT = 48 * 1024 * 1024


def _cparams(n_axes):
    return pltpu.CompilerParams(dimension_semantics=("parallel",) * n_axes,
                                vmem_limit_bytes=VMEM_LIMIT)


def _sigmoid(x):
    return 1.0 / (1.0 + jnp.exp(-x))


def _softplus(x):
    return jnp.maximum(x, 0.0) + jnp.log1p(jnp.exp(-jnp.abs(x)))


def _nt(a, b):
    return lax.dot_general(a, b, (((1,), (1,)), ((), ())), preferred_element_type=F32)


def _tn(a, b):
    return lax.dot_general(a, b, (((0,), (0,)), ((), ())), preferred_element_type=F32)


def _dot(a, b):
    return jnp.dot(a, b, preferred_element_type=F32)


def _chunk_loop(n, unroll, body):
    def wrapped(c, carry):
        body(c, pl.multiple_of(c * Q, Q))
        return carry
    lax.fori_loop(0, n, wrapped, 0, unroll=unroll)


def _ada_kernel(c_ref, w_ref, b_ref, o_ref):
    c = c_ref[...]
    s = (c * _sigmoid(c)).astype(BF16)
    o_ref[...] = _dot(s, w_ref[...].astype(BF16)) + b_ref[...]


def _ada_mod(cvec, ada_w, ada_b):
    depth = ada_w.shape[0]
    r = cvec.shape[0]
    return pl.pallas_call(
        _ada_kernel,
        out_shape=jax.ShapeDtypeStruct((depth, r, N_MOD * D), F32),
        grid=(depth, N_MOD),
        in_specs=[pl.BlockSpec((r, D), lambda l, n: (0, 0)),
                  pl.BlockSpec((None, D, D), lambda l, n: (l, 0, n)),
                  pl.BlockSpec((None, 1, D), lambda l, n: (l, 0, n))],
        out_specs=pl.BlockSpec((None, r, D), lambda l, n: (l, 0, n)),
        compiler_params=_cparams(2),
        name="ada_mod",
    )(cvec, ada_w, ada_b.reshape(depth, 1, N_MOD * D))


TR = 256
NRT = T // TR


def _mod_spec(which):
    return pl.BlockSpec((None, None, None, 1, D), lambda b, j: (b, j // (L // TR), which, 0, 0))


def _rms(x, g):
    return x * lax.rsqrt(jnp.mean(x * x, axis=-1, keepdims=True) + EPS) * g


def _normmod_kernel(x_ref, g_ref, sh_ref, sc_ref, o_ref):
    y = _rms(x_ref[...], g_ref[...])
    o_ref[...] = (y * (1.0 + sc_ref[...]) + sh_ref[...]).astype(o_ref.dtype)


def _normmod(xs, g, ms):
    b = xs.shape[0]
    return pl.pallas_call(
        _normmod_kernel,
        out_shape=jax.ShapeDtypeStruct((b, T, D), BF16),
        grid=(b, NRT),
        in_specs=[pl.BlockSpec((None, TR, D), lambda b, j: (b, j, 0)),
                  pl.BlockSpec((1, D), lambda b, j: (0, 0)),
                  _mod_spec(0), _mod_spec(1)],
        out_specs=pl.BlockSpec((None, TR, D), lambda b, j: (b, j, 0)),
        compiler_params=_cparams(2),
        name="normmod",
    )(xs, g.reshape(1, D), ms, ms)


def _mm_kernel(x_ref, w_ref, o_ref):
    o_ref[...] = _dot(x_ref[...], w_ref[...]).astype(o_ref.dtype)


def _matmul(x, w, out_dtype, tn, name):
    b, t, k = x.shape
    n = w.shape[1]
    return pl.pallas_call(
        _mm_kernel,
        out_shape=jax.ShapeDtypeStruct((b, t, n), out_dtype),
        grid=(n // tn, b),
        in_specs=[pl.BlockSpec((None, t, k), lambda n_, b_: (b_, 0, 0)),
                  pl.BlockSpec((k, tn), lambda n_, b_: (0, n_))],
        out_specs=pl.BlockSpec((None, t, tn), lambda n_, b_: (b_, 0, n_)),
        compiler_params=_cparams(2),
        name=name,
    )(x, w)


CW = 512
CONV_RC = 256
PAD_LAT = 8
PAD_CTX = 16 + L


def _conv_kernel(x_ref, w_ref, b_ref, o_ref, pad_ref):
    z8 = jnp.zeros((8, CW), F32)
    pad_ref[0:8, :] = z8
    pad_ref[PAD_LAT + L:PAD_CTX, :] = z8
    pad_ref[PAD_CTX + NC:PAD_CTX + NC + 8, :] = z8
    pad_ref[PAD_LAT:PAD_LAT + L, :] = x_ref[0:L, :].astype(F32)
    pad_ref[PAD_CTX:PAD_CTX + NC, :] = x_ref[L:T, :].astype(F32)
    w = w_ref[...]
    bias = b_ref[...]
    for c in range(T // CONV_RC):
        r0 = c * CONV_RC
        base = (PAD_LAT if r0 < L else PAD_CTX - L) + r0
        win = pad_ref[base - 8:base + CONV_RC + 8, :]
        acc = bias + win[8:8 + CONV_RC] * w[2:3, :]
        for k in (0, 1, 3, 4):
            shifted = pltpu.roll(win, (2 - k) % (CONV_RC + 16), axis=0)
            acc = acc + shifted[8:8 + CONV_RC] * w[k:k + 1, :]
        o_ref[r0:r0 + CONV_RC, :] = (acc * _sigmoid(acc)).astype(o_ref.dtype)


def _conv_silu(p, conv_w, conv_b):
    b = p.shape[0]
    w8 = jnp.concatenate([conv_w, jnp.zeros((3, SSD_XBC), F32)], axis=0)
    return pl.pallas_call(
        _conv_kernel,
        out_shape=jax.ShapeDtypeStruct((b, T, SSD_XBC), BF16),
        grid=(b, SSD_XBC // CW),
        in_specs=[pl.BlockSpec((None, T, CW), lambda b_, n: (b_, 0, n)),
                  pl.BlockSpec((8, CW), lambda b_, n: (0, n)),
                  pl.BlockSpec((1, CW), lambda b_, n: (0, n))],
        out_specs=pl.BlockSpec((None, T, CW), lambda b_, n: (b_, 0, n)),
        scratch_shapes=[pltpu.VMEM((T + 24, CW), F32)],
        compiler_params=_cparams(2),
        name="conv_silu",
    )(p, w8, conv_b.reshape(1, SSD_XBC))


GW = SSD_HPG * 64


def _split3(x):
    hi = x.astype(BF16)
    r = x - hi.astype(F32)
    mid = r.astype(BF16)
    lo = (r - mid.astype(F32)).astype(BF16)
    return hi, mid, lo


def _cum_constants():
    ii = lax.broadcasted_iota(jnp.int32, (Q, Q), 0)
    jj = lax.broadcasted_iota(jnp.int32, (Q, Q), 1)
    tril = jnp.where(ii >= jj, 1.0, 0.0).astype(BF16)
    triu = jnp.where(ii <= jj, 1.0, 0.0).astype(BF16)
    return jnp.concatenate([jnp.concatenate([tril] * 3, axis=1),
                            jnp.concatenate([triu] * 3, axis=1)], axis=0)


def _ssd_prep_kernel(dt_ref, bias_ref, alog_ref, accg_o, wts_o, eacc_o, rowt_o, dtt_o):
    lcat = _cum_constants()
    isf = (lax.broadcasted_iota(jnp.int32, (1, Q), 1) % 8) < 4
    bias = bias_ref[...]
    a_r = -jnp.exp(alog_ref[...])

    def body(c, t0):
        rows = pl.ds(t0, Q)
        dt = _softplus(dt_ref[rows, :] + bias)
        both = _dot(lcat, jnp.concatenate(_split3(dt * a_r), axis=0))
        acc = jnp.where(isf, both[0:Q], both[Q:2 * Q])
        tot = jnp.where(isf, both[Q - 1:Q], both[Q:Q + 1])
        for g in range(SSD_GROUPS):
            accg_o[g, rows, :] = acc if g == 0 else pltpu.roll(acc, Q - 8 * g, axis=1)
        wts_o[rows, :] = dt * jnp.exp(tot - acc)
        eacc_o[rows, :] = jnp.exp(acc)
        dt_t = dt.T
        rowt_o[:, rows] = acc.T - jnp.log(dt_t)
        dtt_o[:, rows] = dt_t

    _chunk_loop(NCH, 3, body)


def _ssd_kernel(x_ref, b_ref, c_ref, z_ref, acc_ref, wts_ref, eacc_ref, rowt_ref, dtt_ref, dsk_ref, ng_ref,
                o_ref, xw_s, ec_s, big_s, sin_s, cb_s, y_s):
    g = pl.program_id(1)
    ii = lax.broadcasted_iota(jnp.int32, (Q, Q), 0)
    jj = lax.broadcasted_iota(jnp.int32, (Q, Q), 1)
    low = ii > jj
    up = ii < jj
    lo = jj < 64
    ek = lax.broadcasted_iota(jnp.int32, (Q, 2 * GW), 0)
    ech = lax.broadcasted_iota(jnp.int32, (Q, 2 * GW), 1)
    esel = jnp.where(ek == g * 8 + ech // 64, 1.0, 0.0).astype(BF16)
    esel2 = jnp.concatenate([esel] * 2, axis=0)
    zero_x = jnp.zeros((Q, Q), BF16)
    rows = lambda t0: pl.ds(t0, Q)

    def stage_expand(c, t0):
        w2 = _split3(wts_ref[rows(t0), :])[:2]
        e2 = _split3(eacc_ref[rows(t0), :])[:2]
        lhs = jnp.concatenate([jnp.concatenate(w2, axis=1), jnp.concatenate(e2, axis=1)], axis=0)
        both = _dot(lhs, esel2)
        xs = x_ref[rows(t0), :].astype(F32)
        xw_s[rows(t0), :] = (jnp.concatenate([xs, xs], axis=1) * both[0:Q]).astype(BF16)
        ec_s[rows(t0), :] = both[Q:2 * Q]

    _chunk_loop(NCH, 3, stage_expand)

    def stage_upd(c, t0):
        big_s[c] = _tn(b_ref[rows(t0), :], xw_s[rows(t0), :])

    _chunk_loop(NCH, 6, stage_upd)

    s = jnp.zeros((SSD_STATE, GW), F32)
    for c in list(range(L // Q, NCH)) + list(range(L // Q)):
        sin_s[c, :, 0:GW] = s.astype(BF16)
        s = s * ec_s[c * Q + Q - 1:c * Q + Q, 0:GW] + big_s[c, :, 0:GW]
    s = jnp.zeros((SSD_STATE, GW), F32)
    for c in reversed(range(NCH)):
        sin_s[c, :, GW:2 * GW] = s.astype(BF16)
        s = s * ec_s[c * Q:c * Q + 1, GW:2 * GW] + big_s[c, :, GW:2 * GW]

    def stage_cb(c, t0):
        cc = c_ref[rows(t0), :]
        cb_s[rows(t0), :] = _nt(cc, b_ref[rows(t0), :])
        big_s[c] = _dot(cc, sin_s[c])

    _chunk_loop(NCH, 6, stage_cb)

    dsk = dsk_ref[...]
    ng = ng_ref[...]

    def colb(v, k):
        return jnp.broadcast_to(v[:, k:k + 1], (Q, Q))

    def stage_y(c, t0):
        xb = x_ref[rows(t0), :]
        acc = acc_ref[rows(t0), :]
        rowt = rowt_ref[:, rows(t0)]
        dt_t = dtt_ref[:, rows(t0)]
        ec = ec_s[rows(t0), :]
        cb = cb_s[rows(t0), :]
        yoff = big_s[c]
        ys = []
        for pr in range(2):
            ms = []
            for hh in range(2):
                r = 2 * pr + hh
                diag = jnp.log(dt_t[r:r + 1, :] + dt_t[4 + r:5 + r, :])
                seg = jnp.where(low, colb(acc, r) - rowt[r:r + 1, :],
                                jnp.where(up, colb(acc, 4 + r) - rowt[4 + r:5 + r, :], diag))
                ms.append((cb * jnp.exp(seg)).astype(BF16))
            xp = xb[:, pr * Q:(pr + 1) * Q]
            xstack = jnp.concatenate([jnp.where(lo, xp, zero_x), jnp.where(lo, zero_x, xp)], axis=0)
            y = _dot(jnp.concatenate(ms, axis=1), xstack)
            y = (y + ec[:, pr * Q:(pr + 1) * Q] * yoff[:, pr * Q:(pr + 1) * Q]
                 + ec[:, GW + pr * Q:GW + (pr + 1) * Q] * yoff[:, GW + pr * Q:GW + (pr + 1) * Q])
            ys.append(y)
        y_s[rows(t0), :] = jnp.concatenate(ys, axis=1)

    _chunk_loop(NCH, 2, stage_y)

    def stage_out(c, t0):
        y = y_s[rows(t0), :] + dsk * x_ref[rows(t0), :].astype(F32)
        zf = z_ref[rows(t0), :].astype(F32)
        u = y * (zf * _sigmoid(zf))
        o_ref[rows(t0), :] = _rms(u, ng).astype(o_ref.dtype)

    _chunk_loop(NCH, 3, stage_out)


def _slot_order(v):
    s = v.reshape(2, SSD_GROUPS, SSD_HPG).transpose(1, 0, 2).reshape(1, 2 * SSD_HEADS)
    return jnp.pad(s, ((0, 0), (0, Q - 2 * SSD_HEADS)))


def _ssd(xa, p, dtr, dt_bias, a_log, ssd_d, ssd_norm_g):
    b = xa.shape[0]
    row = pl.BlockSpec((None, T, Q), lambda b_: (b_, 0, 0))
    col = pl.BlockSpec((None, Q, T), lambda b_: (b_, 0, 0))
    one = pl.BlockSpec((1, Q), lambda b_: (0, 0))
    accg, wts, eacc, rowt, dtt = pl.pallas_call(
        _ssd_prep_kernel,
        out_shape=([jax.ShapeDtypeStruct((b, SSD_GROUPS, T, Q), F32)] + [jax.ShapeDtypeStruct((b, T, Q), F32)] * 2
                   + [jax.ShapeDtypeStruct((b, Q, T), F32)] * 2),
        grid=(b,),
        in_specs=[row, one, one],
        out_specs=[pl.BlockSpec((None, SSD_GROUPS, T, Q), lambda b_: (b_, 0, 0, 0)), row, row, col, col],
        compiler_params=_cparams(1),
        name="ssd_prep",
    )(dtr, _slot_order(dt_bias), _slot_order(a_log))

    dsk = jnp.repeat(ssd_d, 64).reshape(1, SSD_INNER)
    ng = ssd_norm_g.reshape(1, SSD_INNER)
    xoff, boff, coff, zoff = 0, SSD_INNER // Q, (SSD_INNER + SSD_GROUPS * SSD_STATE) // Q, C_Z // GW
    full = pl.BlockSpec((None, T, Q), lambda b_, g: (b_, 0, 0))
    slot = pl.BlockSpec((None, 8, T), lambda b_, g: (b_, g, 0))
    return pl.pallas_call(
        _ssd_kernel,
        out_shape=jax.ShapeDtypeStruct((b, T, SSD_INNER), BF16),
        grid=(b, SSD_GROUPS),
        in_specs=[pl.BlockSpec((None, T, GW), lambda b_, g: (b_, 0, xoff + g)),
                  pl.BlockSpec((None, T, Q), lambda b_, g: (b_, 0, boff + g)),
                  pl.BlockSpec((None, T, Q), lambda b_, g: (b_, 0, coff + g)),
                  pl.BlockSpec((None, T, GW), lambda b_, g: (b_, 0, zoff + g)),
                  pl.BlockSpec((None, None, T, Q), lambda b_, g: (b_, g, 0, 0)), full, full, slot, slot,
                  pl.BlockSpec((1, GW), lambda b_, g: (0, g)),
                  pl.BlockSpec((1, GW), lambda b_, g: (0, g))],
        out_specs=pl.BlockSpec((None, T, GW), lambda b_, g: (b_, 0, g)),
        scratch_shapes=[pltpu.VMEM((T, 2 * GW), BF16),
                        pltpu.VMEM((T, 2 * GW), F32),
                        pltpu.VMEM((NCH, SSD_STATE, 2 * GW), F32),
                        pltpu.VMEM((NCH, SSD_STATE, 2 * GW), BF16),
                        pltpu.VMEM((T, Q), F32),
                        pltpu.VMEM((T, GW), F32)],
        compiler_params=_cparams(2),
        name="ssd",
    )(xa, xa, xa, p, accg, wts, eacc, rowt, dtt, dsk, ng)


def _softmax_t(parts, extra=None):
    mx = None
    for s in parts:
        r = jnp.max(s, axis=0, keepdims=True)
        mx = r if mx is None else jnp.maximum(mx, r)
    if extra is not None:
        mx = jnp.maximum(mx, extra)
    den = None
    es = []
    for s in parts:
        e = jnp.exp2(s - mx)
        d = jnp.sum(e, axis=0, keepdims=True)
        den = d if den is None else den + d
        es.append(e.astype(BF16))
    if extra is not None:
        den = den + jnp.exp2(extra - mx)
    return es, 1.0 / den


def _pv_t(vts, es):
    out = None
    for vt, e in zip(vts, es):
        o = _dot(vt, e)
        out = o if out is None else out + o
    return out


def _na_row_offsets():
    table = []
    for j in (0, 1, 2, ROWS // NA_QROWS - 2, ROWS // NA_QROWS - 1):
        start = int(np.clip(NA_QROWS * j - NA_KH // 2, 0, ROWS - NA_KROWS))
        per_row = []
        for i in range(NA_KROWS):
            kr = start + i
            offs = []
            for qr in range(NA_QROWS):
                r = NA_QROWS * j + qr
                rs = int(np.clip(r - NA_KH // 2, 0, ROWS - NA_KH))
                offs.append(kr - r + NA_KH - 1 if rs <= kr < rs + NA_KH else NA_NOFF)
            per_row.append(tuple(offs))
        table.append(per_row)
    return table


def _na_kernel(q_ref, k_ref, v_ref, src_ref, o_ref, vt_ref, bias_ref, sa_ref, sb_ref, pa_ref, pb_ref, ra_ref,
               rb_ref):
    lo_tile = lax.broadcasted_iota(jnp.int32, (GRID_W, Q), 1) < 64
    for ci, per_row in enumerate(_na_row_offsets()):
        for i, (off0, off1) in enumerate(per_row):
            for hh in range(2):
                bias_ref[ci, i * GRID_W:(i + 1) * GRID_W, hh * Q:(hh + 1) * Q] = jnp.where(
                    lo_tile, src_ref[hh, off0], src_ref[hh, off1])

    lane = lax.broadcasted_iota(jnp.int32, (NA_NQ, Q), 1)
    lo = lane < 64
    top = lax.broadcasted_iota(jnp.int32, (Q, NA_NQ), 0) < 64
    zero = jnp.zeros((NA_NQ, Q), BF16)

    def vt_body(c, t0):
        vt_ref[:, pl.ds(t0, Q)] = v_ref[pl.ds(t0, Q), :].astype(F32).T.astype(BF16)

    _chunk_loop(NCH, 6, vt_body)

    kc = k_ref[L:T, :]
    vtc = vt_ref[:, L:T]

    def qpair(q0):
        q = (q_ref[pl.ds(q0, NA_NQ), :].astype(F32) * QSCALE).astype(BF16)
        return jnp.concatenate([jnp.where(lo, q, zero), jnp.where(lo, zero, q)], axis=0)

    def finish(q0, ot, rden):
        ot = ot * rden
        w = jnp.where(top, ot[:, 0:NA_NQ], ot[:, NA_NQ:2 * NA_NQ])
        o_ref[pl.ds(q0, NA_NQ), :] = w.T.astype(o_ref.dtype)

    def block_params(j):
        q0 = pl.multiple_of(j * NA_NQ, NA_NQ)
        start = jnp.clip(NA_QROWS * j - NA_KH // 2, 0, ROWS - NA_KROWS)
        koff = pl.multiple_of(start * GRID_W, 2 * GRID_W)
        last = ROWS // NA_QROWS - 1
        case = (jnp.where(j >= 1, 1, 0) + jnp.where(j >= 2, 1, 0)
                + jnp.where(j >= last - 1, 1, 0) + jnp.where(j >= last, 1, 0))
        return q0, koff, case

    def qk(j, s_ref):
        q0, koff, case = block_params(j)
        qp = qpair(q0)
        s_ref[0:NA_NK, :] = _nt(k_ref[pl.ds(koff, NA_NK), :], qp) + bias_ref[case]
        s_ref[NA_NK:NA_NK + NC, :] = _nt(kc, qp)

    def soft(s_ref, p_ref, r_ref):
        s = s_ref[...]
        e = jnp.exp2(s - jnp.max(s, axis=0, keepdims=True))
        r_ref[...] = jnp.broadcast_to(1.0 / jnp.sum(e, axis=0, keepdims=True), r_ref.shape)
        p_ref[...] = e.astype(BF16)

    def pv(j, p_ref, r_ref):
        q0, koff, _ = block_params(j)
        ot = (_dot(vt_ref[:, pl.ds(koff, NA_NK)], p_ref[0:NA_NK, :])
              + _dot(vtc, p_ref[NA_NK:NA_NK + NC, :]))
        finish(q0, ot, r_ref[0:1, :])

    nblk = L // NA_NQ
    qk(0, sa_ref)
    qk(1, sb_ref)
    soft(sa_ref, pa_ref, ra_ref)

    def body(m, carry):
        pv(2 * m - 2, pa_ref, ra_ref)
        qk(2 * m, sa_ref)
        soft(sb_ref, pb_ref, rb_ref)
        pv(2 * m - 1, pb_ref, rb_ref)
        qk(2 * m + 1, sb_ref)
        soft(sa_ref, pa_ref, ra_ref)
        return carry

    lax.fori_loop(1, nblk // 2, body, 0)
    pv(nblk - 2, pa_ref, ra_ref)
    soft(sb_ref, pb_ref, rb_ref)
    pv(nblk - 1, pb_ref, rb_ref)

    ctx_blocks = []
    for cbk in range(NC // NA_NQ):
        q0 = L + cbk * NA_NQ
        ctx_blocks.append((q0, _nt(kc, qpair(q0))))
    ctx_soft = [(q0,) + _softmax_t([s]) for q0, s in ctx_blocks]
    for q0, es, rden in ctx_soft:
        finish(q0, _pv_t([vtc], es), rden)


def _na_bias_source(rpb_all):
    depth = rpb_all.shape[0]
    ck = np.arange(GRID_W)[:, None]
    cq = (np.arange(Q) % GRID_W)[None, :]
    col_oh = ((ck - cq + NA_KW - 1)[None] == np.arange(2 * NA_KW - 1)[:, None, None]).astype(np.float32)
    cs = np.clip(cq - NA_KW // 2, 0, GRID_W - NA_KW)
    col_ok = (ck >= cs) & (ck < cs + NA_KW)
    rp = rpb_all.reshape(depth, NA_HEADS // 2, 2, NA_NOFF, 2 * NA_KW - 1)
    src = jnp.einsum("lperd,dbz->lperbz", rp, col_oh, precision=HIGHEST)
    src = jnp.where(col_ok, src * LOG2E, NEG)
    return jnp.concatenate([src, jnp.full_like(src[:, :, :, :1], NEG)], axis=3)


def _na(p, bias_src):
    b = p.shape[0]
    qo, ko, vo = C_NAQ // Q, C_NAK // Q, C_NAV // Q
    return pl.pallas_call(
        _na_kernel,
        out_shape=jax.ShapeDtypeStruct((b, T, D), BF16),
        grid=(b, NA_HEADS // 2),
        in_specs=[pl.BlockSpec((None, T, Q), lambda b_, h: (b_, 0, qo + h)),
                  pl.BlockSpec((None, T, Q), lambda b_, h: (b_, 0, ko + h)),
                  pl.BlockSpec((None, T, Q), lambda b_, h: (b_, 0, vo + h)),
                  pl.BlockSpec((None, 2, NA_NOFF + 1, GRID_W, Q), lambda b_, h: (h, 0, 0, 0, 0))],
        out_specs=pl.BlockSpec((None, T, Q), lambda b_, h: (b_, 0, h)),
        scratch_shapes=[pltpu.VMEM((Q, T), BF16),
                        pltpu.VMEM((NA_CASES, NA_NK, 2 * NA_NQ), F32),
                        pltpu.VMEM((NA_NK + NC, 2 * NA_NQ), F32),
                        pltpu.VMEM((NA_NK + NC, 2 * NA_NQ), F32),
                        pltpu.VMEM((NA_NK + NC, 2 * NA_NQ), BF16),
                        pltpu.VMEM((NA_NK + NC, 2 * NA_NQ), BF16),
                        pltpu.VMEM((8, 2 * NA_NQ), F32),
                        pltpu.VMEM((8, 2 * NA_NQ), F32)],
        compiler_params=_cparams(2),
        name="na_attn",
    )(p, p, p, bias_src)


SWA_KWIN = 3 * SWA_W
SWA_QB = 128
SWA_STACK = 4 * SWA_QB
ROPE_RC = 256


def _swap16(t, lane):
    a = pltpu.roll(t, 112, axis=1)
    b = pltpu.roll(t, 16, axis=1)
    return jnp.where((lane % 32) < 16, a, b)


def _swa_kernel(sink_ref, q_ref, k_ref, v_ref, cos_ref, sin_ref, mask_ref, o_ref, krot_ref, vt_ref,
                sa_ref, sb_ref, pa_ref, pb_ref, ra_ref, rb_ref):
    kp = pl.program_id(1)
    lane = lax.broadcasted_iota(jnp.int32, (SWA_QB, Q), 1)
    lo = lane < 64
    zero = jnp.zeros((SWA_QB, Q), BF16)
    col = lax.broadcasted_iota(jnp.int32, (1, SWA_STACK), 1)
    eye4 = jnp.where(lax.broadcasted_iota(jnp.int32, (SWA_STACK, Q), 0) % SWA_QB
                     == lax.broadcasted_iota(jnp.int32, (SWA_STACK, Q), 1), 1.0, 0.0).astype(BF16)

    def rope(t, r0, n):
        lane_n = lax.broadcasted_iota(jnp.int32, (n, Q), 1)
        return t * cos_ref[pl.ds(r0, n), :] + _swap16(t, lane_n) * sin_ref[pl.ds(r0, n), :]

    def krot_body(i, carry):
        r0 = pl.multiple_of(i * ROPE_RC, ROPE_RC)
        for gl in range(2):
            t = k_ref[pl.ds(r0, ROPE_RC), gl * Q:(gl + 1) * Q].astype(F32)
            krot_ref[pl.ds(r0, ROPE_RC), gl * Q:(gl + 1) * Q] = rope(t, r0, ROPE_RC).astype(BF16)
        return carry

    lax.fori_loop(0, L // ROPE_RC, krot_body, 0, unroll=2)

    def vt_body(c, t0):
        for gl in range(2):
            vt = v_ref[pl.ds(t0, Q), gl * Q:(gl + 1) * Q].astype(F32).T
            vt_ref[gl, :, pl.ds(t0, Q)] = vt[0:64].astype(BF16)

    _chunk_loop(NCH, 6, vt_body)

    def stack_q(qa, qb):
        return jnp.concatenate([jnp.where(lo, qa, zero), jnp.where(lo, zero, qa),
                                jnp.where(lo, qb, zero), jnp.where(lo, zero, qb)], axis=0)

    def sink_row(gl):
        base = kp * 8 + gl * 4
        row = jnp.where(col < SWA_QB, sink_ref[base],
                        jnp.where(col < 2 * SWA_QB, sink_ref[base + 1],
                                  jnp.where(col < 3 * SWA_QB, sink_ref[base + 2], sink_ref[base + 3])))
        return row * LOG2E

    def finish(q0, gl, ot, rden):
        ot = ot * rden
        pa = jnp.concatenate([ot[:, 0:SWA_QB], ot[:, SWA_QB:2 * SWA_QB]], axis=0).T
        pb = jnp.concatenate([ot[:, 2 * SWA_QB:3 * SWA_QB], ot[:, 3 * SWA_QB:4 * SWA_QB]], axis=0).T
        o_ref[pl.ds(q0, SWA_QB), gl * 2 * Q:(gl + 1) * 2 * Q] = jnp.concatenate([pa, pb], axis=1).astype(o_ref.dtype)

    def block_params(i):
        q0 = pl.multiple_of(i * SWA_QB, SWA_QB)
        ws = pl.multiple_of(jnp.clip((i - 1) * SWA_W, 0, L - SWA_KWIN), SWA_W)
        case = jnp.where(i >= 1, 1, 0) + jnp.where(i >= L // SWA_QB - 1, 1, 0)
        return q0, ws, case

    def qk(i, gl, s_ref):
        q0, ws, case = block_params(i)
        qa = rope(q_ref[pl.ds(q0, SWA_QB), gl * 2 * Q:gl * 2 * Q + Q].astype(F32), q0, SWA_QB) * QSCALE
        qb = rope(q_ref[pl.ds(q0, SWA_QB), gl * 2 * Q + Q:(gl + 1) * 2 * Q].astype(F32), q0, SWA_QB) * QSCALE
        qs = stack_q(qa.astype(BF16), qb.astype(BF16))
        a_loc = jnp.concatenate([krot_ref[pl.ds(ws, SWA_KWIN), gl * Q:(gl + 1) * Q], mask_ref[case]], axis=1)
        s_ref[0:SWA_KWIN, :] = _nt(a_loc, jnp.concatenate([qs, eye4], axis=1))
        s_ref[SWA_KWIN:SWA_KWIN + NC, :] = _nt(k_ref[L:T, gl * Q:(gl + 1) * Q], qs)

    def soft(gl, s_ref, p_ref, r_ref):
        s = s_ref[...]
        sink = sink_row(gl)
        mx = jnp.maximum(jnp.max(s, axis=0, keepdims=True), sink)
        e = jnp.exp2(s - mx)
        den = jnp.sum(e, axis=0, keepdims=True) + jnp.exp2(sink - mx)
        r_ref[...] = jnp.broadcast_to(1.0 / den, r_ref.shape)
        p_ref[...] = e.astype(BF16)

    def pv(i, gl, p_ref, r_ref):
        q0, ws, _ = block_params(i)
        ot = (_dot(vt_ref[gl, :, pl.ds(ws, SWA_KWIN)], p_ref[0:SWA_KWIN, :])
              + _dot(vt_ref[gl, :, L:T], p_ref[SWA_KWIN:SWA_KWIN + NC, :]))
        finish(q0, gl, ot, r_ref[0:1, :])

    nblk = L // SWA_QB
    qk(0, 0, sa_ref)
    qk(0, 1, sb_ref)
    soft(0, sa_ref, pa_ref, ra_ref)

    def body(i, carry):
        pv(i - 1, 0, pa_ref, ra_ref)
        qk(i, 0, sa_ref)
        soft(1, sb_ref, pb_ref, rb_ref)
        pv(i - 1, 1, pb_ref, rb_ref)
        qk(i, 1, sb_ref)
        soft(0, sa_ref, pa_ref, ra_ref)
        return carry

    lax.fori_loop(1, nblk, body, 0)
    pv(nblk - 1, 0, pa_ref, ra_ref)
    soft(1, sb_ref, pb_ref, rb_ref)
    pv(nblk - 1, 1, pb_ref, rb_ref)

    for cbk in range(NC // SWA_QB):
        q0 = L + cbk * SWA_QB
        scores = []
        for gl in range(2):
            qa = q_ref[q0:q0 + SWA_QB, gl * 2 * Q:gl * 2 * Q + Q].astype(F32) * QSCALE
            qb = q_ref[q0:q0 + SWA_QB, gl * 2 * Q + Q:(gl + 1) * 2 * Q].astype(F32) * QSCALE
            scores.append(_nt(k_ref[L:T, gl * Q:(gl + 1) * Q], stack_q(qa.astype(BF16), qb.astype(BF16))))
        soft = [_softmax_t([s], sink_row(gl)) for gl, s in enumerate(scores)]
        for gl, (es, rden) in enumerate(soft):
            finish(q0, gl, _pv_t([vt_ref[gl, :, L:T]], es), rden)


def _rope_tables():
    pos = np.arange(L)
    lane = np.arange(Q) % 64
    inv = ROPE_BASE ** (-(lane % 16).astype(np.float64) / 16.0)
    p = np.where(lane[None, :] < 32, (pos // GRID_W)[:, None], (pos % GRID_W)[:, None]).astype(np.float64)
    ang = p * inv[None, :]
    sign = np.where((lane % 32) < 16, -1.0, 1.0)[None, :]
    return jnp.asarray(np.cos(ang), F32), jnp.asarray(np.sin(ang) * sign, F32)


def _swa_mask_table():
    key = np.arange(SWA_KWIN)[:, None]
    qry = np.arange(SWA_QB)[None, :]
    tabs = [np.where(np.abs(key - qry + delta) <= SWA_W, 0.0, NEG) for delta in (0, -SWA_W, -2 * SWA_W)]
    return jnp.asarray(np.stack(tabs), BF16)


def _swa(p, sink, cos_t, sin_t, mask_t):
    b = p.shape[0]
    qo, ko, vo = C_SWAQ // 512, C_SWAK // 256, C_SWAV // 256
    return pl.pallas_call(
        _swa_kernel,
        out_shape=jax.ShapeDtypeStruct((b, T, D), BF16),
        grid=(b, SWA_KV // 2),
        in_specs=[pl.BlockSpec(memory_space=pltpu.SMEM),
                  pl.BlockSpec((None, T, 512), lambda b_, h: (b_, 0, qo + h)),
                  pl.BlockSpec((None, T, 256), lambda b_, h: (b_, 0, ko + h)),
                  pl.BlockSpec((None, T, 256), lambda b_, h: (b_, 0, vo + h)),
                  pl.BlockSpec((L, Q), lambda b_, h: (0, 0)),
                  pl.BlockSpec((L, Q), lambda b_, h: (0, 0)),
                  pl.BlockSpec((3, SWA_KWIN, SWA_QB), lambda b_, h: (0, 0, 0))],
        out_specs=pl.BlockSpec((None, T, 512), lambda b_, h: (b_, 0, h)),
        scratch_shapes=[pltpu.VMEM((L, 2 * Q), BF16),
                        pltpu.VMEM((2, 64, T), BF16),
                        pltpu.VMEM((SWA_KWIN + NC, SWA_STACK), F32),
                        pltpu.VMEM((SWA_KWIN + NC, SWA_STACK), F32),
                        pltpu.VMEM((SWA_KWIN + NC, SWA_STACK), BF16),
                        pltpu.VMEM((SWA_KWIN + NC, SWA_STACK), BF16),
                        pltpu.VMEM((8, SWA_STACK), F32),
                        pltpu.VMEM((8, SWA_STACK), F32)],
        compiler_params=_cparams(2),
        name="swa_attn",
    )(sink, p, p, p, cos_t, sin_t, mask_t)


def _merge_kernel(x_ref, ya_ref, yb_ref, yc_ref, g0_ref, g1_ref, g2_ref, mg_ref,
                  wa_ref, wb_ref, wc_ref, wo_ref, o_ref):
    m = (_sigmoid(g0_ref[...].astype(F32)) * _dot(ya_ref[...], wa_ref[...])
         + _sigmoid(g1_ref[...].astype(F32)) * _dot(yb_ref[...], wb_ref[...])
         + _sigmoid(g2_ref[...].astype(F32)) * _dot(yc_ref[...], wc_ref[...]))
    o_ref[...] = x_ref[...] + mg_ref[...] * _dot(m.astype(BF16), wo_ref[...])


def _const_spec(shape):
    return pl.BlockSpec(shape, lambda b, j: (0,) * len(shape))


def _merge(xs, ya, yb, yc, p, ms, wa, wb, wc, wo):
    b = xs.shape[0]
    go = C_GATE // D
    row = lambda w: pl.BlockSpec((None, TR, w), lambda b_, j: (b_, j, 0))
    gate = lambda k: pl.BlockSpec((None, TR, D), lambda b_, j: (b_, j, go + k))
    return pl.pallas_call(
        _merge_kernel,
        out_shape=jax.ShapeDtypeStruct((b, T, D), F32),
        grid=(b, NRT),
        in_specs=[row(D), row(SSD_INNER), row(D), row(D), gate(0), gate(1), gate(2), _mod_spec(2),
                  _const_spec((SSD_INNER, D)), _const_spec((D, D)), _const_spec((D, D)), _const_spec((D, D))],
        out_specs=row(D),
        compiler_params=_cparams(2),
        name="merge",
    )(xs, ya, yb, yc, p, p, p, ms, wa, wb, wc, wo)


FF_CH = 1024


def _ffn_kernel(*refs, emit_next):
    if emit_next:
        x_ref, g_ref, sh_ref, sc_ref, mg_ref, w1_ref, w2_ref, gn_ref, shn_ref, scn_ref, o_ref, h_ref = refs
    else:
        x_ref, g_ref, sh_ref, sc_ref, mg_ref, w1_ref, w2_ref, o_ref = refs
    x = x_ref[...]
    h = (_rms(x, g_ref[...]) * (1.0 + sc_ref[...]) + sh_ref[...]).astype(BF16)
    acc = None
    for kf in range(D_FF // FF_CH):
        a = jnp.maximum(_dot(h, w1_ref[:, kf * FF_CH:(kf + 1) * FF_CH]), 0.0)
        o = _dot((a * a).astype(BF16), w2_ref[kf * FF_CH:(kf + 1) * FF_CH, :])
        acc = o if acc is None else acc + o
    y = x + mg_ref[...] * acc
    o_ref[...] = y
    if emit_next:
        h_ref[...] = (_rms(y, gn_ref[...]) * (1.0 + scn_ref[...]) + shn_ref[...]).astype(h_ref.dtype)


def _ffn(xs, g, ms, w1, w2, g_next=None, ms_next=None):
    b = xs.shape[0]
    emit_next = g_next is not None
    row = pl.BlockSpec((None, TR, D), lambda b_, j: (b_, j, 0))
    in_specs = [row, _const_spec((1, D)), _mod_spec(3), _mod_spec(4), _mod_spec(5),
                _const_spec((D, D_FF)), _const_spec((D_FF, D))]
    args = [xs, g.reshape(1, D), ms, ms, ms, w1, w2]
    out_shape = jax.ShapeDtypeStruct((b, T, D), F32)
    out_specs = row
    if emit_next:
        in_specs += [_const_spec((1, D)), _mod_spec(0), _mod_spec(1)]
        args += [g_next.reshape(1, D), ms_next, ms_next]
        out_shape = [out_shape, jax.ShapeDtypeStruct((b, T, D), BF16)]
        out_specs = [row, row]
    return pl.pallas_call(
        functools.partial(_ffn_kernel, emit_next=emit_next),
        out_shape=out_shape,
        grid=(b, NRT),
        in_specs=in_specs,
        out_specs=out_specs,
        compiler_params=_cparams(2),
        name="ffn",
    )(*args)


def _final_kernel(x_ref, g_ref, o_ref):
    o_ref[...] = _rms(x_ref[...], g_ref[...])


def _final_norm(xs, g):
    b = xs.shape[0]
    return pl.pallas_call(
        _final_kernel,
        out_shape=jax.ShapeDtypeStruct((b, L, D), F32),
        grid=(b, L // TR),
        in_specs=[pl.BlockSpec((None, TR, D), lambda b_, j: (b_, j, 0)), _const_spec((1, D))],
        out_specs=pl.BlockSpec((None, TR, D), lambda b_, j: (b_, j, 0)),
        compiler_params=_cparams(2),
        name="final_norm",
    )(xs, g.reshape(1, D))


def _prep_w_in(w):
    def dup(a):
        a = a.reshape(D, SWA_KV, 1, 64)
        return jnp.concatenate([a, a], axis=2).reshape(D, SWA_KV * 128)
    cat = jnp.concatenate([
        w[:, R_XBC:R_DT], w[:, R_NAK:R_NAV], w[:, R_NAV:R_SWAK], dup(w[:, R_SWAK:R_SWAV]),
        dup(w[:, R_SWAV:R_Z]), w[:, R_Z:R_NAQ], w[:, R_NAQ:R_SWAQ], w[:, R_SWAQ:R_GATE],
        w[:, R_GATE:R_END]], axis=1).astype(BF16)
    wd = w[:, R_DT:R_NAK].reshape(D, 2, SSD_GROUPS, SSD_HPG).transpose(0, 2, 1, 3).reshape(D, 2 * SSD_HEADS)
    wdt = jnp.concatenate([wd, jnp.zeros((D, Q - 2 * SSD_HEADS), F32)], axis=1).astype(BF16)
    return cat, wdt


def kernel(x, c, ctx, c_ctx, ada_w, ada_b, norm1_g, norm2_g, w_in, conv_w, conv_b, dt_bias, a_log, ssd_d,
           ssd_norm_g, na_rpb, swa_sink, w_o_ssd, w_o_na, w_o_swa, w_out, w_ff1, w_ff2, final_g):
    b = x.shape[0]
    depth = ada_w.shape[0]
    xs = jnp.concatenate([x, ctx], axis=1)
    nrow = -(-(b + 1) // 8) * 8
    cvec = jnp.concatenate([c, c_ctx[None, :], jnp.zeros((nrow - b - 1, D), F32)], axis=0)
    mod = _ada_mod(cvec, ada_w, ada_b)
    lat = mod[:, :b].reshape(depth, b, 1, N_MOD, 1, D)
    cx = jnp.broadcast_to(mod[:, b].reshape(depth, 1, 1, N_MOD, 1, D), lat.shape)
    ms_all = jnp.concatenate([lat, cx], axis=2)
    cos_t, sin_t = _rope_tables()
    mask_t = _swa_mask_table()
    na_bias = _na_bias_source(na_rpb)

    h = _normmod(xs, norm1_g[0], ms_all[0])
    for l in range(depth):
        ms = ms_all[l]
        wcat, wdt = _prep_w_in(w_in[l])
        p = _matmul(h, wcat, BF16, 1024, "in_proj")
        dtr = _matmul(h, wdt, F32, Q, "dt_proj")
        xa = _conv_silu(p, conv_w[l], conv_b[l])
        ya = _ssd(xa, p, dtr, dt_bias[l], a_log[l], ssd_d[l], ssd_norm_g[l])
        yb = _na(p, na_bias[l])
        yc = _swa(p, swa_sink[l], cos_t, sin_t, mask_t)
        xs = _merge(xs, ya, yb, yc, p, ms, w_o_ssd[l].astype(BF16), w_o_na[l].astype(BF16),
                    w_o_swa[l].astype(BF16), w_out[l].astype(BF16))
        w1, w2 = w_ff1[l].astype(BF16), w_ff2[l].astype(BF16)
        if l + 1 < depth:
            xs, h = _ffn(xs, norm2_g[l], ms, w1, w2, norm1_g[l + 1], ms_all[l + 1])
        else:
            xs = _ffn(xs, norm2_g[l], ms, w1, w2)
    return _final_norm(xs, final_g)
```

```python
import functools

import numpy as np
import jax
import jax.numpy as jnp
from jax import lax
from jax.experimental import pallas as pl
from jax.experimental.pallas import tpu as pltpu

F32 = jnp.float32
BF16 = jnp.bfloat16
HIGHEST = lax.Precision.HIGHEST

D = 1024
L = 2048
NC = 256
T = L + NC
GRID_W = 64
HEAD_DIM = 64
ROWS = L // GRID_W
EPS = 1e-6
N_MOD = 6

SSD_INNER = 2 * D
SSD_HEADS = 32
SSD_GROUPS = 8
SSD_HPG = 4
SSD_STATE = 128
SSD_XBC = SSD_INNER + 2 * SSD_GROUPS * SSD_STATE
Q = 128
NCH = T // Q

NA_HEADS = 16
NA_KH = 8
NA_KW = 16
NA_QROWS = 2
NA_KROWS = 10
NA_NQ = NA_QROWS * GRID_W
NA_NK = NA_KROWS * GRID_W
NA_CASES = 5
NA_NOFF = 2 * NA_KH - 1

SWA_HEADS = 16
SWA_KV = 4
SWA_W = 128
ROPE_BASE = 10000.0
D_FF = 4 * D

R_XBC, R_DT, R_NAK, R_NAV, R_SWAK, R_SWAV, R_Z, R_NAQ, R_SWAQ, R_GATE, R_END = (
    0, 4096, 4160, 5184, 6208, 6464, 6720, 8768, 9792, 10816, 13888)
C_NAK, C_NAV, C_SWAK, C_SWAV, C_Z, C_NAQ, C_SWAQ, C_GATE, NCAT = (
    0, 1024, 2048, 2560, 3072, 5120, 6144, 7168, 10240)

NEG = -1e30
LOG2E = float(np.log2(np.e))
QSCALE = 0.125 * LOG2E
VMEM_LIMIT = 48 * 1024 * 1024


def _cparams(n_axes):
    return pltpu.CompilerParams(dimension_semantics=("parallel",) * n_axes,
                                vmem_limit_bytes=VMEM_LIMIT)


def _sigmoid(x):
    return 1.0 / (1.0 + jnp.exp(-x))


def _softplus(x):
    return jnp.maximum(x, 0.0) + jnp.log1p(jnp.exp(-jnp.abs(x)))


def _nt(a, b):
    return lax.dot_general(a, b, (((1,), (1,)), ((), ())), preferred_element_type=F32)


def _tn(a, b):
    return lax.dot_general(a, b, (((0,), (0,)), ((), ())), preferred_element_type=F32)


def _dot(a, b):
    return jnp.dot(a, b, preferred_element_type=F32)


def _chunk_loop(n, unroll, body):
    def wrapped(c, carry):
        body(c, pl.multiple_of(c * Q, Q))
        return carry
    lax.fori_loop(0, n, wrapped, 0, unroll=unroll)


def _ada_kernel(c_ref, w_ref, b_ref, o_ref):
    c = c_ref[...]
    s = (c * _sigmoid(c)).astype(BF16)
    o_ref[...] = _dot(s, w_ref[...].astype(BF16)) + b_ref[...]


def _ada_mod(cvec, ada_w, ada_b):
    depth = ada_w.shape[0]
    r = cvec.shape[0]
    return pl.pallas_call(
        _ada_kernel,
        out_shape=jax.ShapeDtypeStruct((depth, r, N_MOD * D), F32),
        grid=(depth, N_MOD),
        in_specs=[pl.BlockSpec((r, D), lambda l, n: (0, 0)),
                  pl.BlockSpec((None, D, D), lambda l, n: (l, 0, n)),
                  pl.BlockSpec((None, 1, D), lambda l, n: (l, 0, n))],
        out_specs=pl.BlockSpec((None, r, D), lambda l, n: (l, 0, n)),
        compiler_params=_cparams(2),
        name="ada_mod",
    )(cvec, ada_w, ada_b.reshape(depth, 1, N_MOD * D))


TR = 256
NRT = T // TR


def _mod_spec(which):
    return pl.BlockSpec((None, None, None, 1, D), lambda b, j: (b, j // (L // TR), which, 0, 0))


def _rms(x, g):
    return x * lax.rsqrt(jnp.mean(x * x, axis=-1, keepdims=True) + EPS) * g


def _normmod_kernel(x_ref, g_ref, sh_ref, sc_ref, o_ref):
    y = _rms(x_ref[...], g_ref[...])
    o_ref[...] = (y * (1.0 + sc_ref[...]) + sh_ref[...]).astype(o_ref.dtype)


def _normmod(xs, g, ms):
    b = xs.shape[0]
    return pl.pallas_call(
        _normmod_kernel,
        out_shape=jax.ShapeDtypeStruct((b, T, D), BF16),
        grid=(b, NRT),
        in_specs=[pl.BlockSpec((None, TR, D), lambda b, j: (b, j, 0)),
                  pl.BlockSpec((1, D), lambda b, j: (0, 0)),
                  _mod_spec(0), _mod_spec(1)],
        out_specs=pl.BlockSpec((None, TR, D), lambda b, j: (b, j, 0)),
        compiler_params=_cparams(2),
        name="normmod",
    )(xs, g.reshape(1, D), ms, ms)


def _mm_kernel(x_ref, w_ref, o_ref):
    o_ref[...] = _dot(x_ref[...], w_ref[...]).astype(o_ref.dtype)


def _matmul(x, w, out_dtype, tn, name):
    b, t, k = x.shape
    n = w.shape[1]
    return pl.pallas_call(
        _mm_kernel,
        out_shape=jax.ShapeDtypeStruct((b, t, n), out_dtype),
        grid=(n // tn, b),
        in_specs=[pl.BlockSpec((None, t, k), lambda n_, b_: (b_, 0, 0)),
                  pl.BlockSpec((k, tn), lambda n_, b_: (0, n_))],
        out_specs=pl.BlockSpec((None, t, tn), lambda n_, b_: (b_, 0, n_)),
        compiler_params=_cparams(2),
        name=name,
    )(x, w)


CW = 512
CONV_RC = 128
PAD_LAT = 8
PAD_CTX = 16 + L


def _proj_conv_kernel(h_ref, wp_ref, w_ref, b_ref, o_ref, pad_ref):
    z8 = jnp.zeros((8, CW), F32)
    pad_ref[0:8, :] = z8
    pad_ref[PAD_LAT + L:PAD_CTX, :] = z8
    pad_ref[PAD_CTX + NC:PAD_CTX + NC + 8, :] = z8
    w = w_ref[...]
    bias = b_ref[...]
    n_chunks = T // CONV_RC

    def base(c):
        r0 = c * CONV_RC
        return (PAD_LAT if r0 < L else PAD_CTX - L) + r0

    def project(c):
        r0 = c * CONV_RC
        pad_ref[base(c):base(c) + CONV_RC, :] = _dot(h_ref[r0:r0 + CONV_RC, :], wp_ref[...])

    def conv(c):
        r0 = c * CONV_RC
        win = pad_ref[base(c) - 8:base(c) + CONV_RC + 8, :]
        acc = bias + win[8:8 + CONV_RC] * w[2:3, :]
        for k in (0, 1, 3, 4):
            shifted = pltpu.roll(win, (2 - k) % (CONV_RC + 16), axis=0)
            acc = acc + shifted[8:8 + CONV_RC] * w[k:k + 1, :]
        o_ref[r0:r0 + CONV_RC, :] = (acc * _sigmoid(acc)).astype(o_ref.dtype)

    project(0)
    for c in range(n_chunks):
        if c + 1 < n_chunks:
            project(c + 1)
        conv(c)


def _proj_conv(h, wx, conv_w, conv_b):
    b = h.shape[0]
    w8 = jnp.concatenate([conv_w, jnp.zeros((3, SSD_XBC), F32)], axis=0)
    return pl.pallas_call(
        _proj_conv_kernel,
        out_shape=jax.ShapeDtypeStruct((b, T, SSD_XBC), BF16),
        grid=(SSD_XBC // CW, b),
        in_specs=[pl.BlockSpec((None, T, D), lambda n, b_: (b_, 0, 0)),
                  pl.BlockSpec((D, CW), lambda n, b_: (0, n)),
                  pl.BlockSpec((8, CW), lambda n, b_: (0, n)),
                  pl.BlockSpec((1, CW), lambda n, b_: (0, n))],
        out_specs=pl.BlockSpec((None, T, CW), lambda n, b_: (b_, 0, n)),
        scratch_shapes=[pltpu.VMEM((T + 24, CW), F32)],
        compiler_params=_cparams(2),
        name="proj_conv",
    )(h, wx, w8, conv_b.reshape(1, SSD_XBC))


GW = SSD_HPG * HEAD_DIM


def _split3(x):
    hi = x.astype(BF16)
    r = x - hi.astype(F32)
    mid = r.astype(BF16)
    lo = (r - mid.astype(F32)).astype(BF16)
    return hi, mid, lo


def _cum_constants():
    ii = lax.broadcasted_iota(jnp.int32, (Q, Q), 0)
    jj = lax.broadcasted_iota(jnp.int32, (Q, Q), 1)
    tril = jnp.where(ii >= jj, 1.0, 0.0).astype(BF16)
    triu = jnp.where(ii <= jj, 1.0, 0.0).astype(BF16)
    return jnp.concatenate([jnp.concatenate([tril] * 3, axis=1),
                            jnp.concatenate([triu] * 3, axis=1)], axis=0)


def _ssd_prep_kernel(dt_ref, bias_ref, alog_ref, accg_o, wts_o, eacc_o, rowt_o, dtt_o):
    lcat = _cum_constants()
    isf = (lax.broadcasted_iota(jnp.int32, (1, Q), 1) % 8) < 4
    bias = bias_ref[...]
    a_r = -jnp.exp(alog_ref[...])

    def body(c, t0):
        rows = pl.ds(t0, Q)
        dt = _softplus(dt_ref[rows, :] + bias)
        both = _dot(lcat, jnp.concatenate(_split3(dt * a_r), axis=0))
        acc = jnp.where(isf, both[0:Q], both[Q:2 * Q])
        tot = jnp.where(isf, both[Q - 1:Q], both[Q:Q + 1])
        for g in range(SSD_GROUPS):
            accg_o[g, rows, :] = acc if g == 0 else pltpu.roll(acc, Q - 8 * g, axis=1)
        wts_o[rows, :] = dt * jnp.exp(tot - acc)
        eacc_o[rows, :] = jnp.exp(acc)
        dt_t = dt.T
        rowt_o[:, rows] = acc.T - jnp.log(dt_t)
        dtt_o[:, rows] = dt_t

    _chunk_loop(NCH, 3, body)


def _ssd_kernel(x_ref, b_ref, c_ref, z_ref, acc_ref, wts_ref, eacc_ref, rowt_ref, dtt_ref, dsk_ref, ng_ref,
                o_ref, xw_s, ec_s, big_s, sin_s, cb_s, y_s):
    g = pl.program_id(1)
    ii = lax.broadcasted_iota(jnp.int32, (Q, Q), 0)
    jj = lax.broadcasted_iota(jnp.int32, (Q, Q), 1)
    low = ii > jj
    up = ii < jj
    lo = jj < HEAD_DIM
    ek = lax.broadcasted_iota(jnp.int32, (Q, 2 * GW), 0)
    ech = lax.broadcasted_iota(jnp.int32, (Q, 2 * GW), 1)
    esel = jnp.where(ek == g * 8 + ech // HEAD_DIM, 1.0, 0.0).astype(BF16)
    esel2 = jnp.concatenate([esel] * 2, axis=0)
    zero_x = jnp.zeros((Q, Q), BF16)
    rows = lambda t0: pl.ds(t0, Q)

    def stage_expand(c, t0):
        w2 = _split3(wts_ref[rows(t0), :])[:2]
        e2 = _split3(eacc_ref[rows(t0), :])[:2]
        lhs = jnp.concatenate([jnp.concatenate(w2, axis=1), jnp.concatenate(e2, axis=1)], axis=0)
        both = _dot(lhs, esel2)
        xs = x_ref[rows(t0), :].astype(F32)
        xw_s[rows(t0), :] = (jnp.concatenate([xs, xs], axis=1) * both[0:Q]).astype(BF16)
        ec_s[rows(t0), :] = both[Q:2 * Q]

    _chunk_loop(NCH, 3, stage_expand)

    def stage_upd(c, t0):
        big_s[c] = _tn(b_ref[rows(t0), :], xw_s[rows(t0), :])

    _chunk_loop(NCH, 6, stage_upd)

    s = jnp.zeros((SSD_STATE, GW), F32)
    for c in list(range(L // Q, NCH)) + list(range(L // Q)):
        sin_s[c, :, 0:GW] = s.astype(BF16)
        s = s * ec_s[c * Q + Q - 1:c * Q + Q, 0:GW] + big_s[c, :, 0:GW]
    s = jnp.zeros((SSD_STATE, GW), F32)
    for c in reversed(range(NCH)):
        sin_s[c, :, GW:2 * GW] = s.astype(BF16)
        s = s * ec_s[c * Q:c * Q + 1, GW:2 * GW] + big_s[c, :, GW:2 * GW]

    def stage_cb(c, t0):
        cc = c_ref[rows(t0), :]
        cb_s[rows(t0), :] = _nt(cc, b_ref[rows(t0), :])
        big_s[c] = _dot(cc, sin_s[c])

    _chunk_loop(NCH, 6, stage_cb)

    dsk = dsk_ref[...]
    ng = ng_ref[...]

    def colb(v, k):
        return jnp.broadcast_to(v[:, k:k + 1], (Q, Q))

    def stage_y(c, t0):
        xb = x_ref[rows(t0), :]
        acc = acc_ref[rows(t0), :]
        rowt = rowt_ref[:, rows(t0)]
        dt_t = dtt_ref[:, rows(t0)]
        ec = ec_s[rows(t0), :]
        cb = cb_s[rows(t0), :]
        yoff = big_s[c]
        ys = []
        for pr in range(2):
            ms = []
            for hh in range(2):
                r = 2 * pr + hh
                diag = jnp.log(dt_t[r:r + 1, :] + dt_t[4 + r:5 + r, :])
                seg = jnp.where(low, colb(acc, r) - rowt[r:r + 1, :],
                                jnp.where(up, colb(acc, 4 + r) - rowt[4 + r:5 + r, :], diag))
                ms.append((cb * jnp.exp(seg)).astype(BF16))
            xp = xb[:, pr * Q:(pr + 1) * Q]
            xstack = jnp.concatenate([jnp.where(lo, xp, zero_x), jnp.where(lo, zero_x, xp)], axis=0)
            y = _dot(jnp.concatenate(ms, axis=1), xstack)
            y = (y + ec[:, pr * Q:(pr + 1) * Q] * yoff[:, pr * Q:(pr + 1) * Q]
                 + ec[:, GW + pr * Q:GW + (pr + 1) * Q] * yoff[:, GW + pr * Q:GW + (pr + 1) * Q])
            ys.append(y)
        y_s[rows(t0), :] = jnp.concatenate(ys, axis=1)

    _chunk_loop(NCH, 2, stage_y)

    def stage_out(c, t0):
        y = y_s[rows(t0), :] + dsk * x_ref[rows(t0), :].astype(F32)
        zf = z_ref[rows(t0), :].astype(F32)
        u = y * (zf * _sigmoid(zf))
        o_ref[rows(t0), :] = _rms(u, ng).astype(o_ref.dtype)

    _chunk_loop(NCH, 3, stage_out)


def _slot_order(v):
    s = v.reshape(2, SSD_GROUPS, SSD_HPG).transpose(1, 0, 2).reshape(1, 2 * SSD_HEADS)
    return jnp.pad(s, ((0, 0), (0, Q - 2 * SSD_HEADS)))


def _ssd(xa, p, dtr, dt_bias, a_log, ssd_d, ssd_norm_g):
    b = xa.shape[0]
    row = pl.BlockSpec((None, T, Q), lambda b_: (b_, 0, 0))
    col = pl.BlockSpec((None, Q, T), lambda b_: (b_, 0, 0))
    one = pl.BlockSpec((1, Q), lambda b_: (0, 0))
    accg, wts, eacc, rowt, dtt = pl.pallas_call(
        _ssd_prep_kernel,
        out_shape=([jax.ShapeDtypeStruct((b, SSD_GROUPS, T, Q), F32)] + [jax.ShapeDtypeStruct((b, T, Q), F32)] * 2
                   + [jax.ShapeDtypeStruct((b, Q, T), F32)] * 2),
        grid=(b,),
        in_specs=[row, one, one],
        out_specs=[pl.BlockSpec((None, SSD_GROUPS, T, Q), lambda b_: (b_, 0, 0, 0)), row, row, col, col],
        compiler_params=_cparams(1),
        name="ssd_prep",
    )(dtr, _slot_order(dt_bias), _slot_order(a_log))

    dsk = jnp.repeat(ssd_d, HEAD_DIM).reshape(1, SSD_INNER)
    ng = ssd_norm_g.reshape(1, SSD_INNER)
    xoff, boff, coff, zoff = 0, SSD_INNER // Q, (SSD_INNER + SSD_GROUPS * SSD_STATE) // Q, C_Z // GW
    full = pl.BlockSpec((None, T, Q), lambda b_, g: (b_, 0, 0))
    slot = pl.BlockSpec((None, 8, T), lambda b_, g: (b_, g, 0))
    return pl.pallas_call(
        _ssd_kernel,
        out_shape=jax.ShapeDtypeStruct((b, T, SSD_INNER), BF16),
        grid=(b, SSD_GROUPS),
        in_specs=[pl.BlockSpec((None, T, GW), lambda b_, g: (b_, 0, xoff + g)),
                  pl.BlockSpec((None, T, Q), lambda b_, g: (b_, 0, boff + g)),
                  pl.BlockSpec((None, T, Q), lambda b_, g: (b_, 0, coff + g)),
                  pl.BlockSpec((None, T, GW), lambda b_, g: (b_, 0, zoff + g)),
                  pl.BlockSpec((None, None, T, Q), lambda b_, g: (b_, g, 0, 0)), full, full, slot, slot,
                  pl.BlockSpec((1, GW), lambda b_, g: (0, g)),
                  pl.BlockSpec((1, GW), lambda b_, g: (0, g))],
        out_specs=pl.BlockSpec((None, T, GW), lambda b_, g: (b_, 0, g)),
        scratch_shapes=[pltpu.VMEM((T, 2 * GW), BF16),
                        pltpu.VMEM((T, 2 * GW), F32),
                        pltpu.VMEM((NCH, SSD_STATE, 2 * GW), F32),
                        pltpu.VMEM((NCH, SSD_STATE, 2 * GW), BF16),
                        pltpu.VMEM((T, Q), F32),
                        pltpu.VMEM((T, GW), F32)],
        compiler_params=_cparams(2),
        name="ssd",
    )(xa, xa, xa, p, accg, wts, eacc, rowt, dtt, dsk, ng)


def _softmax_t(parts, extra=None):
    mx = None
    for s in parts:
        r = jnp.max(s, axis=0, keepdims=True)
        mx = r if mx is None else jnp.maximum(mx, r)
    if extra is not None:
        mx = jnp.maximum(mx, extra)
    den = None
    es = []
    for s in parts:
        e = jnp.exp2(s - mx)
        d = jnp.sum(e, axis=0, keepdims=True)
        den = d if den is None else den + d
        es.append(e.astype(BF16))
    if extra is not None:
        den = den + jnp.exp2(extra - mx)
    return es, 1.0 / den


def _pv_t(vts, es):
    out = None
    for vt, e in zip(vts, es):
        o = _dot(vt, e)
        out = o if out is None else out + o
    return out


def _na_row_offsets():
    table = []
    for j in (0, 1, 2, ROWS // NA_QROWS - 2, ROWS // NA_QROWS - 1):
        start = int(np.clip(NA_QROWS * j - NA_KH // 2, 0, ROWS - NA_KROWS))
        per_row = []
        for i in range(NA_KROWS):
            kr = start + i
            offs = []
            for qr in range(NA_QROWS):
                r = NA_QROWS * j + qr
                rs = int(np.clip(r - NA_KH // 2, 0, ROWS - NA_KH))
                offs.append(kr - r + NA_KH - 1 if rs <= kr < rs + NA_KH else NA_NOFF)
            per_row.append(tuple(offs))
        table.append(per_row)
    return table


def _na_kernel(q_ref, k_ref, v_ref, src_ref, o_ref, vt_ref, bias_ref, sa_ref, sb_ref, pa_ref, pb_ref, ra_ref,
               rb_ref):
    @pl.when(pl.program_id(1) == 0)
    def _():
        lo_tile = lax.broadcasted_iota(jnp.int32, (GRID_W, Q), 1) < GRID_W
        for ci, per_row in enumerate(_na_row_offsets()):
            for i, (off0, off1) in enumerate(per_row):
                for hh in range(2):
                    bias_ref[ci, i * GRID_W:(i + 1) * GRID_W, hh * Q:(hh + 1) * Q] = jnp.where(
                        lo_tile, src_ref[hh, off0], src_ref[hh, off1])

    lane = lax.broadcasted_iota(jnp.int32, (NA_NQ, Q), 1)
    lo = lane < HEAD_DIM
    top = lax.broadcasted_iota(jnp.int32, (Q, NA_NQ), 0) < HEAD_DIM
    zero = jnp.zeros((NA_NQ, Q), BF16)

    def vt_body(c, t0):
        vt_ref[:, pl.ds(t0, Q)] = v_ref[pl.ds(t0, Q), :].astype(F32).T.astype(BF16)

    _chunk_loop(NCH, 6, vt_body)

    kc = k_ref[L:T, :]
    vtc = vt_ref[:, L:T]

    def qpair(q0):
        q = (q_ref[pl.ds(q0, NA_NQ), :].astype(F32) * QSCALE).astype(BF16)
        return jnp.concatenate([jnp.where(lo, q, zero), jnp.where(lo, zero, q)], axis=0)

    def finish(q0, ot, rden):
        ot = ot * rden
        w = jnp.where(top, ot[:, 0:NA_NQ], ot[:, NA_NQ:2 * NA_NQ])
        o_ref[pl.ds(q0, NA_NQ), :] = w.T.astype(o_ref.dtype)

    def block_params(j):
        q0 = pl.multiple_of(j * NA_NQ, NA_NQ)
        start = jnp.clip(NA_QROWS * j - NA_KH // 2, 0, ROWS - NA_KROWS)
        koff = pl.multiple_of(start * GRID_W, 2 * GRID_W)
        last = ROWS // NA_QROWS - 1
        case = (jnp.where(j >= 1, 1, 0) + jnp.where(j >= 2, 1, 0)
                + jnp.where(j >= last - 1, 1, 0) + jnp.where(j >= last, 1, 0))
        return q0, koff, case

    def qk(j, s_ref):
        q0, koff, case = block_params(j)
        qp = qpair(q0)
        s_ref[0:NA_NK, :] = _nt(k_ref[pl.ds(koff, NA_NK), :], qp) + bias_ref[case]
        s_ref[NA_NK:NA_NK + NC, :] = _nt(kc, qp)

    def soft(s_ref, p_ref, r_ref):
        s = s_ref[...]
        e = jnp.exp2(s - jnp.max(s, axis=0, keepdims=True))
        r_ref[...] = jnp.broadcast_to(1.0 / jnp.sum(e, axis=0, keepdims=True), r_ref.shape)
        p_ref[...] = e.astype(BF16)

    def pv(j, p_ref, r_ref):
        q0, koff, _ = block_params(j)
        ot = (_dot(vt_ref[:, pl.ds(koff, NA_NK)], p_ref[0:NA_NK, :])
              + _dot(vtc, p_ref[NA_NK:NA_NK + NC, :]))
        finish(q0, ot, r_ref[0:1, :])

    nblk = L // NA_NQ
    qk(0, sa_ref)
    qk(1, sb_ref)
    soft(sa_ref, pa_ref, ra_ref)

    def body(m, carry):
        pv(2 * m - 2, pa_ref, ra_ref)
        qk(2 * m, sa_ref)
        soft(sb_ref, pb_ref, rb_ref)
        pv(2 * m - 1, pb_ref, rb_ref)
        qk(2 * m + 1, sb_ref)
        soft(sa_ref, pa_ref, ra_ref)
        return carry

    lax.fori_loop(1, nblk // 2, body, 0)
    pv(nblk - 2, pa_ref, ra_ref)
    soft(sb_ref, pb_ref, rb_ref)
    pv(nblk - 1, pb_ref, rb_ref)

    ctx_blocks = []
    for cbk in range(NC // NA_NQ):
        q0 = L + cbk * NA_NQ
        ctx_blocks.append((q0, _nt(kc, qpair(q0))))
    ctx_soft = [(q0,) + _softmax_t([s]) for q0, s in ctx_blocks]
    for q0, es, rden in ctx_soft:
        finish(q0, _pv_t([vtc], es), rden)


def _na_bias_source(rpb_all):
    depth = rpb_all.shape[0]
    ck = np.arange(GRID_W)[:, None]
    cq = (np.arange(Q) % GRID_W)[None, :]
    col_oh = ((ck - cq + NA_KW - 1)[None] == np.arange(2 * NA_KW - 1)[:, None, None]).astype(np.float32)
    cs = np.clip(cq - NA_KW // 2, 0, GRID_W - NA_KW)
    col_ok = (ck >= cs) & (ck < cs + NA_KW)
    rp = rpb_all.reshape(depth, NA_HEADS // 2, 2, NA_NOFF, 2 * NA_KW - 1)
    src = jnp.einsum("lperd,dbz->lperbz", rp, col_oh, precision=HIGHEST)
    src = jnp.where(col_ok, src * LOG2E, NEG)
    return jnp.concatenate([src, jnp.full_like(src[:, :, :, :1], NEG)], axis=3)


def _na(p, bias_src):
    b = p.shape[0]
    qo, ko, vo = C_NAQ // Q, C_NAK // Q, C_NAV // Q
    return pl.pallas_call(
        _na_kernel,
        out_shape=jax.ShapeDtypeStruct((b, T, D), BF16),
        grid=(NA_HEADS // 2, b),
        in_specs=[pl.BlockSpec((None, T, Q), lambda h, b_: (b_, 0, qo + h)),
                  pl.BlockSpec((None, T, Q), lambda h, b_: (b_, 0, ko + h)),
                  pl.BlockSpec((None, T, Q), lambda h, b_: (b_, 0, vo + h)),
                  pl.BlockSpec((None, 2, NA_NOFF + 1, GRID_W, Q), lambda h, b_: (h, 0, 0, 0, 0))],
        out_specs=pl.BlockSpec((None, T, Q), lambda h, b_: (b_, 0, h)),
        scratch_shapes=[pltpu.VMEM((Q, T), BF16),
                        pltpu.VMEM((NA_CASES, NA_NK, 2 * NA_NQ), F32),
                        pltpu.VMEM((NA_NK + NC, 2 * NA_NQ), F32),
                        pltpu.VMEM((NA_NK + NC, 2 * NA_NQ), F32),
                        pltpu.VMEM((NA_NK + NC, 2 * NA_NQ), BF16),
                        pltpu.VMEM((NA_NK + NC, 2 * NA_NQ), BF16),
                        pltpu.VMEM((8, 2 * NA_NQ), F32),
                        pltpu.VMEM((8, 2 * NA_NQ), F32)],
        compiler_params=pltpu.CompilerParams(dimension_semantics=("parallel", "arbitrary"),
                                             vmem_limit_bytes=VMEM_LIMIT),
        name="na_attn",
    )(p, p, p, bias_src)


SWA_KWIN = 3 * SWA_W
SWA_QB = 128
SWA_STACK = 4 * SWA_QB
ROPE_RC = 256


def _swap_rotary_halves(t, lane):
    quarter = HEAD_DIM // 4
    a = pltpu.roll(t, Q - quarter, axis=1)
    b = pltpu.roll(t, quarter, axis=1)
    return jnp.where((lane % (2 * quarter)) < quarter, a, b)


def _swa_kernel(sink_ref, q_ref, k_ref, v_ref, cos_ref, sin_ref, mask_ref, o_ref, krot_ref, vt_ref,
                sa_ref, sb_ref, pa_ref, pb_ref, ra_ref, rb_ref):
    kp = pl.program_id(1)
    lane = lax.broadcasted_iota(jnp.int32, (SWA_QB, Q), 1)
    lo = lane < HEAD_DIM
    zero = jnp.zeros((SWA_QB, Q), BF16)
    col = lax.broadcasted_iota(jnp.int32, (1, SWA_STACK), 1)
    eye4 = jnp.where(lax.broadcasted_iota(jnp.int32, (SWA_STACK, Q), 0) % SWA_QB
                     == lax.broadcasted_iota(jnp.int32, (SWA_STACK, Q), 1), 1.0, 0.0).astype(BF16)

    def rope(t, r0, n):
        lane_n = lax.broadcasted_iota(jnp.int32, (n, Q), 1)
        return t * cos_ref[pl.ds(r0, n), :] + _swap_rotary_halves(t, lane_n) * sin_ref[pl.ds(r0, n), :]

    def krot_body(i, carry):
        r0 = pl.multiple_of(i * ROPE_RC, ROPE_RC)
        for gl in range(2):
            t = k_ref[pl.ds(r0, ROPE_RC), gl * Q:(gl + 1) * Q].astype(F32)
            krot_ref[pl.ds(r0, ROPE_RC), gl * Q:(gl + 1) * Q] = rope(t, r0, ROPE_RC).astype(BF16)
        return carry

    lax.fori_loop(0, L // ROPE_RC, krot_body, 0, unroll=2)

    def vt_body(c, t0):
        for gl in range(2):
            vt = v_ref[pl.ds(t0, Q), gl * Q:(gl + 1) * Q].astype(F32).T
            vt_ref[gl, :, pl.ds(t0, Q)] = vt[0:HEAD_DIM].astype(BF16)

    _chunk_loop(NCH, 6, vt_body)

    def stack_q(qa, qb):
        return jnp.concatenate([jnp.where(lo, qa, zero), jnp.where(lo, zero, qa),
                                jnp.where(lo, qb, zero), jnp.where(lo, zero, qb)], axis=0)

    def sink_row(gl):
        base = kp * 8 + gl * 4
        row = jnp.where(col < SWA_QB, sink_ref[base],
                        jnp.where(col < 2 * SWA_QB, sink_ref[base + 1],
                                  jnp.where(col < 3 * SWA_QB, sink_ref[base + 2], sink_ref[base + 3])))
        return row * LOG2E

    def finish(q0, gl, ot, rden):
        ot = ot * rden
        pa = jnp.concatenate([ot[:, 0:SWA_QB], ot[:, SWA_QB:2 * SWA_QB]], axis=0).T
        pb = jnp.concatenate([ot[:, 2 * SWA_QB:3 * SWA_QB], ot[:, 3 * SWA_QB:4 * SWA_QB]], axis=0).T
        o_ref[pl.ds(q0, SWA_QB), gl * 2 * Q:(gl + 1) * 2 * Q] = jnp.concatenate([pa, pb], axis=1).astype(o_ref.dtype)

    def block_params(i):
        q0 = pl.multiple_of(i * SWA_QB, SWA_QB)
        ws = pl.multiple_of(jnp.clip((i - 1) * SWA_W, 0, L - SWA_KWIN), SWA_W)
        case = jnp.where(i >= 1, 1, 0) + jnp.where(i >= L // SWA_QB - 1, 1, 0)
        return q0, ws, case

    def qk(i, gl, s_ref):
        q0, ws, case = block_params(i)
        qa = rope(q_ref[pl.ds(q0, SWA_QB), gl * 2 * Q:gl * 2 * Q + Q].astype(F32), q0, SWA_QB) * QSCALE
        qb = rope(q_ref[pl.ds(q0, SWA_QB), gl * 2 * Q + Q:(gl + 1) * 2 * Q].astype(F32), q0, SWA_QB) * QSCALE
        qs = stack_q(qa.astype(BF16), qb.astype(BF16))
        a_loc = jnp.concatenate([krot_ref[pl.ds(ws, SWA_KWIN), gl * Q:(gl + 1) * Q], mask_ref[case]], axis=1)
        s_ref[0:SWA_KWIN, :] = _nt(a_loc, jnp.concatenate([qs, eye4], axis=1))
        s_ref[SWA_KWIN:SWA_KWIN + NC, :] = _nt(k_ref[L:T, gl * Q:(gl + 1) * Q], qs)

    def soft(gl, s_ref, p_ref, r_ref):
        s = s_ref[...]
        sink = sink_row(gl)
        mx = jnp.maximum(jnp.max(s, axis=0, keepdims=True), sink)
        e = jnp.exp2(s - mx)
        den = jnp.sum(e, axis=0, keepdims=True) + jnp.exp2(sink - mx)
        r_ref[...] = jnp.broadcast_to(1.0 / den, r_ref.shape)
        p_ref[...] = e.astype(BF16)

    def pv(i, gl, p_ref, r_ref):
        q0, ws, _ = block_params(i)
        ot = (_dot(vt_ref[gl, :, pl.ds(ws, SWA_KWIN)], p_ref[0:SWA_KWIN, :])
              + _dot(vt_ref[gl, :, L:T], p_ref[SWA_KWIN:SWA_KWIN + NC, :]))
        finish(q0, gl, ot, r_ref[0:1, :])

    nblk = L // SWA_QB
    qk(0, 0, sa_ref)
    qk(0, 1, sb_ref)
    soft(0, sa_ref, pa_ref, ra_ref)

    def body(i, carry):
        pv(i - 1, 0, pa_ref, ra_ref)
        qk(i, 0, sa_ref)
        soft(1, sb_ref, pb_ref, rb_ref)
        pv(i - 1, 1, pb_ref, rb_ref)
        qk(i, 1, sb_ref)
        soft(0, sa_ref, pa_ref, ra_ref)
        return carry

    lax.fori_loop(1, nblk, body, 0)
    pv(nblk - 1, 0, pa_ref, ra_ref)
    soft(1, sb_ref, pb_ref, rb_ref)
    pv(nblk - 1, 1, pb_ref, rb_ref)

    for cbk in range(NC // SWA_QB):
        q0 = L + cbk * SWA_QB
        scores = []
        for gl in range(2):
            qa = q_ref[q0:q0 + SWA_QB, gl * 2 * Q:gl * 2 * Q + Q].astype(F32) * QSCALE
            qb = q_ref[q0:q0 + SWA_QB, gl * 2 * Q + Q:(gl + 1) * 2 * Q].astype(F32) * QSCALE
            scores.append(_nt(k_ref[L:T, gl * Q:(gl + 1) * Q], stack_q(qa.astype(BF16), qb.astype(BF16))))
        soft = [_softmax_t([s], sink_row(gl)) for gl, s in enumerate(scores)]
        for gl, (es, rden) in enumerate(soft):
            finish(q0, gl, _pv_t([vt_ref[gl, :, L:T]], es), rden)


def _rope_tables():
    pos = np.arange(L)
    quarter = HEAD_DIM // 4
    lane = np.arange(Q) % HEAD_DIM
    inv = ROPE_BASE ** (-(lane % quarter).astype(np.float64) / quarter)
    p = np.where(lane[None, :] < HEAD_DIM // 2, (pos // GRID_W)[:, None], (pos % GRID_W)[:, None]).astype(np.float64)
    ang = p * inv[None, :]
    sign = np.where((lane % (2 * quarter)) < quarter, -1.0, 1.0)[None, :]
    return jnp.asarray(np.cos(ang), F32), jnp.asarray(np.sin(ang) * sign, F32)


def _swa_mask_table():
    key = np.arange(SWA_KWIN)[:, None]
    qry = np.arange(SWA_QB)[None, :]
    tabs = [np.where(np.abs(key - qry + delta) <= SWA_W, 0.0, NEG) for delta in (0, -SWA_W, -2 * SWA_W)]
    return jnp.asarray(np.stack(tabs), BF16)


def _swa(p, sink, cos_t, sin_t, mask_t):
    b = p.shape[0]
    qo, ko, vo = C_SWAQ // 512, C_SWAK // 256, C_SWAV // 256
    return pl.pallas_call(
        _swa_kernel,
        out_shape=jax.ShapeDtypeStruct((b, T, D), BF16),
        grid=(b, SWA_KV // 2),
        in_specs=[pl.BlockSpec(memory_space=pltpu.SMEM),
                  pl.BlockSpec((None, T, 512), lambda b_, h: (b_, 0, qo + h)),
                  pl.BlockSpec((None, T, 256), lambda b_, h: (b_, 0, ko + h)),
                  pl.BlockSpec((None, T, 256), lambda b_, h: (b_, 0, vo + h)),
                  pl.BlockSpec((L, Q), lambda b_, h: (0, 0)),
                  pl.BlockSpec((L, Q), lambda b_, h: (0, 0)),
                  pl.BlockSpec((3, SWA_KWIN, SWA_QB), lambda b_, h: (0, 0, 0))],
        out_specs=pl.BlockSpec((None, T, 512), lambda b_, h: (b_, 0, h)),
        scratch_shapes=[pltpu.VMEM((L, 2 * Q), BF16),
                        pltpu.VMEM((2, HEAD_DIM, T), BF16),
                        pltpu.VMEM((SWA_KWIN + NC, SWA_STACK), F32),
                        pltpu.VMEM((SWA_KWIN + NC, SWA_STACK), F32),
                        pltpu.VMEM((SWA_KWIN + NC, SWA_STACK), BF16),
                        pltpu.VMEM((SWA_KWIN + NC, SWA_STACK), BF16),
                        pltpu.VMEM((8, SWA_STACK), F32),
                        pltpu.VMEM((8, SWA_STACK), F32)],
        compiler_params=_cparams(2),
        name="swa_attn",
    )(sink, p, p, p, cos_t, sin_t, mask_t)


def _merge_kernel(x_ref, ya_ref, yb_ref, yc_ref, g0_ref, g1_ref, g2_ref, mg_ref,
                  wa_ref, wb_ref, wc_ref, wo_ref, o_ref):
    m = (_sigmoid(g0_ref[...].astype(F32)) * _dot(ya_ref[...], wa_ref[...])
         + _sigmoid(g1_ref[...].astype(F32)) * _dot(yb_ref[...], wb_ref[...])
         + _sigmoid(g2_ref[...].astype(F32)) * _dot(yc_ref[...], wc_ref[...]))
    o_ref[...] = x_ref[...] + mg_ref[...] * _dot(m.astype(BF16), wo_ref[...])


def _const_spec(shape):
    return pl.BlockSpec(shape, lambda b, j: (0,) * len(shape))


def _merge(xs, ya, yb, yc, p, ms, wa, wb, wc, wo):
    b = xs.shape[0]
    go = C_GATE // D
    row = lambda w: pl.BlockSpec((None, TR, w), lambda b_, j: (b_, j, 0))
    gate = lambda k: pl.BlockSpec((None, TR, D), lambda b_, j: (b_, j, go + k))
    return pl.pallas_call(
        _merge_kernel,
        out_shape=jax.ShapeDtypeStruct((b, T, D), F32),
        grid=(b, NRT),
        in_specs=[row(D), row(SSD_INNER), row(D), row(D), gate(0), gate(1), gate(2), _mod_spec(2),
                  _const_spec((SSD_INNER, D)), _const_spec((D, D)), _const_spec((D, D)), _const_spec((D, D))],
        out_specs=row(D),
        compiler_params=_cparams(2),
        name="merge",
    )(xs, ya, yb, yc, p, p, p, ms, wa, wb, wc, wo)


FF_CH = 1024


def _ffn_kernel(*refs, emit_next):
    if emit_next:
        x_ref, g_ref, sh_ref, sc_ref, mg_ref, w1_ref, w2_ref, gn_ref, shn_ref, scn_ref, o_ref, h_ref = refs
    else:
        x_ref, g_ref, sh_ref, sc_ref, mg_ref, w1_ref, w2_ref, o_ref = refs
    x = x_ref[...]
    h = (_rms(x, g_ref[...]) * (1.0 + sc_ref[...]) + sh_ref[...]).astype(BF16)
    acc = None
    for kf in range(D_FF // FF_CH):
        a = jnp.maximum(_dot(h, w1_ref[:, kf * FF_CH:(kf + 1) * FF_CH]), 0.0)
        o = _dot((a * a).astype(BF16), w2_ref[kf * FF_CH:(kf + 1) * FF_CH, :])
        acc = o if acc is None else acc + o
    y = x + mg_ref[...] * acc
    o_ref[...] = y
    if emit_next:
        h_ref[...] = (_rms(y, gn_ref[...]) * (1.0 + scn_ref[...]) + shn_ref[...]).astype(h_ref.dtype)


def _ffn(xs, g, ms, w1, w2, g_next=None, ms_next=None):
    b = xs.shape[0]
    emit_next = g_next is not None
    row = pl.BlockSpec((None, TR, D), lambda b_, j: (b_, j, 0))
    in_specs = [row, _const_spec((1, D)), _mod_spec(3), _mod_spec(4), _mod_spec(5),
                _const_spec((D, D_FF)), _const_spec((D_FF, D))]
    args = [xs, g.reshape(1, D), ms, ms, ms, w1, w2]
    out_shape = jax.ShapeDtypeStruct((b, T, D), F32)
    out_specs = row
    if emit_next:
        in_specs += [_const_spec((1, D)), _mod_spec(0), _mod_spec(1)]
        args += [g_next.reshape(1, D), ms_next, ms_next]
        out_shape = [out_shape, jax.ShapeDtypeStruct((b, T, D), BF16)]
        out_specs = [row, row]
    return pl.pallas_call(
        functools.partial(_ffn_kernel, emit_next=emit_next),
        out_shape=out_shape,
        grid=(b, NRT),
        in_specs=in_specs,
        out_specs=out_specs,
        compiler_params=_cparams(2),
        name="ffn",
    )(*args)


def _final_kernel(x_ref, g_ref, o_ref):
    o_ref[...] = _rms(x_ref[...], g_ref[...])


def _final_norm(xs, g):
    b = xs.shape[0]
    return pl.pallas_call(
        _final_kernel,
        out_shape=jax.ShapeDtypeStruct((b, L, D), F32),
        grid=(b, L // TR),
        in_specs=[pl.BlockSpec((None, TR, D), lambda b_, j: (b_, j, 0)), _const_spec((1, D))],
        out_specs=pl.BlockSpec((None, TR, D), lambda b_, j: (b_, j, 0)),
        compiler_params=_cparams(2),
        name="final_norm",
    )(xs, g.reshape(1, D))


def _prep_w_in(w):
    def dup(a):
        a = a.reshape(D, SWA_KV, 1, HEAD_DIM)
        return jnp.concatenate([a, a], axis=2).reshape(D, SWA_KV * 128)
    wx = w[:, R_XBC:R_DT].astype(BF16)
    cat = jnp.concatenate([
        w[:, R_NAK:R_NAV], w[:, R_NAV:R_SWAK], dup(w[:, R_SWAK:R_SWAV]),
        dup(w[:, R_SWAV:R_Z]), w[:, R_Z:R_NAQ], w[:, R_NAQ:R_SWAQ], w[:, R_SWAQ:R_GATE],
        w[:, R_GATE:R_END]], axis=1).astype(BF16)
    wd = w[:, R_DT:R_NAK].reshape(D, 2, SSD_GROUPS, SSD_HPG).transpose(0, 2, 1, 3).reshape(D, 2 * SSD_HEADS)
    wdt = jnp.concatenate([wd, jnp.zeros((D, Q - 2 * SSD_HEADS), F32)], axis=1).astype(BF16)
    return wx, cat, wdt


def kernel(x, c, ctx, c_ctx, ada_w, ada_b, norm1_g, norm2_g, w_in, conv_w, conv_b, dt_bias, a_log, ssd_d,
           ssd_norm_g, na_rpb, swa_sink, w_o_ssd, w_o_na, w_o_swa, w_out, w_ff1, w_ff2, final_g):
    b = x.shape[0]
    depth = ada_w.shape[0]
    xs = jnp.concatenate([x, ctx], axis=1)
    nrow = -(-(b + 1) // 8) * 8
    cvec = jnp.concatenate([c, c_ctx[None, :], jnp.zeros((nrow - b - 1, D), F32)], axis=0)
    mod = _ada_mod(cvec, ada_w, ada_b)
    lat = mod[:, :b].reshape(depth, b, 1, N_MOD, 1, D)
    cx = jnp.broadcast_to(mod[:, b].reshape(depth, 1, 1, N_MOD, 1, D), lat.shape)
    ms_all = jnp.concatenate([lat, cx], axis=2)
    cos_t, sin_t = _rope_tables()
    mask_t = _swa_mask_table()
    na_bias = _na_bias_source(na_rpb)

    h = _normmod(xs, norm1_g[0], ms_all[0])
    for l in range(depth):
        ms = ms_all[l]
        wx, wcat, wdt = _prep_w_in(w_in[l])
        p = _matmul(h, wcat, BF16, 1024, "in_proj")
        dtr = _matmul(h, wdt, F32, Q, "dt_proj")
        xa = _proj_conv(h, wx, conv_w[l], conv_b[l])
        ya = _ssd(xa, p, dtr, dt_bias[l], a_log[l], ssd_d[l], ssd_norm_g[l])
        yb = _na(p, na_bias[l])
        yc = _swa(p, swa_sink[l], cos_t, sin_t, mask_t)
        xs = _merge(xs, ya, yb, yc, p, ms, w_o_ssd[l].astype(BF16), w_o_na[l].astype(BF16),
                    w_o_swa[l].astype(BF16), w_out[l].astype(BF16))
        w1, w2 = w_ff1[l].astype(BF16), w_ff2[l].astype(BF16)
        if l + 1 < depth:
            xs, h = _ffn(xs, norm2_g[l], ms, w1, w2, norm1_g[l + 1], ms_all[l + 1])
        else:
            xs = _ffn(xs, norm2_g[l], ms, w1, w2)
    return _final_norm(xs, final_g)
```

```python
import functools

import numpy as np
import jax
import jax.numpy as jnp
from jax import lax
from jax.experimental import pallas as pl
from jax.experimental.pallas import tpu as pltpu

F32 = jnp.float32
BF16 = jnp.bfloat16
HIGHEST = lax.Precision.HIGHEST

D = 1024
L = 2048
NC = 256
T = L + NC
GRID_W = 64
HEAD_DIM = 64
ROWS = L // GRID_W
EPS = 1e-6
N_MOD = 6

SSD_INNER = 2 * D
SSD_HEADS = 32
SSD_GROUPS = 8
SSD_HPG = 4
SSD_STATE = 128
SSD_XBC = SSD_INNER + 2 * SSD_GROUPS * SSD_STATE
Q = 128
NCH = T // Q

NA_HEADS = 16
NA_KH = 8
NA_KW = 16
NA_QROWS = 2
NA_KROWS = 10
NA_NQ = NA_QROWS * GRID_W
NA_NK = NA_KROWS * GRID_W
NA_CASES = 5
NA_NOFF = 2 * NA_KH - 1

SWA_HEADS = 16
SWA_KV = 4
SWA_W = 128
ROPE_BASE = 10000.0
D_FF = 4 * D

R_XBC, R_DT, R_NAK, R_NAV, R_SWAK, R_SWAV, R_Z, R_NAQ, R_SWAQ, R_GATE, R_END = (
    0, 4096, 4160, 5184, 6208, 6464, 6720, 8768, 9792, 10816, 13888)
C_XBC, C_NAK, C_NAV, C_SWAK, C_SWAV, C_Z, C_NAQ, C_SWAQ, C_GATE, NCAT = (
    0, 4096, 5120, 6144, 6656, 7168, 9216, 10240, 11264, 14336)

NEG = -1e30
LOG2E = float(np.log2(np.e))
QSCALE = 0.125 * LOG2E
VMEM_LIMIT = 48 * 1024 * 1024


def _cparams(n_axes):
    return pltpu.CompilerParams(dimension_semantics=("parallel",) * n_axes,
                                vmem_limit_bytes=VMEM_LIMIT)


def _sigmoid(x):
    return 0.5 * jnp.tanh(0.5 * x) + 0.5


def _silu(x):
    h = 0.5 * x
    return h * jnp.tanh(h) + h


def _softplus(x):
    return jnp.maximum(x, 0.0) + jnp.log1p(jnp.exp(-jnp.abs(x)))


def _nt(a, b):
    return lax.dot_general(a, b, (((1,), (1,)), ((), ())), preferred_element_type=F32)


def _tn(a, b):
    return lax.dot_general(a, b, (((0,), (0,)), ((), ())), preferred_element_type=F32)


def _dot(a, b):
    return jnp.dot(a, b, preferred_element_type=F32)


def _chunk_loop(n, unroll, body):
    def wrapped(c, carry):
        body(c, pl.multiple_of(c * Q, Q))
        return carry
    lax.fori_loop(0, n, wrapped, 0, unroll=unroll)


def _ada_kernel(c_ref, w_ref, b_ref, o_ref):
    c = c_ref[...]
    s = _silu(c).astype(BF16)
    o_ref[...] = _dot(s, w_ref[...].astype(BF16)) + b_ref[...]


def _ada_mod(cvec, ada_w, ada_b):
    depth = ada_w.shape[0]
    r = cvec.shape[0]
    return pl.pallas_call(
        _ada_kernel,
        out_shape=jax.ShapeDtypeStruct((depth, r, N_MOD * D), F32),
        grid=(depth, N_MOD),
        in_specs=[pl.BlockSpec((r, D), lambda l, n: (0, 0)),
                  pl.BlockSpec((None, D, D), lambda l, n: (l, 0, n)),
                  pl.BlockSpec((None, 1, D), lambda l, n: (l, 0, n))],
        out_specs=pl.BlockSpec((None, r, D), lambda l, n: (l, 0, n)),
        compiler_params=_cparams(2),
        name="ada_mod",
    )(cvec, ada_w, ada_b.reshape(depth, 1, N_MOD * D))


TR = 256
NRT = T // TR


def _mod_spec(which):
    return pl.BlockSpec((None, None, None, 1, D), lambda b, j: (b, j // (L // TR), which, 0, 0))


def _rms(x, g):
    return x * lax.rsqrt(jnp.mean(x * x, axis=-1, keepdims=True) + EPS) * g


def _normmod_kernel(x_ref, g_ref, sh_ref, sc_ref, o_ref):
    y = _rms(x_ref[...], g_ref[...])
    o_ref[...] = (y * (1.0 + sc_ref[...]) + sh_ref[...]).astype(o_ref.dtype)


def _normmod(xs, g, ms):
    b = xs.shape[0]
    return pl.pallas_call(
        _normmod_kernel,
        out_shape=jax.ShapeDtypeStruct((b, T, D), BF16),
        grid=(b, NRT),
        in_specs=[pl.BlockSpec((None, TR, D), lambda b, j: (b, j, 0)),
                  pl.BlockSpec((1, D), lambda b, j: (0, 0)),
                  _mod_spec(0), _mod_spec(1)],
        out_specs=pl.BlockSpec((None, TR, D), lambda b, j: (b, j, 0)),
        compiler_params=_cparams(2),
        name="normmod",
    )(xs, g.reshape(1, D), ms, ms)


def _mm_kernel(x_ref, w_ref, o_ref):
    o_ref[...] = _dot(x_ref[...], w_ref[...]).astype(o_ref.dtype)


def _matmul(x, w, out_dtype, tn, name):
    b, t, k = x.shape
    n = w.shape[1]
    return pl.pallas_call(
        _mm_kernel,
        out_shape=jax.ShapeDtypeStruct((b, t, n), out_dtype),
        grid=(n // tn, b),
        in_specs=[pl.BlockSpec((None, t, k), lambda n_, b_: (b_, 0, 0)),
                  pl.BlockSpec((k, tn), lambda n_, b_: (0, n_))],
        out_specs=pl.BlockSpec((None, t, tn), lambda n_, b_: (b_, 0, n_)),
        compiler_params=_cparams(2),
        name=name,
    )(x, w)


CW = 512
CONV_RC = 256
PAD_LAT = 8
PAD_CTX = 16 + L


def _conv_kernel(x_ref, w_ref, b_ref, o_ref, pad_ref):
    z8 = jnp.zeros((8, CW), F32)
    pad_ref[0:8, :] = z8
    pad_ref[PAD_LAT + L:PAD_CTX, :] = z8
    pad_ref[PAD_CTX + NC:PAD_CTX + NC + 8, :] = z8
    pad_ref[PAD_LAT:PAD_LAT + L, :] = x_ref[0:L, :].astype(F32)
    pad_ref[PAD_CTX:PAD_CTX + NC, :] = x_ref[L:T, :].astype(F32)
    w = w_ref[...]
    bias = b_ref[...]
    for c in range(T // CONV_RC):
        r0 = c * CONV_RC
        base = (PAD_LAT if r0 < L else PAD_CTX - L) + r0
        win = pad_ref[base - 8:base + CONV_RC + 8, :]
        acc = bias + win[8:8 + CONV_RC] * w[2:3, :]
        for k in (0, 1, 3, 4):
            shifted = pltpu.roll(win, (2 - k) % (CONV_RC + 16), axis=0)
            acc = acc + shifted[8:8 + CONV_RC] * w[k:k + 1, :]
        o_ref[r0:r0 + CONV_RC, :] = _silu(acc).astype(o_ref.dtype)


def _conv_silu(p, conv_w, conv_b):
    b = p.shape[0]
    w8 = jnp.concatenate([conv_w, jnp.zeros((3, SSD_XBC), F32)], axis=0)
    return pl.pallas_call(
        _conv_kernel,
        out_shape=jax.ShapeDtypeStruct((b, T, SSD_XBC), BF16),
        grid=(b, SSD_XBC // CW),
        in_specs=[pl.BlockSpec((None, T, CW), lambda b_, n: (b_, 0, n)),
                  pl.BlockSpec((8, CW), lambda b_, n: (0, n)),
                  pl.BlockSpec((1, CW), lambda b_, n: (0, n))],
        out_specs=pl.BlockSpec((None, T, CW), lambda b_, n: (b_, 0, n)),
        scratch_shapes=[pltpu.VMEM((T + 24, CW), F32)],
        compiler_params=_cparams(2),
        name="conv_silu",
    )(p, w8, conv_b.reshape(1, SSD_XBC))


GW = SSD_HPG * HEAD_DIM


def _split3(x):
    hi = x.astype(BF16)
    r = x - hi.astype(F32)
    mid = r.astype(BF16)
    lo = (r - mid.astype(F32)).astype(BF16)
    return hi, mid, lo


def _cum_constants():
    ii = lax.broadcasted_iota(jnp.int32, (Q, Q), 0)
    jj = lax.broadcasted_iota(jnp.int32, (Q, Q), 1)
    tril = jnp.where(ii >= jj, 1.0, 0.0).astype(BF16)
    triu = jnp.where(ii <= jj, 1.0, 0.0).astype(BF16)
    return jnp.concatenate([jnp.concatenate([tril] * 3, axis=1),
                            jnp.concatenate([triu] * 3, axis=1)], axis=0)


def _ssd_prep_kernel(dt_ref, bias_ref, alog_ref, accg_o, wts_o, eacc_o, rowt_o, dtt_o):
    lcat = _cum_constants()
    isf = (lax.broadcasted_iota(jnp.int32, (1, Q), 1) % 8) < 4
    bias = bias_ref[...]
    a_r = -jnp.exp(alog_ref[...])

    def body(c, t0):
        rows = pl.ds(t0, Q)
        dt = _softplus(dt_ref[rows, :] + bias)
        both = _dot(lcat, jnp.concatenate(_split3(dt * a_r), axis=0))
        acc = jnp.where(isf, both[0:Q], both[Q:2 * Q])
        tot = jnp.where(isf, both[Q - 1:Q], both[Q:Q + 1])
        for g in range(SSD_GROUPS):
            accg_o[g, rows, :] = acc if g == 0 else pltpu.roll(acc, Q - 8 * g, axis=1)
        wts_o[rows, :] = dt * jnp.exp(tot - acc)
        eacc_o[rows, :] = jnp.exp(acc)
        dt_t = dt.T
        rowt_o[:, rows] = acc.T - jnp.log(dt_t)
        dtt_o[:, rows] = dt_t

    _chunk_loop(NCH, 3, body)


def _ssd_kernel(x_ref, b_ref, c_ref, z_ref, acc_ref, wts_ref, eacc_ref, rowt_ref, dtt_ref, dsk_ref, ng_ref,
                o_ref, xw_s, ec_s, big_s, sin_s, cb_s, y_s, *, n_out):
    g = pl.program_id(1)
    ii = lax.broadcasted_iota(jnp.int32, (Q, Q), 0)
    jj = lax.broadcasted_iota(jnp.int32, (Q, Q), 1)
    low = ii > jj
    up = ii < jj
    lo = jj < HEAD_DIM
    ek = lax.broadcasted_iota(jnp.int32, (Q, 2 * GW), 0)
    ech = lax.broadcasted_iota(jnp.int32, (Q, 2 * GW), 1)
    esel = jnp.where(ek == g * 8 + ech // HEAD_DIM, 1.0, 0.0).astype(BF16)
    esel2 = jnp.concatenate([esel] * 2, axis=0)
    zero_x = jnp.zeros((Q, Q), BF16)
    rows = lambda t0: pl.ds(t0, Q)

    def stage_expand(c, t0):
        w2 = _split3(wts_ref[rows(t0), :])[:2]
        e2 = _split3(eacc_ref[rows(t0), :])[:2]
        lhs = jnp.concatenate([jnp.concatenate(w2, axis=1), jnp.concatenate(e2, axis=1)], axis=0)
        both = _dot(lhs, esel2)
        xs = x_ref[rows(t0), :].astype(F32)
        xw_s[rows(t0), :] = (jnp.concatenate([xs, xs], axis=1) * both[0:Q]).astype(BF16)
        ec_s[rows(t0), :] = both[Q:2 * Q]

    _chunk_loop(NCH, 3, stage_expand)

    def stage_upd(c, t0):
        big_s[c] = _tn(b_ref[rows(t0), :], xw_s[rows(t0), :])

    _chunk_loop(NCH, 6, stage_upd)

    s = jnp.zeros((SSD_STATE, GW), F32)
    for c in list(range(L // Q, NCH)) + list(range(L // Q)):
        sin_s[c, :, 0:GW] = s.astype(BF16)
        s = s * ec_s[c * Q + Q - 1:c * Q + Q, 0:GW] + big_s[c, :, 0:GW]
    s = jnp.zeros((SSD_STATE, GW), F32)
    for c in reversed(range(NCH)):
        sin_s[c, :, GW:2 * GW] = s.astype(BF16)
        s = s * ec_s[c * Q:c * Q + 1, GW:2 * GW] + big_s[c, :, GW:2 * GW]

    def stage_cb(c, t0):
        cc = c_ref[rows(t0), :]
        cb_s[rows(t0), :] = _nt(cc, b_ref[rows(t0), :])
        big_s[c] = _dot(cc, sin_s[c])

    _chunk_loop(n_out, 6 if n_out % 6 == 0 else 4, stage_cb)

    dsk = dsk_ref[...]
    ng = ng_ref[...]

    def colb(v, k):
        return jnp.broadcast_to(v[:, k:k + 1], (Q, Q))

    def stage_y(c, t0):
        xb = x_ref[rows(t0), :]
        acc = acc_ref[rows(t0), :]
        rowt = rowt_ref[:, rows(t0)]
        dt_t = dtt_ref[:, rows(t0)]
        ec = ec_s[rows(t0), :]
        cb = cb_s[rows(t0), :]
        yoff = big_s[c]
        ys = []
        for pr in range(2):
            ms = []
            for hh in range(2):
                r = 2 * pr + hh
                diag = jnp.log(dt_t[r:r + 1, :] + dt_t[4 + r:5 + r, :])
                seg = jnp.where(low, colb(acc, r) - rowt[r:r + 1, :],
                                jnp.where(up, colb(acc, 4 + r) - rowt[4 + r:5 + r, :], diag))
                ms.append((cb * jnp.exp(seg)).astype(BF16))
            xp = xb[:, pr * Q:(pr + 1) * Q]
            xstack = jnp.concatenate([jnp.where(lo, xp, zero_x), jnp.where(lo, zero_x, xp)], axis=0)
            y = _dot(jnp.concatenate(ms, axis=1), xstack)
            y = (y + ec[:, pr * Q:(pr + 1) * Q] * yoff[:, pr * Q:(pr + 1) * Q]
                 + ec[:, GW + pr * Q:GW + (pr + 1) * Q] * yoff[:, GW + pr * Q:GW + (pr + 1) * Q])
            ys.append(y)
        y_s[rows(t0), :] = jnp.concatenate(ys, axis=1)

    _chunk_loop(n_out, 2, stage_y)

    def stage_out(c, t0):
        y = y_s[rows(t0), :] + dsk * x_ref[rows(t0), :].astype(F32)
        zf = z_ref[rows(t0), :].astype(F32)
        u = y * _silu(zf)
        o_ref[rows(t0), :] = _rms(u, ng).astype(o_ref.dtype)

    _chunk_loop(n_out, 3 if n_out % 3 == 0 else 4, stage_out)


def _slot_order(v):
    s = v.reshape(2, SSD_GROUPS, SSD_HPG).transpose(1, 0, 2).reshape(1, 2 * SSD_HEADS)
    return jnp.pad(s, ((0, 0), (0, Q - 2 * SSD_HEADS)))


def _ssd(xa, p, dtr, dt_bias, a_log, ssd_d, ssd_norm_g, emit_ctx):
    b = xa.shape[0]
    t_out = T if emit_ctx else L
    row = pl.BlockSpec((None, T, Q), lambda b_: (b_, 0, 0))
    col = pl.BlockSpec((None, Q, T), lambda b_: (b_, 0, 0))
    one = pl.BlockSpec((1, Q), lambda b_: (0, 0))
    accg, wts, eacc, rowt, dtt = pl.pallas_call(
        _ssd_prep_kernel,
        out_shape=([jax.ShapeDtypeStruct((b, SSD_GROUPS, T, Q), F32)] + [jax.ShapeDtypeStruct((b, T, Q), F32)] * 2
                   + [jax.ShapeDtypeStruct((b, Q, T), F32)] * 2),
        grid=(b,),
        in_specs=[row, one, one],
        out_specs=[pl.BlockSpec((None, SSD_GROUPS, T, Q), lambda b_: (b_, 0, 0, 0)), row, row, col, col],
        compiler_params=_cparams(1),
        name="ssd_prep",
    )(dtr, _slot_order(dt_bias), _slot_order(a_log))

    dsk = jnp.repeat(ssd_d, HEAD_DIM).reshape(1, SSD_INNER)
    ng = ssd_norm_g.reshape(1, SSD_INNER)
    xoff, boff, coff, zoff = 0, SSD_INNER // Q, (SSD_INNER + SSD_GROUPS * SSD_STATE) // Q, C_Z // GW
    full = pl.BlockSpec((None, T, Q), lambda b_, g: (b_, 0, 0))
    slot = pl.BlockSpec((None, 8, T), lambda b_, g: (b_, g, 0))
    return pl.pallas_call(
        functools.partial(_ssd_kernel, n_out=t_out // Q),
        out_shape=jax.ShapeDtypeStruct((b, t_out, SSD_INNER), BF16),
        grid=(b, SSD_GROUPS),
        in_specs=[pl.BlockSpec((None, T, GW), lambda b_, g: (b_, 0, xoff + g)),
                  pl.BlockSpec((None, T, Q), lambda b_, g: (b_, 0, boff + g)),
                  pl.BlockSpec((None, T, Q), lambda b_, g: (b_, 0, coff + g)),
                  pl.BlockSpec((None, T, GW), lambda b_, g: (b_, 0, zoff + g)),
                  pl.BlockSpec((None, None, T, Q), lambda b_, g: (b_, g, 0, 0)), full, full, slot, slot,
                  pl.BlockSpec((1, GW), lambda b_, g: (0, g)),
                  pl.BlockSpec((1, GW), lambda b_, g: (0, g))],
        out_specs=pl.BlockSpec((None, t_out, GW), lambda b_, g: (b_, 0, g)),
        scratch_shapes=[pltpu.VMEM((T, 2 * GW), BF16),
                        pltpu.VMEM((T, 2 * GW), F32),
                        pltpu.VMEM((NCH, SSD_STATE, 2 * GW), F32),
                        pltpu.VMEM((NCH, SSD_STATE, 2 * GW), BF16),
                        pltpu.VMEM((T, Q), F32),
                        pltpu.VMEM((T, GW), F32)],
        compiler_params=_cparams(2),
        name="ssd",
    )(xa, xa, xa, p, accg, wts, eacc, rowt, dtt, dsk, ng)


def _softmax_t(parts, extra=None):
    mx = None
    for s in parts:
        r = jnp.max(s, axis=0, keepdims=True)
        mx = r if mx is None else jnp.maximum(mx, r)
    if extra is not None:
        mx = jnp.maximum(mx, extra)
    den = None
    es = []
    for s in parts:
        e = jnp.exp2(s - mx)
        d = jnp.sum(e, axis=0, keepdims=True)
        den = d if den is None else den + d
        es.append(e.astype(BF16))
    if extra is not None:
        den = den + jnp.exp2(extra - mx)
    return es, 1.0 / den


def _pv_t(vts, es):
    out = None
    for vt, e in zip(vts, es):
        o = _dot(vt, e)
        out = o if out is None else out + o
    return out


def _na_row_offsets():
    table = []
    for j in (0, 1, 2, ROWS // NA_QROWS - 2, ROWS // NA_QROWS - 1):
        start = int(np.clip(NA_QROWS * j - NA_KH // 2, 0, ROWS - NA_KROWS))
        per_row = []
        for i in range(NA_KROWS):
            kr = start + i
            offs = []
            for qr in range(NA_QROWS):
                r = NA_QROWS * j + qr
                rs = int(np.clip(r - NA_KH // 2, 0, ROWS - NA_KH))
                offs.append(kr - r + NA_KH - 1 if rs <= kr < rs + NA_KH else NA_NOFF)
            per_row.append(tuple(offs))
        table.append(per_row)
    return table


def _na_kernel(q_ref, k_ref, v_ref, src_ref, o_ref, vt_ref, bias_ref, sa_ref, sb_ref, pa_ref, pb_ref, ra_ref,
               rb_ref, *, emit_ctx):
    @pl.when(pl.program_id(1) == 0)
    def _():
        lo_tile = lax.broadcasted_iota(jnp.int32, (GRID_W, Q), 1) < GRID_W
        for ci, per_row in enumerate(_na_row_offsets()):
            for i, (off0, off1) in enumerate(per_row):
                for hh in range(2):
                    bias_ref[ci, i * GRID_W:(i + 1) * GRID_W, hh * Q:(hh + 1) * Q] = jnp.where(
                        lo_tile, src_ref[hh, off0], src_ref[hh, off1])

    lane = lax.broadcasted_iota(jnp.int32, (NA_NQ, Q), 1)
    lo = lane < HEAD_DIM
    top = lax.broadcasted_iota(jnp.int32, (Q, NA_NQ), 0) < HEAD_DIM
    zero = jnp.zeros((NA_NQ, Q), BF16)

    def vt_body(c, t0):
        vt_ref[:, pl.ds(t0, Q)] = v_ref[pl.ds(t0, Q), :].astype(F32).T.astype(BF16)

    _chunk_loop(NCH, 6, vt_body)

    kc = k_ref[L:T, :]
    vtc = vt_ref[:, L:T]

    def qpair(q0):
        q = (q_ref[pl.ds(q0, NA_NQ), :].astype(F32) * QSCALE).astype(BF16)
        return jnp.concatenate([jnp.where(lo, q, zero), jnp.where(lo, zero, q)], axis=0)

    def finish(q0, ot, rden):
        ot = ot * rden
        w = jnp.where(top, ot[:, 0:NA_NQ], ot[:, NA_NQ:2 * NA_NQ])
        o_ref[pl.ds(q0, NA_NQ), :] = w.T.astype(o_ref.dtype)

    def block_params(j):
        q0 = pl.multiple_of(j * NA_NQ, NA_NQ)
        start = jnp.clip(NA_QROWS * j - NA_KH // 2, 0, ROWS - NA_KROWS)
        koff = pl.multiple_of(start * GRID_W, 2 * GRID_W)
        last = ROWS // NA_QROWS - 1
        case = (jnp.where(j >= 1, 1, 0) + jnp.where(j >= 2, 1, 0)
                + jnp.where(j >= last - 1, 1, 0) + jnp.where(j >= last, 1, 0))
        return q0, koff, case

    def qk(j, s_ref):
        q0, koff, case = block_params(j)
        qp = qpair(q0)
        s_ref[0:NA_NK, :] = _nt(k_ref[pl.ds(koff, NA_NK), :], qp) + bias_ref[case]
        s_ref[NA_NK:NA_NK + NC, :] = _nt(kc, qp)

    def soft(s_ref, p_ref, r_ref):
        s = s_ref[...]
        e = jnp.exp2(s - jnp.max(s, axis=0, keepdims=True))
        r_ref[...] = jnp.broadcast_to(1.0 / jnp.sum(e, axis=0, keepdims=True), r_ref.shape)
        p_ref[...] = e.astype(BF16)

    def pv(j, p_ref, r_ref):
        q0, koff, _ = block_params(j)
        ot = (_dot(vt_ref[:, pl.ds(koff, NA_NK)], p_ref[0:NA_NK, :])
              + _dot(vtc, p_ref[NA_NK:NA_NK + NC, :]))
        finish(q0, ot, r_ref[0:1, :])

    nblk = L // NA_NQ
    qk(0, sa_ref)
    qk(1, sb_ref)
    soft(sa_ref, pa_ref, ra_ref)

    def body(m, carry):
        pv(2 * m - 2, pa_ref, ra_ref)
        qk(2 * m, sa_ref)
        soft(sb_ref, pb_ref, rb_ref)
        pv(2 * m - 1, pb_ref, rb_ref)
        qk(2 * m + 1, sb_ref)
        soft(sa_ref, pa_ref, ra_ref)
        return carry

    lax.fori_loop(1, nblk // 2, body, 0)
    pv(nblk - 2, pa_ref, ra_ref)
    soft(sb_ref, pb_ref, rb_ref)
    pv(nblk - 1, pb_ref, rb_ref)

    if emit_ctx:
        ctx_blocks = []
        for cbk in range(NC // NA_NQ):
            q0 = L + cbk * NA_NQ
            ctx_blocks.append((q0, _nt(kc, qpair(q0))))
        ctx_soft = [(q0,) + _softmax_t([s]) for q0, s in ctx_blocks]
        for q0, es, rden in ctx_soft:
            finish(q0, _pv_t([vtc], es), rden)


def _na_bias_source(rpb_all):
    depth = rpb_all.shape[0]
    ck = np.arange(GRID_W)[:, None]
    cq = (np.arange(Q) % GRID_W)[None, :]
    col_oh = ((ck - cq + NA_KW - 1)[None] == np.arange(2 * NA_KW - 1)[:, None, None]).astype(np.float32)
    cs = np.clip(cq - NA_KW // 2, 0, GRID_W - NA_KW)
    col_ok = (ck >= cs) & (ck < cs + NA_KW)
    rp = rpb_all.reshape(depth, NA_HEADS // 2, 2, NA_NOFF, 2 * NA_KW - 1)
    src = jnp.einsum("lperd,dbz->lperbz", rp, col_oh, precision=HIGHEST)
    src = jnp.where(col_ok, src * LOG2E, NEG)
    return jnp.concatenate([src, jnp.full_like(src[:, :, :, :1], NEG)], axis=3)


def _na(p, bias_src, emit_ctx):
    b = p.shape[0]
    t_out = T if emit_ctx else L
    qo, ko, vo = C_NAQ // Q, C_NAK // Q, C_NAV // Q
    return pl.pallas_call(
        functools.partial(_na_kernel, emit_ctx=emit_ctx),
        out_shape=jax.ShapeDtypeStruct((b, t_out, D), BF16),
        grid=(NA_HEADS // 2, b),
        in_specs=[pl.BlockSpec((None, T, Q), lambda h, b_: (b_, 0, qo + h)),
                  pl.BlockSpec((None, T, Q), lambda h, b_: (b_, 0, ko + h)),
                  pl.BlockSpec((None, T, Q), lambda h, b_: (b_, 0, vo + h)),
                  pl.BlockSpec((None, 2, NA_NOFF + 1, GRID_W, Q), lambda h, b_: (h, 0, 0, 0, 0))],
        out_specs=pl.BlockSpec((None, t_out, Q), lambda h, b_: (b_, 0, h)),
        scratch_shapes=[pltpu.VMEM((Q, T), BF16),
                        pltpu.VMEM((NA_CASES, NA_NK, 2 * NA_NQ), F32),
                        pltpu.VMEM((NA_NK + NC, 2 * NA_NQ), F32),
                        pltpu.VMEM((NA_NK + NC, 2 * NA_NQ), F32),
                        pltpu.VMEM((NA_NK + NC, 2 * NA_NQ), BF16),
                        pltpu.VMEM((NA_NK + NC, 2 * NA_NQ), BF16),
                        pltpu.VMEM((8, 2 * NA_NQ), F32),
                        pltpu.VMEM((8, 2 * NA_NQ), F32)],
        compiler_params=pltpu.CompilerParams(dimension_semantics=("parallel", "arbitrary"),
                                             vmem_limit_bytes=VMEM_LIMIT),
        name="na_attn",
    )(p, p, p, bias_src)


SWA_KWIN = 3 * SWA_W
SWA_QB = 128
SWA_STACK = 4 * SWA_QB
ROPE_RC = 256


def _swap_rotary_halves(t, lane):
    quarter = HEAD_DIM // 4
    a = pltpu.roll(t, Q - quarter, axis=1)
    b = pltpu.roll(t, quarter, axis=1)
    return jnp.where((lane % (2 * quarter)) < quarter, a, b)


def _swa_kernel(sink_ref, q_ref, k_ref, v_ref, cos_ref, sin_ref, mask_ref, o_ref, krot_ref, vt_ref,
                sa_ref, sb_ref, pa_ref, pb_ref, ra_ref, rb_ref, *, emit_ctx):
    kp = pl.program_id(1)
    lane = lax.broadcasted_iota(jnp.int32, (SWA_QB, Q), 1)
    lo = lane < HEAD_DIM
    zero = jnp.zeros((SWA_QB, Q), BF16)
    col = lax.broadcasted_iota(jnp.int32, (1, SWA_STACK), 1)
    eye4 = jnp.where(lax.broadcasted_iota(jnp.int32, (SWA_STACK, Q), 0) % SWA_QB
                     == lax.broadcasted_iota(jnp.int32, (SWA_STACK, Q), 1), 1.0, 0.0).astype(BF16)

    def rope(t, r0, n):
        lane_n = lax.broadcasted_iota(jnp.int32, (n, Q), 1)
        return t * cos_ref[pl.ds(r0, n), :] + _swap_rotary_halves(t, lane_n) * sin_ref[pl.ds(r0, n), :]

    def krot_body(i, carry):
        r0 = pl.multiple_of(i * ROPE_RC, ROPE_RC)
        for gl in range(2):
            t = k_ref[pl.ds(r0, ROPE_RC), gl * Q:(gl + 1) * Q].astype(F32)
            krot_ref[pl.ds(r0, ROPE_RC), gl * Q:(gl + 1) * Q] = rope(t, r0, ROPE_RC).astype(BF16)
        return carry

    lax.fori_loop(0, L // ROPE_RC, krot_body, 0, unroll=2)

    def vt_body(c, t0):
        for gl in range(2):
            vt = v_ref[pl.ds(t0, Q), gl * Q:(gl + 1) * Q].astype(F32).T
            vt_ref[gl, :, pl.ds(t0, Q)] = vt[0:HEAD_DIM].astype(BF16)

    _chunk_loop(NCH, 6, vt_body)

    def stack_q(qa, qb):
        return jnp.concatenate([jnp.where(lo, qa, zero), jnp.where(lo, zero, qa),
                                jnp.where(lo, qb, zero), jnp.where(lo, zero, qb)], axis=0)

    def sink_row(gl):
        base = kp * 8 + gl * 4
        row = jnp.where(col < SWA_QB, sink_ref[base],
                        jnp.where(col < 2 * SWA_QB, sink_ref[base + 1],
                                  jnp.where(col < 3 * SWA_QB, sink_ref[base + 2], sink_ref[base + 3])))
        return row * LOG2E

    def finish(q0, gl, ot, rden):
        ot = ot * rden
        pa = jnp.concatenate([ot[:, 0:SWA_QB], ot[:, SWA_QB:2 * SWA_QB]], axis=0).T
        pb = jnp.concatenate([ot[:, 2 * SWA_QB:3 * SWA_QB], ot[:, 3 * SWA_QB:4 * SWA_QB]], axis=0).T
        o_ref[pl.ds(q0, SWA_QB), gl * 2 * Q:(gl + 1) * 2 * Q] = jnp.concatenate([pa, pb], axis=1).astype(o_ref.dtype)

    def block_params(i):
        q0 = pl.multiple_of(i * SWA_QB, SWA_QB)
        ws = pl.multiple_of(jnp.clip((i - 1) * SWA_W, 0, L - SWA_KWIN), SWA_W)
        case = jnp.where(i >= 1, 1, 0) + jnp.where(i >= L // SWA_QB - 1, 1, 0)
        return q0, ws, case

    def qk(i, gl, s_ref):
        q0, ws, case = block_params(i)
        qa = rope(q_ref[pl.ds(q0, SWA_QB), gl * 2 * Q:gl * 2 * Q + Q].astype(F32), q0, SWA_QB) * QSCALE
        qb = rope(q_ref[pl.ds(q0, SWA_QB), gl * 2 * Q + Q:(gl + 1) * 2 * Q].astype(F32), q0, SWA_QB) * QSCALE
        qs = stack_q(qa.astype(BF16), qb.astype(BF16))
        a_loc = jnp.concatenate([krot_ref[pl.ds(ws, SWA_KWIN), gl * Q:(gl + 1) * Q], mask_ref[case]], axis=1)
        s_ref[0:SWA_KWIN, :] = _nt(a_loc, jnp.concatenate([qs, eye4], axis=1))
        s_ref[SWA_KWIN:SWA_KWIN + NC, :] = _nt(k_ref[L:T, gl * Q:(gl + 1) * Q], qs)

    def soft(gl, s_ref, p_ref, r_ref):
        s = s_ref[...]
        sink = sink_row(gl)
        mx = jnp.maximum(jnp.max(s, axis=0, keepdims=True), sink)
        e = jnp.exp2(s - mx)
        den = jnp.sum(e, axis=0, keepdims=True) + jnp.exp2(sink - mx)
        r_ref[...] = jnp.broadcast_to(1.0 / den, r_ref.shape)
        p_ref[...] = e.astype(BF16)

    def pv(i, gl, p_ref, r_ref):
        q0, ws, _ = block_params(i)
        ot = (_dot(vt_ref[gl, :, pl.ds(ws, SWA_KWIN)], p_ref[0:SWA_KWIN, :])
              + _dot(vt_ref[gl, :, L:T], p_ref[SWA_KWIN:SWA_KWIN + NC, :]))
        finish(q0, gl, ot, r_ref[0:1, :])

    nblk = L // SWA_QB
    qk(0, 0, sa_ref)
    qk(0, 1, sb_ref)
    soft(0, sa_ref, pa_ref, ra_ref)

    def body(i, carry):
        pv(i - 1, 0, pa_ref, ra_ref)
        qk(i, 0, sa_ref)
        soft(1, sb_ref, pb_ref, rb_ref)
        pv(i - 1, 1, pb_ref, rb_ref)
        qk(i, 1, sb_ref)
        soft(0, sa_ref, pa_ref, ra_ref)
        return carry

    lax.fori_loop(1, nblk, body, 0)
    pv(nblk - 1, 0, pa_ref, ra_ref)
    soft(1, sb_ref, pb_ref, rb_ref)
    pv(nblk - 1, 1, pb_ref, rb_ref)

    for cbk in range(NC // SWA_QB if emit_ctx else 0):
        q0 = L + cbk * SWA_QB
        scores = []
        for gl in range(2):
            qa = q_ref[q0:q0 + SWA_QB, gl * 2 * Q:gl * 2 * Q + Q].astype(F32) * QSCALE
            qb = q_ref[q0:q0 + SWA_QB, gl * 2 * Q + Q:(gl + 1) * 2 * Q].astype(F32) * QSCALE
            scores.append(_nt(k_ref[L:T, gl * Q:(gl + 1) * Q], stack_q(qa.astype(BF16), qb.astype(BF16))))
        soft = [_softmax_t([s], sink_row(gl)) for gl, s in enumerate(scores)]
        for gl, (es, rden) in enumerate(soft):
            finish(q0, gl, _pv_t([vt_ref[gl, :, L:T]], es), rden)


def _rope_tables():
    pos = np.arange(L)
    quarter = HEAD_DIM // 4
    lane = np.arange(Q) % HEAD_DIM
    inv = ROPE_BASE ** (-(lane % quarter).astype(np.float64) / quarter)
    p = np.where(lane[None, :] < HEAD_DIM // 2, (pos // GRID_W)[:, None], (pos % GRID_W)[:, None]).astype(np.float64)
    ang = p * inv[None, :]
    sign = np.where((lane % (2 * quarter)) < quarter, -1.0, 1.0)[None, :]
    return jnp.asarray(np.cos(ang), F32), jnp.asarray(np.sin(ang) * sign, F32)


def _swa_mask_table():
    key = np.arange(SWA_KWIN)[:, None]
    qry = np.arange(SWA_QB)[None, :]
    tabs = [np.where(np.abs(key - qry + delta) <= SWA_W, 0.0, NEG) for delta in (0, -SWA_W, -2 * SWA_W)]
    return jnp.asarray(np.stack(tabs), BF16)


def _swa(p, sink, cos_t, sin_t, mask_t, emit_ctx):
    b = p.shape[0]
    t_out = T if emit_ctx else L
    qo, ko, vo = C_SWAQ // 512, C_SWAK // 256, C_SWAV // 256
    return pl.pallas_call(
        functools.partial(_swa_kernel, emit_ctx=emit_ctx),
        out_shape=jax.ShapeDtypeStruct((b, t_out, D), BF16),
        grid=(b, SWA_KV // 2),
        in_specs=[pl.BlockSpec(memory_space=pltpu.SMEM),
                  pl.BlockSpec((None, T, 512), lambda b_, h: (b_, 0, qo + h)),
                  pl.BlockSpec((None, T, 256), lambda b_, h: (b_, 0, ko + h)),
                  pl.BlockSpec((None, T, 256), lambda b_, h: (b_, 0, vo + h)),
                  pl.BlockSpec((L, Q), lambda b_, h: (0, 0)),
                  pl.BlockSpec((L, Q), lambda b_, h: (0, 0)),
                  pl.BlockSpec((3, SWA_KWIN, SWA_QB), lambda b_, h: (0, 0, 0))],
        out_specs=pl.BlockSpec((None, t_out, 512), lambda b_, h: (b_, 0, h)),
        scratch_shapes=[pltpu.VMEM((L, 2 * Q), BF16),
                        pltpu.VMEM((2, HEAD_DIM, T), BF16),
                        pltpu.VMEM((SWA_KWIN + NC, SWA_STACK), F32),
                        pltpu.VMEM((SWA_KWIN + NC, SWA_STACK), F32),
                        pltpu.VMEM((SWA_KWIN + NC, SWA_STACK), BF16),
                        pltpu.VMEM((SWA_KWIN + NC, SWA_STACK), BF16),
                        pltpu.VMEM((8, SWA_STACK), F32),
                        pltpu.VMEM((8, SWA_STACK), F32)],
        compiler_params=_cparams(2),
        name="swa_attn",
    )(sink, p, p, p, cos_t, sin_t, mask_t)


def _merge_kernel(x_ref, ya_ref, yb_ref, yc_ref, g0_ref, g1_ref, g2_ref, mg_ref,
                  wa_ref, wb_ref, wc_ref, wo_ref, o_ref):
    m = (_sigmoid(g0_ref[...].astype(F32)) * _dot(ya_ref[...], wa_ref[...])
         + _sigmoid(g1_ref[...].astype(F32)) * _dot(yb_ref[...], wb_ref[...])
         + _sigmoid(g2_ref[...].astype(F32)) * _dot(yc_ref[...], wc_ref[...]))
    o_ref[...] = x_ref[...] + mg_ref[...] * _dot(m.astype(BF16), wo_ref[...])


def _const_spec(shape):
    return pl.BlockSpec(shape, lambda b, j: (0,) * len(shape))


def _merge(xs, ya, yb, yc, p, ms, wa, wb, wc, wo, n_tiles):
    b = xs.shape[0]
    go = C_GATE // D
    row = lambda w: pl.BlockSpec((None, TR, w), lambda b_, j: (b_, j, 0))
    gate = lambda k: pl.BlockSpec((None, TR, D), lambda b_, j: (b_, j, go + k))
    return pl.pallas_call(
        _merge_kernel,
        out_shape=jax.ShapeDtypeStruct((b, n_tiles * TR, D), F32),
        grid=(b, n_tiles),
        in_specs=[row(D), row(SSD_INNER), row(D), row(D), gate(0), gate(1), gate(2), _mod_spec(2),
                  _const_spec((SSD_INNER, D)), _const_spec((D, D)), _const_spec((D, D)), _const_spec((D, D))],
        out_specs=row(D),
        compiler_params=_cparams(2),
        name="merge",
    )(xs, ya, yb, yc, p, p, p, ms, wa, wb, wc, wo)


FF_CH = 1024


def _ffn_kernel(*refs, emit_next):
    if emit_next:
        x_ref, g_ref, sh_ref, sc_ref, mg_ref, w1_ref, w2_ref, gn_ref, shn_ref, scn_ref, o_ref, h_ref = refs
    else:
        x_ref, g_ref, sh_ref, sc_ref, mg_ref, w1_ref, w2_ref, o_ref = refs
    x = x_ref[...]
    h = (_rms(x, g_ref[...]) * (1.0 + sc_ref[...]) + sh_ref[...]).astype(BF16)
    acc = None
    for kf in range(D_FF // FF_CH):
        a = jnp.maximum(_dot(h, w1_ref[:, kf * FF_CH:(kf + 1) * FF_CH]), 0.0)
        o = _dot((a * a).astype(BF16), w2_ref[kf * FF_CH:(kf + 1) * FF_CH, :])
        acc = o if acc is None else acc + o
    y = x + mg_ref[...] * acc
    o_ref[...] = y
    if emit_next:
        h_ref[...] = (_rms(y, gn_ref[...]) * (1.0 + scn_ref[...]) + shn_ref[...]).astype(h_ref.dtype)


def _ffn(xs, g, ms, w1, w2, n_tiles, g_next=None, ms_next=None):
    b = xs.shape[0]
    emit_next = g_next is not None
    row = pl.BlockSpec((None, TR, D), lambda b_, j: (b_, j, 0))
    in_specs = [row, _const_spec((1, D)), _mod_spec(3), _mod_spec(4), _mod_spec(5),
                _const_spec((D, D_FF)), _const_spec((D_FF, D))]
    args = [xs, g.reshape(1, D), ms, ms, ms, w1, w2]
    out_shape = jax.ShapeDtypeStruct((b, n_tiles * TR, D), F32)
    out_specs = row
    if emit_next:
        in_specs += [_const_spec((1, D)), _mod_spec(0), _mod_spec(1)]
        args += [g_next.reshape(1, D), ms_next, ms_next]
        out_shape = [out_shape, jax.ShapeDtypeStruct((b, n_tiles * TR, D), BF16)]
        out_specs = [row, row]
    return pl.pallas_call(
        functools.partial(_ffn_kernel, emit_next=emit_next),
        out_shape=out_shape,
        grid=(b, n_tiles),
        in_specs=in_specs,
        out_specs=out_specs,
        compiler_params=_cparams(2),
        name="ffn",
    )(*args)


def _final_kernel(x_ref, g_ref, o_ref):
    o_ref[...] = _rms(x_ref[...], g_ref[...])


def _final_norm(xs, g):
    b = xs.shape[0]
    return pl.pallas_call(
        _final_kernel,
        out_shape=jax.ShapeDtypeStruct((b, L, D), F32),
        grid=(b, L // TR),
        in_specs=[pl.BlockSpec((None, TR, D), lambda b_, j: (b_, j, 0)), _const_spec((1, D))],
        out_specs=pl.BlockSpec((None, TR, D), lambda b_, j: (b_, j, 0)),
        compiler_params=_cparams(2),
        name="final_norm",
    )(xs, g.reshape(1, D))


def _prep_w_in(w):
    def dup(a):
        a = a.reshape(D, SWA_KV, 1, HEAD_DIM)
        return jnp.concatenate([a, a], axis=2).reshape(D, SWA_KV * 128)
    cat = jnp.concatenate([
        w[:, R_XBC:R_DT], w[:, R_NAK:R_NAV], w[:, R_NAV:R_SWAK], dup(w[:, R_SWAK:R_SWAV]),
        dup(w[:, R_SWAV:R_Z]), w[:, R_Z:R_NAQ], w[:, R_NAQ:R_SWAQ], w[:, R_SWAQ:R_GATE],
        w[:, R_GATE:R_END]], axis=1).astype(BF16)
    wd = w[:, R_DT:R_NAK].reshape(D, 2, SSD_GROUPS, SSD_HPG).transpose(0, 2, 1, 3).reshape(D, 2 * SSD_HEADS)
    wdt = jnp.concatenate([wd, jnp.zeros((D, Q - 2 * SSD_HEADS), F32)], axis=1).astype(BF16)
    return cat, wdt


def kernel(x, c, ctx, c_ctx, ada_w, ada_b, norm1_g, norm2_g, w_in, conv_w, conv_b, dt_bias, a_log, ssd_d,
           ssd_norm_g, na_rpb, swa_sink, w_o_ssd, w_o_na, w_o_swa, w_out, w_ff1, w_ff2, final_g):
    b = x.shape[0]
    depth = ada_w.shape[0]
    xs = jnp.concatenate([x, ctx], axis=1)
    nrow = -(-(b + 1) // 8) * 8
    cvec = jnp.concatenate([c, c_ctx[None, :], jnp.zeros((nrow - b - 1, D), F32)], axis=0)
    mod = _ada_mod(cvec, ada_w, ada_b)
    lat = mod[:, :b].reshape(depth, b, 1, N_MOD, 1, D)
    cx = jnp.broadcast_to(mod[:, b].reshape(depth, 1, 1, N_MOD, 1, D), lat.shape)
    ms_all = jnp.concatenate([lat, cx], axis=2)
    cos_t, sin_t = _rope_tables()
    mask_t = _swa_mask_table()
    na_bias = _na_bias_source(na_rpb)

    h = _normmod(xs, norm1_g[0], ms_all[0])
    for l in range(depth):
        last = l + 1 == depth
        ms = ms_all[l]
        wcat, wdt = _prep_w_in(w_in[l])
        p = _matmul(h, wcat, BF16, 1024, "in_proj")
        dtr = _matmul(h, wdt, F32, Q, "dt_proj")
        xa = _conv_silu(p, conv_w[l], conv_b[l])
        ya = _ssd(xa, p, dtr, dt_bias[l], a_log[l], ssd_d[l], ssd_norm_g[l], not last)
        yb = _na(p, na_bias[l], not last)
        yc = _swa(p, swa_sink[l], cos_t, sin_t, mask_t, not last)
        n_tiles = L // TR if last else NRT
        xs = _merge(xs, ya, yb, yc, p, ms, w_o_ssd[l].astype(BF16), w_o_na[l].astype(BF16),
                    w_o_swa[l].astype(BF16), w_out[l].astype(BF16), n_tiles)
        w1, w2 = w_ff1[l].astype(BF16), w_ff2[l].astype(BF16)
        if last:
            xs = _ffn(xs, norm2_g[l], ms, w1, w2, n_tiles)
        else:
            xs, h = _ffn(xs, norm2_g[l], ms, w1, w2, n_tiles, norm1_g[l + 1], ms_all[l + 1])
    return _final_norm(xs, final_g)
```

```python
import functools

import numpy as np
import jax
import jax.numpy as jnp
from jax import lax
from jax.experimental import pallas as pl
from jax.experimental.pallas import tpu as pltpu

F32 = jnp.float32
BF16 = jnp.bfloat16
HIGHEST = lax.Precision.HIGHEST

D = 1024
L = 2048
NC = 256
T = L + NC
GRID_W = 64
HEAD_DIM = 64
ROWS = L // GRID_W
EPS = 1e-6
N_MOD = 6

SSD_INNER = 2 * D
SSD_HEADS = 32
SSD_GROUPS = 8
SSD_HPG = 4
SSD_STATE = 128
SSD_XBC = SSD_INNER + 2 * SSD_GROUPS * SSD_STATE
Q = 128
NCH = T // Q

NA_HEADS = 16
NA_KH = 8
NA_KW = 16
NA_QROWS = 2
NA_KROWS = 10
NA_NQ = NA_QROWS * GRID_W
NA_NK = NA_KROWS * GRID_W
NA_CASES = 5
NA_NOFF = 2 * NA_KH - 1

SWA_HEADS = 16
SWA_KV = 4
SWA_W = 128
ROPE_BASE = 10000.0
D_FF = 4 * D

R_XBC, R_DT, R_NAK, R_NAV, R_SWAK, R_SWAV, R_Z, R_NAQ, R_SWAQ, R_GATE, R_END = (
    0, 4096, 4160, 5184, 6208, 6464, 6720, 8768, 9792, 10816, 13888)
C_XBC, C_NAK, C_NAV, C_SWAK, C_SWAV, C_Z, C_NAQ, C_SWAQ, C_GATE, NCAT = (
    0, 4096, 5120, 6144, 6656, 7168, 9216, 10240, 11264, 14336)

NEG = -1e30
LOG2E = float(np.log2(np.e))
QSCALE = 0.125 * LOG2E
VMEM_LIMIT = 48 * 1024 * 1024


def _cparams(n_axes):
    return pltpu.CompilerParams(dimension_semantics=("parallel",) * n_axes,
                                vmem_limit_bytes=VMEM_LIMIT)


def _sigmoid(x):
    return 0.5 * jnp.tanh(0.5 * x) + 0.5


def _silu(x):
    h = 0.5 * x
    return h * jnp.tanh(h) + h


def _softplus(x):
    return jnp.maximum(x, 0.0) + jnp.log1p(jnp.exp(-jnp.abs(x)))


def _nt(a, b):
    return lax.dot_general(a, b, (((1,), (1,)), ((), ())), preferred_element_type=F32)


def _tn(a, b):
    return lax.dot_general(a, b, (((0,), (0,)), ((), ())), preferred_element_type=F32)


def _dot(a, b):
    return jnp.dot(a, b, preferred_element_type=F32)


def _chunk_loop(n, unroll, body):
    def wrapped(c, carry):
        body(c, pl.multiple_of(c * Q, Q))
        return carry
    lax.fori_loop(0, n, wrapped, 0, unroll=unroll)


def _ada_kernel(c_ref, w_ref, b_ref, o_ref):
    c = c_ref[...]
    s = _silu(c).astype(BF16)
    o_ref[...] = _dot(s, w_ref[...].astype(BF16)) + b_ref[...]


def _ada_mod(cvec, ada_w, ada_b):
    depth = ada_w.shape[0]
    r = cvec.shape[0]
    return pl.pallas_call(
        _ada_kernel,
        out_shape=jax.ShapeDtypeStruct((depth, r, N_MOD * D), F32),
        grid=(depth, N_MOD),
        in_specs=[pl.BlockSpec((r, D), lambda l, n: (0, 0)),
                  pl.BlockSpec((None, D, D), lambda l, n: (l, 0, n)),
                  pl.BlockSpec((None, 1, D), lambda l, n: (l, 0, n))],
        out_specs=pl.BlockSpec((None, r, D), lambda l, n: (l, 0, n)),
        compiler_params=_cparams(2),
        name="ada_mod",
    )(cvec, ada_w, ada_b.reshape(depth, 1, N_MOD * D))


TR = 256
NRT = T // TR
MERGE_TILES = ((T, 384), (L, 512))
FFN_TILES = ((T, 384), (L, 512))


def _mod_spec(which):
    return pl.BlockSpec((None, None, None, 1, D), lambda b, j: (b, j // (L // TR), which, 0, 0))


def _mod_pair(which):
    return [pl.BlockSpec((None, None, None, 1, D), lambda b, j, k=k: (b, k, which, 0, 0)) for k in (0, 1)]


def _pick_mod(lat_ref, ctx_ref, tr):
    row = lax.broadcasted_iota(jnp.int32, (tr, 1), 0) + pl.program_id(1) * tr
    return jnp.where(row >= L, ctx_ref[...], lat_ref[...])


def _rms(x, g):
    return x * lax.rsqrt(jnp.mean(x * x, axis=-1, keepdims=True) + EPS) * g


def _normmod_kernel(x_ref, g_ref, sh_ref, sc_ref, o_ref):
    y = _rms(x_ref[...], g_ref[...])
    o_ref[...] = (y * (1.0 + sc_ref[...]) + sh_ref[...]).astype(o_ref.dtype)


def _normmod(xs, g, ms):
    b = xs.shape[0]
    return pl.pallas_call(
        _normmod_kernel,
        out_shape=jax.ShapeDtypeStruct((b, T, D), BF16),
        grid=(b, NRT),
        in_specs=[pl.BlockSpec((None, TR, D), lambda b, j: (b, j, 0)),
                  pl.BlockSpec((1, D), lambda b, j: (0, 0)),
                  _mod_spec(0), _mod_spec(1)],
        out_specs=pl.BlockSpec((None, TR, D), lambda b, j: (b, j, 0)),
        compiler_params=_cparams(2),
        name="normmod",
    )(xs, g.reshape(1, D), ms, ms)


def _mm_kernel(x_ref, w_ref, o_ref):
    o_ref[...] = _dot(x_ref[...], w_ref[...]).astype(o_ref.dtype)


def _matmul(x, w, out_dtype, tn, name):
    b, t, k = x.shape
    n = w.shape[1]
    return pl.pallas_call(
        _mm_kernel,
        out_shape=jax.ShapeDtypeStruct((b, t, n), out_dtype),
        grid=(n // tn, b),
        in_specs=[pl.BlockSpec((None, t, k), lambda n_, b_: (b_, 0, 0)),
                  pl.BlockSpec((k, tn), lambda n_, b_: (0, n_))],
        out_specs=pl.BlockSpec((None, t, tn), lambda n_, b_: (b_, 0, n_)),
        compiler_params=_cparams(2),
        name=name,
    )(x, w)


CW = 512
CONV_RC = 256
PAD_LAT = 8
PAD_CTX = 16 + L


def _conv_kernel(x_ref, w_ref, b_ref, o_ref, pad_ref):
    z8 = jnp.zeros((8, CW), F32)
    pad_ref[0:8, :] = z8
    pad_ref[PAD_LAT + L:PAD_CTX, :] = z8
    pad_ref[PAD_CTX + NC:PAD_CTX + NC + 8, :] = z8
    pad_ref[PAD_LAT:PAD_LAT + L, :] = x_ref[0:L, :].astype(F32)
    pad_ref[PAD_CTX:PAD_CTX + NC, :] = x_ref[L:T, :].astype(F32)
    w = w_ref[...]
    bias = b_ref[...]
    for c in range(T // CONV_RC):
        r0 = c * CONV_RC
        base = (PAD_LAT if r0 < L else PAD_CTX - L) + r0
        win = pad_ref[base - 8:base + CONV_RC + 8, :]
        acc = bias + win[8:8 + CONV_RC] * w[2:3, :]
        for k in (0, 1, 3, 4):
            shifted = pltpu.roll(win, (2 - k) % (CONV_RC + 16), axis=0)
            acc = acc + shifted[8:8 + CONV_RC] * w[k:k + 1, :]
        o_ref[r0:r0 + CONV_RC, :] = _silu(acc).astype(o_ref.dtype)


def _conv_silu(p, conv_w, conv_b):
    b = p.shape[0]
    w8 = jnp.concatenate([conv_w, jnp.zeros((3, SSD_XBC), F32)], axis=0)
    return pl.pallas_call(
        _conv_kernel,
        out_shape=jax.ShapeDtypeStruct((b, T, SSD_XBC), BF16),
        grid=(b, SSD_XBC // CW),
        in_specs=[pl.BlockSpec((None, T, CW), lambda b_, n: (b_, 0, n)),
                  pl.BlockSpec((8, CW), lambda b_, n: (0, n)),
                  pl.BlockSpec((1, CW), lambda b_, n: (0, n))],
        out_specs=pl.BlockSpec((None, T, CW), lambda b_, n: (b_, 0, n)),
        scratch_shapes=[pltpu.VMEM((T + 24, CW), F32)],
        compiler_params=_cparams(2),
        name="conv_silu",
    )(p, w8, conv_b.reshape(1, SSD_XBC))


GW = SSD_HPG * HEAD_DIM


def _split3(x):
    hi = x.astype(BF16)
    r = x - hi.astype(F32)
    mid = r.astype(BF16)
    lo = (r - mid.astype(F32)).astype(BF16)
    return hi, mid, lo


def _cum_constants():
    ii = lax.broadcasted_iota(jnp.int32, (Q, Q), 0)
    jj = lax.broadcasted_iota(jnp.int32, (Q, Q), 1)
    tril = jnp.where(ii >= jj, 1.0, 0.0).astype(BF16)
    triu = jnp.where(ii <= jj, 1.0, 0.0).astype(BF16)
    return jnp.concatenate([jnp.concatenate([tril] * 3, axis=1),
                            jnp.concatenate([triu] * 3, axis=1)], axis=0)


def _ssd_prep_kernel(dt_ref, bias_ref, alog_ref, accg_o, wts_o, eacc_o, rowt_o, dtt_o):
    lcat = _cum_constants()
    isf = (lax.broadcasted_iota(jnp.int32, (1, Q), 1) % 8) < 4
    bias = bias_ref[...]
    a_r = -jnp.exp(alog_ref[...])

    def body(c, t0):
        rows = pl.ds(t0, Q)
        dt = _softplus(dt_ref[rows, :] + bias)
        both = _dot(lcat, jnp.concatenate(_split3(dt * a_r), axis=0))
        acc = jnp.where(isf, both[0:Q], both[Q:2 * Q])
        tot = jnp.where(isf, both[Q - 1:Q], both[Q:Q + 1])
        for g in range(SSD_GROUPS):
            accg_o[g, rows, :] = acc if g == 0 else pltpu.roll(acc, Q - 8 * g, axis=1)
        wts_o[rows, :] = dt * jnp.exp(tot - acc)
        eacc_o[rows, :] = jnp.exp(acc)
        dt_t = dt.T
        rowt_o[:, rows] = acc.T - jnp.log(dt_t)
        dtt_o[:, rows] = dt_t

    _chunk_loop(NCH, 3, body)


def _ssd_kernel(x_ref, b_ref, c_ref, z_ref, acc_ref, wts_ref, eacc_ref, rowt_ref, dtt_ref, dsk_ref, ng_ref,
                o_ref, xw_s, ec_s, big_s, sin_s, cb_s, y_s, *, n_out):
    g = pl.program_id(1)
    ii = lax.broadcasted_iota(jnp.int32, (Q, Q), 0)
    jj = lax.broadcasted_iota(jnp.int32, (Q, Q), 1)
    low = ii > jj
    up = ii < jj
    lo = jj < HEAD_DIM
    ek = lax.broadcasted_iota(jnp.int32, (Q, 2 * GW), 0)
    ech = lax.broadcasted_iota(jnp.int32, (Q, 2 * GW), 1)
    esel = jnp.where(ek == g * 8 + ech // HEAD_DIM, 1.0, 0.0).astype(BF16)
    esel2 = jnp.concatenate([esel] * 2, axis=0)
    zero_x = jnp.zeros((Q, Q), BF16)
    rows = lambda t0: pl.ds(t0, Q)

    def stage_expand(c, t0):
        w2 = _split3(wts_ref[rows(t0), :])[:2]
        e2 = _split3(eacc_ref[rows(t0), :])[:2]
        lhs = jnp.concatenate([jnp.concatenate(w2, axis=1), jnp.concatenate(e2, axis=1)], axis=0)
        both = _dot(lhs, esel2)
        xs = x_ref[rows(t0), :].astype(F32)
        xw_s[rows(t0), :] = (jnp.concatenate([xs, xs], axis=1) * both[0:Q]).astype(BF16)
        ec_s[rows(t0), :] = both[Q:2 * Q]

    _chunk_loop(NCH, 3, stage_expand)

    def stage_upd(c, t0):
        big_s[c] = _tn(b_ref[rows(t0), :], xw_s[rows(t0), :])

    _chunk_loop(NCH, 6, stage_upd)

    s = jnp.zeros((SSD_STATE, GW), F32)
    for c in list(range(L // Q, NCH)) + list(range(L // Q)):
        sin_s[c, :, 0:GW] = s.astype(BF16)
        s = s * ec_s[c * Q + Q - 1:c * Q + Q, 0:GW] + big_s[c, :, 0:GW]
    s = jnp.zeros((SSD_STATE, GW), F32)
    for c in reversed(range(NCH)):
        sin_s[c, :, GW:2 * GW] = s.astype(BF16)
        s = s * ec_s[c * Q:c * Q + 1, GW:2 * GW] + big_s[c, :, GW:2 * GW]

    def stage_cb(c, t0):
        cc = c_ref[rows(t0), :]
        cb_s[rows(t0), :] = _nt(cc, b_ref[rows(t0), :])
        big_s[c] = _dot(cc, sin_s[c])

    _chunk_loop(n_out, 6 if n_out % 6 == 0 else 4, stage_cb)

    dsk = dsk_ref[...]
    ng = ng_ref[...]

    def colb(v, k):
        return jnp.broadcast_to(v[:, k:k + 1], (Q, Q))

    def stage_y(c, t0):
        xb = x_ref[rows(t0), :]
        acc = acc_ref[rows(t0), :]
        rowt = rowt_ref[:, rows(t0)]
        dt_t = dtt_ref[:, rows(t0)]
        ec = ec_s[rows(t0), :]
        cb = cb_s[rows(t0), :]
        yoff = big_s[c]
        ys = []
        for pr in range(2):
            ms = []
            for hh in range(2):
                r = 2 * pr + hh
                diag = jnp.log(dt_t[r:r + 1, :] + dt_t[4 + r:5 + r, :])
                seg = jnp.where(low, colb(acc, r) - rowt[r:r + 1, :],
                                jnp.where(up, colb(acc, 4 + r) - rowt[4 + r:5 + r, :], diag))
                ms.append((cb * jnp.exp(seg)).astype(BF16))
            xp = xb[:, pr * Q:(pr + 1) * Q]
            xstack = jnp.concatenate([jnp.where(lo, xp, zero_x), jnp.where(lo, zero_x, xp)], axis=0)
            y = _dot(jnp.concatenate(ms, axis=1), xstack)
            y = (y + ec[:, pr * Q:(pr + 1) * Q] * yoff[:, pr * Q:(pr + 1) * Q]
                 + ec[:, GW + pr * Q:GW + (pr + 1) * Q] * yoff[:, GW + pr * Q:GW + (pr + 1) * Q])
            ys.append(y)
        y_s[rows(t0), :] = jnp.concatenate(ys, axis=1)

    _chunk_loop(n_out, 2, stage_y)

    def stage_out(c, t0):
        y = y_s[rows(t0), :] + dsk * x_ref[rows(t0), :].astype(F32)
        zf = z_ref[rows(t0), :].astype(F32)
        u = y * _silu(zf)
        o_ref[rows(t0), :] = _rms(u, ng).astype(o_ref.dtype)

    _chunk_loop(n_out, 3 if n_out % 3 == 0 else 4, stage_out)


def _slot_order(v):
    s = v.reshape(2, SSD_GROUPS, SSD_HPG).transpose(1, 0, 2).reshape(1, 2 * SSD_HEADS)
    return jnp.pad(s, ((0, 0), (0, Q - 2 * SSD_HEADS)))


def _ssd(xa, p, dtr, dt_bias, a_log, ssd_d, ssd_norm_g, emit_ctx):
    b = xa.shape[0]
    t_out = T if emit_ctx else L
    row = pl.BlockSpec((None, T, Q), lambda b_: (b_, 0, 0))
    col = pl.BlockSpec((None, Q, T), lambda b_: (b_, 0, 0))
    one = pl.BlockSpec((1, Q), lambda b_: (0, 0))
    accg, wts, eacc, rowt, dtt = pl.pallas_call(
        _ssd_prep_kernel,
        out_shape=([jax.ShapeDtypeStruct((b, SSD_GROUPS, T, Q), F32)] + [jax.ShapeDtypeStruct((b, T, Q), F32)] * 2
                   + [jax.ShapeDtypeStruct((b, Q, T), F32)] * 2),
        grid=(b,),
        in_specs=[row, one, one],
        out_specs=[pl.BlockSpec((None, SSD_GROUPS, T, Q), lambda b_: (b_, 0, 0, 0)), row, row, col, col],
        compiler_params=_cparams(1),
        name="ssd_prep",
    )(dtr, _slot_order(dt_bias), _slot_order(a_log))

    dsk = jnp.repeat(ssd_d, HEAD_DIM).reshape(1, SSD_INNER)
    ng = ssd_norm_g.reshape(1, SSD_INNER)
    xoff, boff, coff, zoff = 0, SSD_INNER // Q, (SSD_INNER + SSD_GROUPS * SSD_STATE) // Q, C_Z // GW
    full = pl.BlockSpec((None, T, Q), lambda b_, g: (b_, 0, 0))
    slot = pl.BlockSpec((None, 8, T), lambda b_, g: (b_, g, 0))
    return pl.pallas_call(
        functools.partial(_ssd_kernel, n_out=t_out // Q),
        out_shape=jax.ShapeDtypeStruct((b, t_out, SSD_INNER), BF16),
        grid=(b, SSD_GROUPS),
        in_specs=[pl.BlockSpec((None, T, GW), lambda b_, g: (b_, 0, xoff + g)),
                  pl.BlockSpec((None, T, Q), lambda b_, g: (b_, 0, boff + g)),
                  pl.BlockSpec((None, T, Q), lambda b_, g: (b_, 0, coff + g)),
                  pl.BlockSpec((None, T, GW), lambda b_, g: (b_, 0, zoff + g)),
                  pl.BlockSpec((None, None, T, Q), lambda b_, g: (b_, g, 0, 0)), full, full, slot, slot,
                  pl.BlockSpec((1, GW), lambda b_, g: (0, g)),
                  pl.BlockSpec((1, GW), lambda b_, g: (0, g))],
        out_specs=pl.BlockSpec((None, t_out, GW), lambda b_, g: (b_, 0, g)),
        scratch_shapes=[pltpu.VMEM((T, 2 * GW), BF16),
                        pltpu.VMEM((T, 2 * GW), F32),
                        pltpu.VMEM((NCH, SSD_STATE, 2 * GW), F32),
                        pltpu.VMEM((NCH, SSD_STATE, 2 * GW), BF16),
                        pltpu.VMEM((T, Q), F32),
                        pltpu.VMEM((T, GW), F32)],
        compiler_params=_cparams(2),
        name="ssd",
    )(xa, xa, xa, p, accg, wts, eacc, rowt, dtt, dsk, ng)


def _softmax_t(parts, extra=None):
    mx = None
    for s in parts:
        r = jnp.max(s, axis=0, keepdims=True)
        mx = r if mx is None else jnp.maximum(mx, r)
    if extra is not None:
        mx = jnp.maximum(mx, extra)
    den = None
    es = []
    for s in parts:
        e = jnp.exp2(s - mx)
        d = jnp.sum(e, axis=0, keepdims=True)
        den = d if den is None else den + d
        es.append(e.astype(BF16))
    if extra is not None:
        den = den + jnp.exp2(extra - mx)
    return es, 1.0 / den


def _pv_t(vts, es):
    out = None
    for vt, e in zip(vts, es):
        o = _dot(vt, e)
        out = o if out is None else out + o
    return out


def _na_row_offsets():
    table = []
    for j in (0, 1, 2, ROWS // NA_QROWS - 2, ROWS // NA_QROWS - 1):
        start = int(np.clip(NA_QROWS * j - NA_KH // 2, 0, ROWS - NA_KROWS))
        per_row = []
        for i in range(NA_KROWS):
            kr = start + i
            offs = []
            for qr in range(NA_QROWS):
                r = NA_QROWS * j + qr
                rs = int(np.clip(r - NA_KH // 2, 0, ROWS - NA_KH))
                offs.append(kr - r + NA_KH - 1 if rs <= kr < rs + NA_KH else NA_NOFF)
            per_row.append(tuple(offs))
        table.append(per_row)
    return table


def _na_kernel(q_ref, k_ref, v_ref, src_ref, o_ref, vt_ref, bias_ref, sa_ref, sb_ref, pa_ref, pb_ref, ra_ref,
               rb_ref, *, emit_ctx):
    @pl.when(pl.program_id(1) == 0)
    def _():
        lo_tile = lax.broadcasted_iota(jnp.int32, (GRID_W, Q), 1) < GRID_W
        for ci, per_row in enumerate(_na_row_offsets()):
            for i, (off0, off1) in enumerate(per_row):
                for hh in range(2):
                    bias_ref[ci, i * GRID_W:(i + 1) * GRID_W, hh * Q:(hh + 1) * Q] = jnp.where(
                        lo_tile, src_ref[hh, off0], src_ref[hh, off1])

    lane = lax.broadcasted_iota(jnp.int32, (NA_NQ, Q), 1)
    lo = lane < HEAD_DIM
    top = lax.broadcasted_iota(jnp.int32, (Q, NA_NQ), 0) < HEAD_DIM
    zero = jnp.zeros((NA_NQ, Q), BF16)

    def vt_body(c, t0):
        vt_ref[:, pl.ds(t0, Q)] = v_ref[pl.ds(t0, Q), :].astype(F32).T.astype(BF16)

    _chunk_loop(NCH, 6, vt_body)

    kc = k_ref[L:T, :]
    vtc = vt_ref[:, L:T]

    def qpair(q0):
        q = (q_ref[pl.ds(q0, NA_NQ), :].astype(F32) * QSCALE).astype(BF16)
        return jnp.concatenate([jnp.where(lo, q, zero), jnp.where(lo, zero, q)], axis=0)

    def finish(q0, ot, rden):
        ot = ot * rden
        w = jnp.where(top, ot[:, 0:NA_NQ], ot[:, NA_NQ:2 * NA_NQ])
        o_ref[pl.ds(q0, NA_NQ), :] = w.T.astype(o_ref.dtype)

    def block_params(j):
        q0 = pl.multiple_of(j * NA_NQ, NA_NQ)
        start = jnp.clip(NA_QROWS * j - NA_KH // 2, 0, ROWS - NA_KROWS)
        koff = pl.multiple_of(start * GRID_W, 2 * GRID_W)
        last = ROWS // NA_QROWS - 1
        case = (jnp.where(j >= 1, 1, 0) + jnp.where(j >= 2, 1, 0)
                + jnp.where(j >= last - 1, 1, 0) + jnp.where(j >= last, 1, 0))
        return q0, koff, case

    def qk(j, s_ref):
        q0, koff, case = block_params(j)
        qp = qpair(q0)
        s_ref[0:NA_NK, :] = _nt(k_ref[pl.ds(koff, NA_NK), :], qp) + bias_ref[case]
        s_ref[NA_NK:NA_NK + NC, :] = _nt(kc, qp)

    def soft(s_ref, p_ref, r_ref):
        s = s_ref[...]
        e = jnp.exp2(s - jnp.max(s, axis=0, keepdims=True))
        r_ref[...] = jnp.broadcast_to(1.0 / jnp.sum(e, axis=0, keepdims=True), r_ref.shape)
        p_ref[...] = e.astype(BF16)

    def pv(j, p_ref, r_ref):
        q0, koff, _ = block_params(j)
        ot = (_dot(vt_ref[:, pl.ds(koff, NA_NK)], p_ref[0:NA_NK, :])
              + _dot(vtc, p_ref[NA_NK:NA_NK + NC, :]))
        finish(q0, ot, r_ref[0:1, :])

    nblk = L // NA_NQ
    qk(0, sa_ref)
    qk(1, sb_ref)
    soft(sa_ref, pa_ref, ra_ref)

    def body(m, carry):
        pv(2 * m - 2, pa_ref, ra_ref)
        qk(2 * m, sa_ref)
        soft(sb_ref, pb_ref, rb_ref)
        pv(2 * m - 1, pb_ref, rb_ref)
        qk(2 * m + 1, sb_ref)
        soft(sa_ref, pa_ref, ra_ref)
        return carry

    lax.fori_loop(1, nblk // 2, body, 0)
    pv(nblk - 2, pa_ref, ra_ref)
    soft(sb_ref, pb_ref, rb_ref)
    pv(nblk - 1, pb_ref, rb_ref)

    if emit_ctx:
        ctx_blocks = []
        for cbk in range(NC // NA_NQ):
            q0 = L + cbk * NA_NQ
            ctx_blocks.append((q0, _nt(kc, qpair(q0))))
        ctx_soft = [(q0,) + _softmax_t([s]) for q0, s in ctx_blocks]
        for q0, es, rden in ctx_soft:
            finish(q0, _pv_t([vtc], es), rden)


def _na_bias_source(rpb_all):
    depth = rpb_all.shape[0]
    ck = np.arange(GRID_W)[:, None]
    cq = (np.arange(Q) % GRID_W)[None, :]
    col_oh = ((ck - cq + NA_KW - 1)[None] == np.arange(2 * NA_KW - 1)[:, None, None]).astype(np.float32)
    cs = np.clip(cq - NA_KW // 2, 0, GRID_W - NA_KW)
    col_ok = (ck >= cs) & (ck < cs + NA_KW)
    rp = rpb_all.reshape(depth, NA_HEADS // 2, 2, NA_NOFF, 2 * NA_KW - 1)
    src = jnp.einsum("lperd,dbz->lperbz", rp, col_oh, precision=HIGHEST)
    src = jnp.where(col_ok, src * LOG2E, NEG)
    return jnp.concatenate([src, jnp.full_like(src[:, :, :, :1], NEG)], axis=3)


def _na(p, bias_src, emit_ctx):
    b = p.shape[0]
    t_out = T if emit_ctx else L
    qo, ko, vo = C_NAQ // Q, C_NAK // Q, C_NAV // Q
    return pl.pallas_call(
        functools.partial(_na_kernel, emit_ctx=emit_ctx),
        out_shape=jax.ShapeDtypeStruct((b, t_out, D), BF16),
        grid=(NA_HEADS // 2, b),
        in_specs=[pl.BlockSpec((None, T, Q), lambda h, b_: (b_, 0, qo + h)),
                  pl.BlockSpec((None, T, Q), lambda h, b_: (b_, 0, ko + h)),
                  pl.BlockSpec((None, T, Q), lambda h, b_: (b_, 0, vo + h)),
                  pl.BlockSpec((None, 2, NA_NOFF + 1, GRID_W, Q), lambda h, b_: (h, 0, 0, 0, 0))],
        out_specs=pl.BlockSpec((None, t_out, Q), lambda h, b_: (b_, 0, h)),
        scratch_shapes=[pltpu.VMEM((Q, T), BF16),
                        pltpu.VMEM((NA_CASES, NA_NK, 2 * NA_NQ), F32),
                        pltpu.VMEM((NA_NK + NC, 2 * NA_NQ), F32),
                        pltpu.VMEM((NA_NK + NC, 2 * NA_NQ), F32),
                        pltpu.VMEM((NA_NK + NC, 2 * NA_NQ), BF16),
                        pltpu.VMEM((NA_NK + NC, 2 * NA_NQ), BF16),
                        pltpu.VMEM((8, 2 * NA_NQ), F32),
                        pltpu.VMEM((8, 2 * NA_NQ), F32)],
        compiler_params=pltpu.CompilerParams(dimension_semantics=("parallel", "arbitrary"),
                                             vmem_limit_bytes=VMEM_LIMIT),
        name="na_attn",
    )(p, p, p, bias_src)


SWA_KWIN = 3 * SWA_W
SWA_QB = 128
SWA_STACK = 4 * SWA_QB
ROPE_RC = 256


def _swap_rotary_halves(t, lane):
    quarter = HEAD_DIM // 4
    a = pltpu.roll(t, Q - quarter, axis=1)
    b = pltpu.roll(t, quarter, axis=1)
    return jnp.where((lane % (2 * quarter)) < quarter, a, b)


def _swa_kernel(sink_ref, q_ref, k_ref, v_ref, cos_ref, sin_ref, mask_ref, o_ref, krot_ref, vt_ref,
                sa_ref, sb_ref, pa_ref, pb_ref, ra_ref, rb_ref, *, emit_ctx):
    kp = pl.program_id(1)
    lane = lax.broadcasted_iota(jnp.int32, (SWA_QB, Q), 1)
    lo = lane < HEAD_DIM
    zero = jnp.zeros((SWA_QB, Q), BF16)
    col = lax.broadcasted_iota(jnp.int32, (1, SWA_STACK), 1)
    eye4 = jnp.where(lax.broadcasted_iota(jnp.int32, (SWA_STACK, Q), 0) % SWA_QB
                     == lax.broadcasted_iota(jnp.int32, (SWA_STACK, Q), 1), 1.0, 0.0).astype(BF16)

    def rope(t, r0, n):
        lane_n = lax.broadcasted_iota(jnp.int32, (n, Q), 1)
        return t * cos_ref[pl.ds(r0, n), :] + _swap_rotary_halves(t, lane_n) * sin_ref[pl.ds(r0, n), :]

    def krot_body(i, carry):
        r0 = pl.multiple_of(i * ROPE_RC, ROPE_RC)
        for gl in range(2):
            t = k_ref[pl.ds(r0, ROPE_RC), gl * Q:(gl + 1) * Q].astype(F32)
            krot_ref[pl.ds(r0, ROPE_RC), gl * Q:(gl + 1) * Q] = rope(t, r0, ROPE_RC).astype(BF16)
        return carry

    lax.fori_loop(0, L // ROPE_RC, krot_body, 0, unroll=2)

    def vt_body(c, t0):
        for gl in range(2):
            vt = v_ref[pl.ds(t0, Q), gl * Q:(gl + 1) * Q].astype(F32).T
            vt_ref[gl, :, pl.ds(t0, Q)] = vt[0:HEAD_DIM].astype(BF16)

    _chunk_loop(NCH, 6, vt_body)

    def stack_q(qa, qb):
        return jnp.concatenate([jnp.where(lo, qa, zero), jnp.where(lo, zero, qa),
                                jnp.where(lo, qb, zero), jnp.where(lo, zero, qb)], axis=0)

    def sink_row(gl):
        base = kp * 8 + gl * 4
        row = jnp.where(col < SWA_QB, sink_ref[base],
                        jnp.where(col < 2 * SWA_QB, sink_ref[base + 1],
                                  jnp.where(col < 3 * SWA_QB, sink_ref[base + 2], sink_ref[base + 3])))
        return row * LOG2E

    def finish(q0, gl, ot, rden):
        ot = ot * rden
        pa = jnp.concatenate([ot[:, 0:SWA_QB], ot[:, SWA_QB:2 * SWA_QB]], axis=0).T
        pb = jnp.concatenate([ot[:, 2 * SWA_QB:3 * SWA_QB], ot[:, 3 * SWA_QB:4 * SWA_QB]], axis=0).T
        o_ref[pl.ds(q0, SWA_QB), gl * 2 * Q:(gl + 1) * 2 * Q] = jnp.concatenate([pa, pb], axis=1).astype(o_ref.dtype)

    def block_params(i):
        q0 = pl.multiple_of(i * SWA_QB, SWA_QB)
        ws = pl.multiple_of(jnp.clip((i - 1) * SWA_W, 0, L - SWA_KWIN), SWA_W)
        case = jnp.where(i >= 1, 1, 0) + jnp.where(i >= L // SWA_QB - 1, 1, 0)
        return q0, ws, case

    def qk(i, gl, s_ref):
        q0, ws, case = block_params(i)
        qa = rope(q_ref[pl.ds(q0, SWA_QB), gl * 2 * Q:gl * 2 * Q + Q].astype(F32), q0, SWA_QB) * QSCALE
        qb = rope(q_ref[pl.ds(q0, SWA_QB), gl * 2 * Q + Q:(gl + 1) * 2 * Q].astype(F32), q0, SWA_QB) * QSCALE
        qs = stack_q(qa.astype(BF16), qb.astype(BF16))
        a_loc = jnp.concatenate([krot_ref[pl.ds(ws, SWA_KWIN), gl * Q:(gl + 1) * Q], mask_ref[case]], axis=1)
        s_ref[0:SWA_KWIN, :] = _nt(a_loc, jnp.concatenate([qs, eye4], axis=1))
        s_ref[SWA_KWIN:SWA_KWIN + NC, :] = _nt(k_ref[L:T, gl * Q:(gl + 1) * Q], qs)

    def soft(gl, s_ref, p_ref, r_ref):
        s = s_ref[...]
        sink = sink_row(gl)
        mx = jnp.maximum(jnp.max(s, axis=0, keepdims=True), sink)
        e = jnp.exp2(s - mx)
        den = jnp.sum(e, axis=0, keepdims=True) + jnp.exp2(sink - mx)
        r_ref[...] = jnp.broadcast_to(1.0 / den, r_ref.shape)
        p_ref[...] = e.astype(BF16)

    def pv(i, gl, p_ref, r_ref):
        q0, ws, _ = block_params(i)
        ot = (_dot(vt_ref[gl, :, pl.ds(ws, SWA_KWIN)], p_ref[0:SWA_KWIN, :])
              + _dot(vt_ref[gl, :, L:T], p_ref[SWA_KWIN:SWA_KWIN + NC, :]))
        finish(q0, gl, ot, r_ref[0:1, :])

    nblk = L // SWA_QB
    qk(0, 0, sa_ref)
    qk(0, 1, sb_ref)
    soft(0, sa_ref, pa_ref, ra_ref)

    def body(i, carry):
        pv(i - 1, 0, pa_ref, ra_ref)
        qk(i, 0, sa_ref)
        soft(1, sb_ref, pb_ref, rb_ref)
        pv(i - 1, 1, pb_ref, rb_ref)
        qk(i, 1, sb_ref)
        soft(0, sa_ref, pa_ref, ra_ref)
        return carry

    lax.fori_loop(1, nblk, body, 0)
    pv(nblk - 1, 0, pa_ref, ra_ref)
    soft(1, sb_ref, pb_ref, rb_ref)
    pv(nblk - 1, 1, pb_ref, rb_ref)

    for cbk in range(NC // SWA_QB if emit_ctx else 0):
        q0 = L + cbk * SWA_QB
        scores = []
        for gl in range(2):
            qa = q_ref[q0:q0 + SWA_QB, gl * 2 * Q:gl * 2 * Q + Q].astype(F32) * QSCALE
            qb = q_ref[q0:q0 + SWA_QB, gl * 2 * Q + Q:(gl + 1) * 2 * Q].astype(F32) * QSCALE
            scores.append(_nt(k_ref[L:T, gl * Q:(gl + 1) * Q], stack_q(qa.astype(BF16), qb.astype(BF16))))
        soft = [_softmax_t([s], sink_row(gl)) for gl, s in enumerate(scores)]
        for gl, (es, rden) in enumerate(soft):
            finish(q0, gl, _pv_t([vt_ref[gl, :, L:T]], es), rden)


def _rope_tables():
    pos = np.arange(L)
    quarter = HEAD_DIM // 4
    lane = np.arange(Q) % HEAD_DIM
    inv = ROPE_BASE ** (-(lane % quarter).astype(np.float64) / quarter)
    p = np.where(lane[None, :] < HEAD_DIM // 2, (pos // GRID_W)[:, None], (pos % GRID_W)[:, None]).astype(np.float64)
    ang = p * inv[None, :]
    sign = np.where((lane % (2 * quarter)) < quarter, -1.0, 1.0)[None, :]
    return jnp.asarray(np.cos(ang), F32), jnp.asarray(np.sin(ang) * sign, F32)


def _swa_mask_table():
    key = np.arange(SWA_KWIN)[:, None]
    qry = np.arange(SWA_QB)[None, :]
    tabs = [np.where(np.abs(key - qry + delta) <= SWA_W, 0.0, NEG) for delta in (0, -SWA_W, -2 * SWA_W)]
    return jnp.asarray(np.stack(tabs), BF16)


def _swa(p, sink, cos_t, sin_t, mask_t, emit_ctx):
    b = p.shape[0]
    t_out = T if emit_ctx else L
    qo, ko, vo = C_SWAQ // 512, C_SWAK // 256, C_SWAV // 256
    return pl.pallas_call(
        functools.partial(_swa_kernel, emit_ctx=emit_ctx),
        out_shape=jax.ShapeDtypeStruct((b, t_out, D), BF16),
        grid=(b, SWA_KV // 2),
        in_specs=[pl.BlockSpec(memory_space=pltpu.SMEM),
                  pl.BlockSpec((None, T, 512), lambda b_, h: (b_, 0, qo + h)),
                  pl.BlockSpec((None, T, 256), lambda b_, h: (b_, 0, ko + h)),
                  pl.BlockSpec((None, T, 256), lambda b_, h: (b_, 0, vo + h)),
                  pl.BlockSpec((L, Q), lambda b_, h: (0, 0)),
                  pl.BlockSpec((L, Q), lambda b_, h: (0, 0)),
                  pl.BlockSpec((3, SWA_KWIN, SWA_QB), lambda b_, h: (0, 0, 0))],
        out_specs=pl.BlockSpec((None, t_out, 512), lambda b_, h: (b_, 0, h)),
        scratch_shapes=[pltpu.VMEM((L, 2 * Q), BF16),
                        pltpu.VMEM((2, HEAD_DIM, T), BF16),
                        pltpu.VMEM((SWA_KWIN + NC, SWA_STACK), F32),
                        pltpu.VMEM((SWA_KWIN + NC, SWA_STACK), F32),
                        pltpu.VMEM((SWA_KWIN + NC, SWA_STACK), BF16),
                        pltpu.VMEM((SWA_KWIN + NC, SWA_STACK), BF16),
                        pltpu.VMEM((8, SWA_STACK), F32),
                        pltpu.VMEM((8, SWA_STACK), F32)],
        compiler_params=_cparams(2),
        name="swa_attn",
    )(sink, p, p, p, cos_t, sin_t, mask_t)


def _merge_kernel(x_ref, ya_ref, yb_ref, yc_ref, g0_ref, g1_ref, g2_ref, mgl_ref, mgc_ref,
                  wa_ref, wb_ref, wc_ref, wo_ref, o_ref):
    m = (_sigmoid(g0_ref[...].astype(F32)) * _dot(ya_ref[...], wa_ref[...])
         + _sigmoid(g1_ref[...].astype(F32)) * _dot(yb_ref[...], wb_ref[...])
         + _sigmoid(g2_ref[...].astype(F32)) * _dot(yc_ref[...], wc_ref[...]))
    mg = _pick_mod(mgl_ref, mgc_ref, x_ref.shape[0])
    o_ref[...] = x_ref[...] + mg * _dot(m.astype(BF16), wo_ref[...])


def _const_spec(shape, single=True):
    mode = {"pipeline_mode": pl.Buffered(1)} if single else {}
    return pl.BlockSpec(shape, lambda b, j: (0,) * len(shape), **mode)


def _merge(xs, ya, yb, yc, p, ms, wa, wb, wc, wo, tr, n_tiles):
    b = xs.shape[0]
    go = C_GATE // D
    row = lambda w: pl.BlockSpec((None, tr, w), lambda b_, j: (b_, j, 0))
    gate = lambda k: pl.BlockSpec((None, tr, D), lambda b_, j: (b_, j, go + k))
    return pl.pallas_call(
        _merge_kernel,
        out_shape=jax.ShapeDtypeStruct((b, n_tiles * tr, D), F32),
        grid=(b, n_tiles),
        in_specs=[row(D), row(SSD_INNER), row(D), row(D), gate(0), gate(1), gate(2), *_mod_pair(2),
                  _const_spec((SSD_INNER, D)), _const_spec((D, D)), _const_spec((D, D)), _const_spec((D, D))],
        out_specs=row(D),
        compiler_params=_cparams(2),
        name="merge",
    )(xs, ya, yb, yc, p, p, p, ms, ms, wa, wb, wc, wo)


FF_CH = 1024


def _ffn_kernel(*refs, final):
    if final:
        x_ref, g_ref, shl, shc, scl, scc, mgl, mgc, w1_ref, w2_ref, gn_ref, o_ref = refs
    else:
        (x_ref, g_ref, shl, shc, scl, scc, mgl, mgc, w1_ref, w2_ref, gn_ref, shnl, shnc, scnl, scnc,
         o_ref, h_ref) = refs
    tr = x_ref.shape[0]
    x = x_ref[...]
    h = (_rms(x, g_ref[...]) * (1.0 + _pick_mod(scl, scc, tr)) + _pick_mod(shl, shc, tr)).astype(BF16)
    acc = None
    for kf in range(D_FF // FF_CH):
        a = jnp.maximum(_dot(h, w1_ref[:, kf * FF_CH:(kf + 1) * FF_CH]), 0.0)
        o = _dot((a * a).astype(BF16), w2_ref[kf * FF_CH:(kf + 1) * FF_CH, :])
        acc = o if acc is None else acc + o
    y = x + _pick_mod(mgl, mgc, tr) * acc
    if final:
        o_ref[...] = _rms(y, gn_ref[...])
    else:
        o_ref[...] = y
        h_ref[...] = (_rms(y, gn_ref[...]) * (1.0 + _pick_mod(scnl, scnc, tr))
                      + _pick_mod(shnl, shnc, tr)).astype(h_ref.dtype)


def _ffn(xs, g, ms, w1, w2, tr, n_tiles, g_next, ms_next=None):
    b = xs.shape[0]
    final = ms_next is None
    row = pl.BlockSpec((None, tr, D), lambda b_, j: (b_, j, 0))
    in_specs = [row, _const_spec((1, D)), *_mod_pair(3), *_mod_pair(4), *_mod_pair(5),
                _const_spec((D, D_FF), single=False), _const_spec((D_FF, D), single=False)]
    in_specs.append(_const_spec((1, D)))
    args = [xs, g.reshape(1, D), ms, ms, ms, ms, ms, ms, w1, w2, g_next.reshape(1, D)]
    out_shape = jax.ShapeDtypeStruct((b, n_tiles * tr, D), F32)
    out_specs = row
    if not final:
        in_specs += [*_mod_pair(0), *_mod_pair(1)]
        args += [ms_next, ms_next, ms_next, ms_next]
        out_shape = [out_shape, jax.ShapeDtypeStruct((b, n_tiles * tr, D), BF16)]
        out_specs = [row, row]
    return pl.pallas_call(
        functools.partial(_ffn_kernel, final=final),
        out_shape=out_shape,
        grid=(b, n_tiles),
        in_specs=in_specs,
        out_specs=out_specs,
        compiler_params=_cparams(2),
        name="ffn",
    )(*args)


def _prep_w_in(w):
    def dup(a):
        a = a.reshape(D, SWA_KV, 1, HEAD_DIM)
        return jnp.concatenate([a, a], axis=2).reshape(D, SWA_KV * 128)
    cat = jnp.concatenate([
        w[:, R_XBC:R_DT], w[:, R_NAK:R_NAV], w[:, R_NAV:R_SWAK], dup(w[:, R_SWAK:R_SWAV]),
        dup(w[:, R_SWAV:R_Z]), w[:, R_Z:R_NAQ], w[:, R_NAQ:R_SWAQ], w[:, R_SWAQ:R_GATE],
        w[:, R_GATE:R_END]], axis=1).astype(BF16)
    wd = w[:, R_DT:R_NAK].reshape(D, 2, SSD_GROUPS, SSD_HPG).transpose(0, 2, 1, 3).reshape(D, 2 * SSD_HEADS)
    wdt = jnp.concatenate([wd, jnp.zeros((D, Q - 2 * SSD_HEADS), F32)], axis=1).astype(BF16)
    return cat, wdt


def kernel(x, c, ctx, c_ctx, ada_w, ada_b, norm1_g, norm2_g, w_in, conv_w, conv_b, dt_bias, a_log, ssd_d,
           ssd_norm_g, na_rpb, swa_sink, w_o_ssd, w_o_na, w_o_swa, w_out, w_ff1, w_ff2, final_g):
    b = x.shape[0]
    depth = ada_w.shape[0]
    xs = jnp.concatenate([x, ctx], axis=1)
    nrow = -(-(b + 1) // 8) * 8
    cvec = jnp.concatenate([c, c_ctx[None, :], jnp.zeros((nrow - b - 1, D), F32)], axis=0)
    mod = _ada_mod(cvec, ada_w, ada_b)
    lat = mod[:, :b].reshape(depth, b, 1, N_MOD, 1, D)
    cx = jnp.broadcast_to(mod[:, b].reshape(depth, 1, 1, N_MOD, 1, D), lat.shape)
    ms_all = jnp.concatenate([lat, cx], axis=2)
    cos_t, sin_t = _rope_tables()
    mask_t = _swa_mask_table()
    na_bias = _na_bias_source(na_rpb)

    h = _normmod(xs, norm1_g[0], ms_all[0])
    for l in range(depth):
        last = l + 1 == depth
        ms = ms_all[l]
        wcat, wdt = _prep_w_in(w_in[l])
        p = _matmul(h, wcat, BF16, 1024, "in_proj")
        dtr = _matmul(h, wdt, F32, Q, "dt_proj")
        xa = _conv_silu(p, conv_w[l], conv_b[l])
        ya = _ssd(xa, p, dtr, dt_bias[l], a_log[l], ssd_d[l], ssd_norm_g[l], not last)
        yb = _na(p, na_bias[l], not last)
        yc = _swa(p, swa_sink[l], cos_t, sin_t, mask_t, not last)
        rows, tr = MERGE_TILES[last]
        xs = _merge(xs, ya, yb, yc, p, ms, w_o_ssd[l].astype(BF16), w_o_na[l].astype(BF16),
                    w_o_swa[l].astype(BF16), w_out[l].astype(BF16), tr, rows // tr)
        w1, w2 = w_ff1[l].astype(BF16), w_ff2[l].astype(BF16)
        rows, tr = FFN_TILES[last]
        if last:
            return _ffn(xs, norm2_g[l], ms, w1, w2, tr, rows // tr, final_g)
        xs, h = _ffn(xs, norm2_g[l], ms, w1, w2, tr, rows // tr, norm1_g[l + 1], ms_all[l + 1])
```

```python
import functools

import numpy as np
import jax
import jax.numpy as jnp
from jax import lax
from jax.experimental import pallas as pl
from jax.experimental.pallas import tpu as pltpu

F32 = jnp.float32
BF16 = jnp.bfloat16
HIGHEST = lax.Precision.HIGHEST

D = 1024
L = 2048
NC = 256
T = L + NC
GRID_W = 64
HEAD_DIM = 64
ROWS = L // GRID_W
EPS = 1e-6
N_MOD = 6

SSD_INNER = 2 * D
SSD_HEADS = 32
SSD_GROUPS = 8
SSD_HPG = 4
SSD_STATE = 128
SSD_XBC = SSD_INNER + 2 * SSD_GROUPS * SSD_STATE
Q = 128
NCH = T // Q

NA_HEADS = 16
NA_KH = 8
NA_KW = 16
NA_QROWS = 2
NA_KROWS = 10
NA_NQ = NA_QROWS * GRID_W
NA_NK = NA_KROWS * GRID_W
NA_CASES = 5
NA_NOFF = 2 * NA_KH - 1

SWA_HEADS = 16
SWA_KV = 4
SWA_W = 128
ROPE_BASE = 10000.0
D_FF = 4 * D

R_XBC, R_DT, R_NAK, R_NAV, R_SWAK, R_SWAV, R_Z, R_NAQ, R_SWAQ, R_GATE, R_END = (
    0, 4096, 4160, 5184, 6208, 6464, 6720, 8768, 9792, 10816, 13888)
C_XBC, C_NAK, C_NAV, C_SWAK, C_SWAV, C_Z, C_NAQ, C_SWAQ, C_GATE, NCAT = (
    0, 4096, 5120, 6144, 6656, 7168, 9216, 10240, 11264, 14336)

NEG = -1e30
LOG2E = float(np.log2(np.e))
QSCALE = 0.125 * LOG2E
VMEM_LIMIT = 48 * 1024 * 1024


def _cparams(n_axes):
    return pltpu.CompilerParams(dimension_semantics=("parallel",) * n_axes,
                                vmem_limit_bytes=VMEM_LIMIT)


def _sigmoid(x):
    return 0.5 * jnp.tanh(0.5 * x) + 0.5


def _silu(x):
    h = 0.5 * x
    return h * jnp.tanh(h) + h


def _softplus(x):
    return jnp.maximum(x, 0.0) + jnp.log1p(jnp.exp(-jnp.abs(x)))


def _nt(a, b):
    return lax.dot_general(a, b, (((1,), (1,)), ((), ())), preferred_element_type=F32)


def _tn(a, b):
    return lax.dot_general(a, b, (((0,), (0,)), ((), ())), preferred_element_type=F32)


def _dot(a, b):
    return jnp.dot(a, b, preferred_element_type=F32)


def _chunk_loop(n, unroll, body):
    def wrapped(c, carry):
        body(c, pl.multiple_of(c * Q, Q))
        return carry
    lax.fori_loop(0, n, wrapped, 0, unroll=unroll)


def _ada_kernel(c_ref, w_ref, b_ref, o_ref):
    c = c_ref[...]
    s = _silu(c).astype(BF16)
    o_ref[...] = _dot(s, w_ref[...].astype(BF16)) + b_ref[...]


def _ada_mod(cvec, ada_w, ada_b):
    depth = ada_w.shape[0]
    r = cvec.shape[0]
    return pl.pallas_call(
        _ada_kernel,
        out_shape=jax.ShapeDtypeStruct((depth, r, N_MOD * D), F32),
        grid=(depth, N_MOD),
        in_specs=[pl.BlockSpec((r, D), lambda l, n: (0, 0)),
                  pl.BlockSpec((None, D, D), lambda l, n: (l, 0, n)),
                  pl.BlockSpec((None, 1, D), lambda l, n: (l, 0, n))],
        out_specs=pl.BlockSpec((None, r, D), lambda l, n: (l, 0, n)),
        compiler_params=_cparams(2),
        name="ada_mod",
    )(cvec, ada_w, ada_b.reshape(depth, 1, N_MOD * D))


TR = 256
NRT = T // TR
MERGE_TILES = ((T, 384), (L, 512))
FFN_TILES = ((T, 256), (L, 512))


def _mod_spec(which):
    return pl.BlockSpec((None, None, None, 1, D), lambda b, j: (b, j // (L // TR), which, 0, 0))


def _mod_pair(which, tr):
    if L % tr == 0:
        kinds = [lambda j: j * tr // L] * 2
    else:
        kinds = [lambda j: 0, lambda j: 1]
    return [pl.BlockSpec((None, None, None, 1, D), lambda b, j, kind=kind: (b, kind(j), which, 0, 0))
            for kind in kinds]


def _pick_mod(lat_ref, ctx_ref, tr):
    if L % tr == 0:
        return lat_ref[...]
    row = lax.broadcasted_iota(jnp.int32, (tr, 1), 0) + pl.program_id(1) * tr
    return jnp.where(row >= L, ctx_ref[...], lat_ref[...])


def _rms(x, g):
    return x * lax.rsqrt(jnp.mean(x * x, axis=-1, keepdims=True) + EPS) * g


def _normmod_kernel(x_ref, g_ref, sh_ref, sc_ref, o_ref):
    y = _rms(x_ref[...], g_ref[...])
    o_ref[...] = (y * (1.0 + sc_ref[...]) + sh_ref[...]).astype(o_ref.dtype)


def _normmod(xs, g, ms):
    b = xs.shape[0]
    return pl.pallas_call(
        _normmod_kernel,
        out_shape=jax.ShapeDtypeStruct((b, T, D), BF16),
        grid=(b, NRT),
        in_specs=[pl.BlockSpec((None, TR, D), lambda b, j: (b, j, 0)),
                  pl.BlockSpec((1, D), lambda b, j: (0, 0)),
                  _mod_spec(0), _mod_spec(1)],
        out_specs=pl.BlockSpec((None, TR, D), lambda b, j: (b, j, 0)),
        compiler_params=_cparams(2),
        name="normmod",
    )(xs, g.reshape(1, D), ms, ms)


def _mm_kernel(x_ref, w_ref, o_ref):
    o_ref[...] = _dot(x_ref[...], w_ref[...]).astype(o_ref.dtype)


def _matmul(x, w, out_dtype, tn, name):
    b, t, k = x.shape
    n = w.shape[1]
    return pl.pallas_call(
        _mm_kernel,
        out_shape=jax.ShapeDtypeStruct((b, t, n), out_dtype),
        grid=(n // tn, b),
        in_specs=[pl.BlockSpec((None, t, k), lambda n_, b_: (b_, 0, 0)),
                  pl.BlockSpec((k, tn), lambda n_, b_: (0, n_))],
        out_specs=pl.BlockSpec((None, t, tn), lambda n_, b_: (b_, 0, n_)),
        compiler_params=_cparams(2),
        name=name,
    )(x, w)


CW = 512
CONV_RC = 256
PAD_LAT = 8
PAD_CTX = 16 + L


def _conv_kernel(x_ref, w_ref, b_ref, o_ref, pad_ref):
    z8 = jnp.zeros((8, CW), F32)
    pad_ref[0:8, :] = z8
    pad_ref[PAD_LAT + L:PAD_CTX, :] = z8
    pad_ref[PAD_CTX + NC:PAD_CTX + NC + 8, :] = z8
    pad_ref[PAD_LAT:PAD_LAT + L, :] = x_ref[0:L, :].astype(F32)
    pad_ref[PAD_CTX:PAD_CTX + NC, :] = x_ref[L:T, :].astype(F32)
    w = w_ref[...]
    bias = b_ref[...]
    for c in range(T // CONV_RC):
        r0 = c * CONV_RC
        base = (PAD_LAT if r0 < L else PAD_CTX - L) + r0
        win = pad_ref[base - 8:base + CONV_RC + 8, :]
        acc = bias + win[8:8 + CONV_RC] * w[2:3, :]
        for k in (0, 1, 3, 4):
            shifted = pltpu.roll(win, (2 - k) % (CONV_RC + 16), axis=0)
            acc = acc + shifted[8:8 + CONV_RC] * w[k:k + 1, :]
        o_ref[r0:r0 + CONV_RC, :] = _silu(acc).astype(o_ref.dtype)


def _conv_silu(p, conv_w, conv_b):
    b = p.shape[0]
    w8 = jnp.concatenate([conv_w, jnp.zeros((3, SSD_XBC), F32)], axis=0)
    return pl.pallas_call(
        _conv_kernel,
        out_shape=jax.ShapeDtypeStruct((b, T, SSD_XBC), BF16),
        grid=(b, SSD_XBC // CW),
        in_specs=[pl.BlockSpec((None, T, CW), lambda b_, n: (b_, 0, n)),
                  pl.BlockSpec((8, CW), lambda b_, n: (0, n)),
                  pl.BlockSpec((1, CW), lambda b_, n: (0, n))],
        out_specs=pl.BlockSpec((None, T, CW), lambda b_, n: (b_, 0, n)),
        scratch_shapes=[pltpu.VMEM((T + 24, CW), F32)],
        compiler_params=_cparams(2),
        name="conv_silu",
    )(p, w8, conv_b.reshape(1, SSD_XBC))


GW = SSD_HPG * HEAD_DIM


def _split3(x):
    hi = x.astype(BF16)
    r = x - hi.astype(F32)
    mid = r.astype(BF16)
    lo = (r - mid.astype(F32)).astype(BF16)
    return hi, mid, lo


def _cum_constants():
    ii = lax.broadcasted_iota(jnp.int32, (Q, Q), 0)
    jj = lax.broadcasted_iota(jnp.int32, (Q, Q), 1)
    tril = jnp.where(ii >= jj, 1.0, 0.0).astype(BF16)
    triu = jnp.where(ii <= jj, 1.0, 0.0).astype(BF16)
    return jnp.concatenate([jnp.concatenate([tril] * 3, axis=1),
                            jnp.concatenate([triu] * 3, axis=1)], axis=0)


def _ssd_prep_kernel(dt_ref, bias_ref, alog_ref, accg_o, wts_o, eacc_o, rowt_o, dtt_o):
    lcat = _cum_constants()
    isf = (lax.broadcasted_iota(jnp.int32, (1, Q), 1) % 8) < 4
    bias = bias_ref[...]
    a_r = -jnp.exp(alog_ref[...])

    def body(c, t0):
        rows = pl.ds(t0, Q)
        dt = _softplus(dt_ref[rows, :] + bias)
        both = _dot(lcat, jnp.concatenate(_split3(dt * a_r), axis=0))
        acc = jnp.where(isf, both[0:Q], both[Q:2 * Q])
        tot = jnp.where(isf, both[Q - 1:Q], both[Q:Q + 1])
        for g in range(SSD_GROUPS):
            accg_o[g, rows, :] = acc if g == 0 else pltpu.roll(acc, Q - 8 * g, axis=1)
        wts_o[rows, :] = dt * jnp.exp(tot - acc)
        eacc_o[rows, :] = jnp.exp(acc)
        dt_t = dt.T
        rowt_o[:, rows] = acc.T - jnp.log(dt_t)
        dtt_o[:, rows] = dt_t

    _chunk_loop(NCH, 3, body)


def _ssd_kernel(x_ref, b_ref, c_ref, z_ref, acc_ref, wts_ref, eacc_ref, rowt_ref, dtt_ref, dsk_ref, ng_ref,
                o_ref, xw_s, ec_s, big_s, sin_s, cb_s, y_s, *, n_out):
    g = pl.program_id(1)
    ii = lax.broadcasted_iota(jnp.int32, (Q, Q), 0)
    jj = lax.broadcasted_iota(jnp.int32, (Q, Q), 1)
    low = ii > jj
    up = ii < jj
    lo = jj < HEAD_DIM
    ek = lax.broadcasted_iota(jnp.int32, (Q, 2 * GW), 0)
    ech = lax.broadcasted_iota(jnp.int32, (Q, 2 * GW), 1)
    esel = jnp.where(ek == g * 8 + ech // HEAD_DIM, 1.0, 0.0).astype(BF16)
    esel2 = jnp.concatenate([esel] * 2, axis=0)
    zero_x = jnp.zeros((Q, Q), BF16)
    rows = lambda t0: pl.ds(t0, Q)

    def stage_expand(c, t0):
        w2 = _split3(wts_ref[rows(t0), :])[:2]
        e2 = _split3(eacc_ref[rows(t0), :])[:2]
        lhs = jnp.concatenate([jnp.concatenate(w2, axis=1), jnp.concatenate(e2, axis=1)], axis=0)
        both = _dot(lhs, esel2)
        xs = x_ref[rows(t0), :].astype(F32)
        xw_s[rows(t0), :] = (jnp.concatenate([xs, xs], axis=1) * both[0:Q]).astype(BF16)
        ec_s[rows(t0), :] = both[Q:2 * Q]

    _chunk_loop(NCH, 3, stage_expand)

    def stage_upd(c, t0):
        big_s[c] = _tn(b_ref[rows(t0), :], xw_s[rows(t0), :])

    _chunk_loop(NCH, 6, stage_upd)

    s = jnp.zeros((SSD_STATE, GW), F32)
    for c in list(range(L // Q, NCH)) + list(range(L // Q)):
        sin_s[c, :, 0:GW] = s.astype(BF16)
        s = s * ec_s[c * Q + Q - 1:c * Q + Q, 0:GW] + big_s[c, :, 0:GW]
    s = jnp.zeros((SSD_STATE, GW), F32)
    for c in reversed(range(NCH)):
        sin_s[c, :, GW:2 * GW] = s.astype(BF16)
        s = s * ec_s[c * Q:c * Q + 1, GW:2 * GW] + big_s[c, :, GW:2 * GW]

    def stage_cb(c, t0):
        cc = c_ref[rows(t0), :]
        cb_s[rows(t0), :] = _nt(cc, b_ref[rows(t0), :])
        big_s[c] = _dot(cc, sin_s[c])

    _chunk_loop(n_out, 6 if n_out % 6 == 0 else 4, stage_cb)

    dsk = dsk_ref[...]
    ng = ng_ref[...]

    def colb(v, k):
        return jnp.broadcast_to(v[:, k:k + 1], (Q, Q))

    def stage_y(c, t0):
        xb = x_ref[rows(t0), :]
        acc = acc_ref[rows(t0), :]
        rowt = rowt_ref[:, rows(t0)]
        dt_t = dtt_ref[:, rows(t0)]
        ec = ec_s[rows(t0), :]
        cb = cb_s[rows(t0), :]
        yoff = big_s[c]
        ys = []
        for pr in range(2):
            ms = []
            for hh in range(2):
                r = 2 * pr + hh
                diag = jnp.log(dt_t[r:r + 1, :] + dt_t[4 + r:5 + r, :])
                seg = jnp.where(low, colb(acc, r) - rowt[r:r + 1, :],
                                jnp.where(up, colb(acc, 4 + r) - rowt[4 + r:5 + r, :], diag))
                ms.append((cb * jnp.exp(seg)).astype(BF16))
            xp = xb[:, pr * Q:(pr + 1) * Q]
            xstack = jnp.concatenate([jnp.where(lo, xp, zero_x), jnp.where(lo, zero_x, xp)], axis=0)
            y = _dot(jnp.concatenate(ms, axis=1), xstack)
            y = (y + ec[:, pr * Q:(pr + 1) * Q] * yoff[:, pr * Q:(pr + 1) * Q]
                 + ec[:, GW + pr * Q:GW + (pr + 1) * Q] * yoff[:, GW + pr * Q:GW + (pr + 1) * Q])
            ys.append(y)
        y_s[rows(t0), :] = jnp.concatenate(ys, axis=1)

    _chunk_loop(n_out, 2, stage_y)

    def stage_out(c, t0):
        y = y_s[rows(t0), :] + dsk * x_ref[rows(t0), :].astype(F32)
        zf = z_ref[rows(t0), :].astype(F32)
        u = y * _silu(zf)
        o_ref[rows(t0), :] = _rms(u, ng).astype(o_ref.dtype)

    _chunk_loop(n_out, 3 if n_out % 3 == 0 else 4, stage_out)


def _slot_order(v):
    s = v.reshape(2, SSD_GROUPS, SSD_HPG).transpose(1, 0, 2).reshape(1, 2 * SSD_HEADS)
    return jnp.pad(s, ((0, 0), (0, Q - 2 * SSD_HEADS)))


def _ssd(xa, p, dtr, dt_bias, a_log, ssd_d, ssd_norm_g, emit_ctx):
    b = xa.shape[0]
    t_out = T if emit_ctx else L
    row = pl.BlockSpec((None, T, Q), lambda b_: (b_, 0, 0))
    col = pl.BlockSpec((None, Q, T), lambda b_: (b_, 0, 0))
    one = pl.BlockSpec((1, Q), lambda b_: (0, 0))
    accg, wts, eacc, rowt, dtt = pl.pallas_call(
        _ssd_prep_kernel,
        out_shape=([jax.ShapeDtypeStruct((b, SSD_GROUPS, T, Q), F32)] + [jax.ShapeDtypeStruct((b, T, Q), F32)] * 2
                   + [jax.ShapeDtypeStruct((b, Q, T), F32)] * 2),
        grid=(b,),
        in_specs=[row, one, one],
        out_specs=[pl.BlockSpec((None, SSD_GROUPS, T, Q), lambda b_: (b_, 0, 0, 0)), row, row, col, col],
        compiler_params=_cparams(1),
        name="ssd_prep",
    )(dtr, _slot_order(dt_bias), _slot_order(a_log))

    dsk = jnp.repeat(ssd_d, HEAD_DIM).reshape(1, SSD_INNER)
    ng = ssd_norm_g.reshape(1, SSD_INNER)
    xoff, boff, coff, zoff = 0, SSD_INNER // Q, (SSD_INNER + SSD_GROUPS * SSD_STATE) // Q, C_Z // GW
    full = pl.BlockSpec((None, T, Q), lambda b_, g: (b_, 0, 0))
    slot = pl.BlockSpec((None, 8, T), lambda b_, g: (b_, g, 0))
    return pl.pallas_call(
        functools.partial(_ssd_kernel, n_out=t_out // Q),
        out_shape=jax.ShapeDtypeStruct((b, t_out, SSD_INNER), BF16),
        grid=(b, SSD_GROUPS),
        in_specs=[pl.BlockSpec((None, T, GW), lambda b_, g: (b_, 0, xoff + g)),
                  pl.BlockSpec((None, T, Q), lambda b_, g: (b_, 0, boff + g)),
                  pl.BlockSpec((None, T, Q), lambda b_, g: (b_, 0, coff + g)),
                  pl.BlockSpec((None, T, GW), lambda b_, g: (b_, 0, zoff + g)),
                  pl.BlockSpec((None, None, T, Q), lambda b_, g: (b_, g, 0, 0)), full, full, slot, slot,
                  pl.BlockSpec((1, GW), lambda b_, g: (0, g)),
                  pl.BlockSpec((1, GW), lambda b_, g: (0, g))],
        out_specs=pl.BlockSpec((None, t_out, GW), lambda b_, g: (b_, 0, g)),
        scratch_shapes=[pltpu.VMEM((T, 2 * GW), BF16),
                        pltpu.VMEM((T, 2 * GW), F32),
                        pltpu.VMEM((NCH, SSD_STATE, 2 * GW), F32),
                        pltpu.VMEM((NCH, SSD_STATE, 2 * GW), BF16),
                        pltpu.VMEM((T, Q), F32),
                        pltpu.VMEM((T, GW), F32)],
        compiler_params=_cparams(2),
        name="ssd",
    )(xa, xa, xa, p, accg, wts, eacc, rowt, dtt, dsk, ng)


def _softmax_t(parts, extra=None):
    mx = None
    for s in parts:
        r = jnp.max(s, axis=0, keepdims=True)
        mx = r if mx is None else jnp.maximum(mx, r)
    if extra is not None:
        mx = jnp.maximum(mx, extra)
    den = None
    es = []
    for s in parts:
        e = jnp.exp2(s - mx)
        d = jnp.sum(e, axis=0, keepdims=True)
        den = d if den is None else den + d
        es.append(e.astype(BF16))
    if extra is not None:
        den = den + jnp.exp2(extra - mx)
    return es, 1.0 / den


def _pv_t(vts, es):
    out = None
    for vt, e in zip(vts, es):
        o = _dot(vt, e)
        out = o if out is None else out + o
    return out


def _na_row_offsets():
    table = []
    for j in (0, 1, 2, ROWS // NA_QROWS - 2, ROWS // NA_QROWS - 1):
        start = int(np.clip(NA_QROWS * j - NA_KH // 2, 0, ROWS - NA_KROWS))
        per_row = []
        for i in range(NA_KROWS):
            kr = start + i
            offs = []
            for qr in range(NA_QROWS):
                r = NA_QROWS * j + qr
                rs = int(np.clip(r - NA_KH // 2, 0, ROWS - NA_KH))
                offs.append(kr - r + NA_KH - 1 if rs <= kr < rs + NA_KH else NA_NOFF)
            per_row.append(tuple(offs))
        table.append(per_row)
    return table


def _na_kernel(q_ref, k_ref, v_ref, src_ref, o_ref, vt_ref, bias_ref, sa_ref, sb_ref, pa_ref, pb_ref, ra_ref,
               rb_ref, *, emit_ctx):
    @pl.when(pl.program_id(1) == 0)
    def _():
        lo_tile = lax.broadcasted_iota(jnp.int32, (GRID_W, Q), 1) < GRID_W
        for ci, per_row in enumerate(_na_row_offsets()):
            for i, (off0, off1) in enumerate(per_row):
                for hh in range(2):
                    bias_ref[ci, i * GRID_W:(i + 1) * GRID_W, hh * Q:(hh + 1) * Q] = jnp.where(
                        lo_tile, src_ref[hh, off0], src_ref[hh, off1])

    lane = lax.broadcasted_iota(jnp.int32, (NA_NQ, Q), 1)
    lo = lane < HEAD_DIM
    top = lax.broadcasted_iota(jnp.int32, (Q, NA_NQ), 0) < HEAD_DIM
    zero = jnp.zeros((NA_NQ, Q), BF16)

    def vt_body(c, t0):
        vt_ref[:, pl.ds(t0, Q)] = v_ref[pl.ds(t0, Q), :].astype(F32).T.astype(BF16)

    _chunk_loop(NCH, 6, vt_body)

    kc = k_ref[L:T, :]
    vtc = vt_ref[:, L:T]

    def qpair(q0):
        q = (q_ref[pl.ds(q0, NA_NQ), :].astype(F32) * QSCALE).astype(BF16)
        return jnp.concatenate([jnp.where(lo, q, zero), jnp.where(lo, zero, q)], axis=0)

    def finish(q0, ot, rden):
        ot = ot * rden
        w = jnp.where(top, ot[:, 0:NA_NQ], ot[:, NA_NQ:2 * NA_NQ])
        o_ref[pl.ds(q0, NA_NQ), :] = w.T.astype(o_ref.dtype)

    def block_params(j):
        q0 = pl.multiple_of(j * NA_NQ, NA_NQ)
        start = jnp.clip(NA_QROWS * j - NA_KH // 2, 0, ROWS - NA_KROWS)
        koff = pl.multiple_of(start * GRID_W, 2 * GRID_W)
        last = ROWS // NA_QROWS - 1
        case = (jnp.where(j >= 1, 1, 0) + jnp.where(j >= 2, 1, 0)
                + jnp.where(j >= last - 1, 1, 0) + jnp.where(j >= last, 1, 0))
        return q0, koff, case

    def qk(j, s_ref):
        q0, koff, case = block_params(j)
        qp = qpair(q0)
        s_ref[0:NA_NK, :] = _nt(k_ref[pl.ds(koff, NA_NK), :], qp) + bias_ref[case]
        s_ref[NA_NK:NA_NK + NC, :] = _nt(kc, qp)

    def soft(s_ref, p_ref, r_ref):
        s = s_ref[...]
        e = jnp.exp2(s - jnp.max(s, axis=0, keepdims=True))
        r_ref[...] = jnp.broadcast_to(1.0 / jnp.sum(e, axis=0, keepdims=True), r_ref.shape)
        p_ref[...] = e.astype(BF16)

    def pv(j, p_ref, r_ref):
        q0, koff, _ = block_params(j)
        ot = (_dot(vt_ref[:, pl.ds(koff, NA_NK)], p_ref[0:NA_NK, :])
              + _dot(vtc, p_ref[NA_NK:NA_NK + NC, :]))
        finish(q0, ot, r_ref[0:1, :])

    nblk = L // NA_NQ
    qk(0, sa_ref)
    qk(1, sb_ref)
    soft(sa_ref, pa_ref, ra_ref)

    def body(m, carry):
        pv(2 * m - 2, pa_ref, ra_ref)
        qk(2 * m, sa_ref)
        soft(sb_ref, pb_ref, rb_ref)
        pv(2 * m - 1, pb_ref, rb_ref)
        qk(2 * m + 1, sb_ref)
        soft(sa_ref, pa_ref, ra_ref)
        return carry

    lax.fori_loop(1, nblk // 2, body, 0)
    pv(nblk - 2, pa_ref, ra_ref)
    soft(sb_ref, pb_ref, rb_ref)
    pv(nblk - 1, pb_ref, rb_ref)

    if emit_ctx:
        ctx_blocks = []
        for cbk in range(NC // NA_NQ):
            q0 = L + cbk * NA_NQ
            ctx_blocks.append((q0, _nt(kc, qpair(q0))))
        ctx_soft = [(q0,) + _softmax_t([s]) for q0, s in ctx_blocks]
        for q0, es, rden in ctx_soft:
            finish(q0, _pv_t([vtc], es), rden)


def _na_bias_source(rpb_all):
    depth = rpb_all.shape[0]
    ck = np.arange(GRID_W)[:, None]
    cq = (np.arange(Q) % GRID_W)[None, :]
    col_oh = ((ck - cq + NA_KW - 1)[None] == np.arange(2 * NA_KW - 1)[:, None, None]).astype(np.float32)
    cs = np.clip(cq - NA_KW // 2, 0, GRID_W - NA_KW)
    col_ok = (ck >= cs) & (ck < cs + NA_KW)
    rp = rpb_all.reshape(depth, NA_HEADS // 2, 2, NA_NOFF, 2 * NA_KW - 1)
    src = jnp.einsum("lperd,dbz->lperbz", rp, col_oh, precision=HIGHEST)
    src = jnp.where(col_ok, src * LOG2E, NEG)
    return jnp.concatenate([src, jnp.full_like(src[:, :, :, :1], NEG)], axis=3)


def _na(p, bias_src, emit_ctx):
    b = p.shape[0]
    t_out = T if emit_ctx else L
    qo, ko, vo = C_NAQ // Q, C_NAK // Q, C_NAV // Q
    return pl.pallas_call(
        functools.partial(_na_kernel, emit_ctx=emit_ctx),
        out_shape=jax.ShapeDtypeStruct((b, t_out, D), BF16),
        grid=(NA_HEADS // 2, b),
        in_specs=[pl.BlockSpec((None, T, Q), lambda h, b_: (b_, 0, qo + h)),
                  pl.BlockSpec((None, T, Q), lambda h, b_: (b_, 0, ko + h)),
                  pl.BlockSpec((None, T, Q), lambda h, b_: (b_, 0, vo + h)),
                  pl.BlockSpec((None, 2, NA_NOFF + 1, GRID_W, Q), lambda h, b_: (h, 0, 0, 0, 0))],
        out_specs=pl.BlockSpec((None, t_out, Q), lambda h, b_: (b_, 0, h)),
        scratch_shapes=[pltpu.VMEM((Q, T), BF16),
                        pltpu.VMEM((NA_CASES, NA_NK, 2 * NA_NQ), F32),
                        pltpu.VMEM((NA_NK + NC, 2 * NA_NQ), F32),
                        pltpu.VMEM((NA_NK + NC, 2 * NA_NQ), F32),
                        pltpu.VMEM((NA_NK + NC, 2 * NA_NQ), BF16),
                        pltpu.VMEM((NA_NK + NC, 2 * NA_NQ), BF16),
                        pltpu.VMEM((8, 2 * NA_NQ), F32),
                        pltpu.VMEM((8, 2 * NA_NQ), F32)],
        compiler_params=pltpu.CompilerParams(dimension_semantics=("parallel", "arbitrary"),
                                             vmem_limit_bytes=VMEM_LIMIT),
        name="na_attn",
    )(p, p, p, bias_src)


SWA_KWIN = 3 * SWA_W
SWA_QB = 128
SWA_STACK = 4 * SWA_QB
ROPE_RC = 256


def _swap_rotary_halves(t, lane):
    quarter = HEAD_DIM // 4
    a = pltpu.roll(t, Q - quarter, axis=1)
    b = pltpu.roll(t, quarter, axis=1)
    return jnp.where((lane % (2 * quarter)) < quarter, a, b)


def _swa_kernel(sink_ref, q_ref, k_ref, v_ref, cos_ref, sin_ref, mask_ref, o_ref, krot_ref, vt_ref,
                sa_ref, sb_ref, pa_ref, pb_ref, ra_ref, rb_ref, *, emit_ctx):
    kp = pl.program_id(1)
    lane = lax.broadcasted_iota(jnp.int32, (SWA_QB, Q), 1)
    lo = lane < HEAD_DIM
    zero = jnp.zeros((SWA_QB, Q), BF16)
    col = lax.broadcasted_iota(jnp.int32, (1, SWA_STACK), 1)
    eye4 = jnp.where(lax.broadcasted_iota(jnp.int32, (SWA_STACK, Q), 0) % SWA_QB
                     == lax.broadcasted_iota(jnp.int32, (SWA_STACK, Q), 1), 1.0, 0.0).astype(BF16)

    def rope(t, r0, n):
        lane_n = lax.broadcasted_iota(jnp.int32, (n, Q), 1)
        return t * cos_ref[pl.ds(r0, n), :] + _swap_rotary_halves(t, lane_n) * sin_ref[pl.ds(r0, n), :]

    def krot_body(i, carry):
        r0 = pl.multiple_of(i * ROPE_RC, ROPE_RC)
        for gl in range(2):
            t = k_ref[pl.ds(r0, ROPE_RC), gl * Q:(gl + 1) * Q].astype(F32)
            krot_ref[pl.ds(r0, ROPE_RC), gl * Q:(gl + 1) * Q] = rope(t, r0, ROPE_RC).astype(BF16)
        return carry

    lax.fori_loop(0, L // ROPE_RC, krot_body, 0, unroll=2)

    def vt_body(c, t0):
        for gl in range(2):
            vt = v_ref[pl.ds(t0, Q), gl * Q:(gl + 1) * Q].astype(F32).T
            vt_ref[gl, :, pl.ds(t0, Q)] = vt[0:HEAD_DIM].astype(BF16)

    _chunk_loop(NCH, 6, vt_body)

    def stack_q(qa, qb):
        return jnp.concatenate([jnp.where(lo, qa, zero), jnp.where(lo, zero, qa),
                                jnp.where(lo, qb, zero), jnp.where(lo, zero, qb)], axis=0)

    def sink_row(gl):
        base = kp * 8 + gl * 4
        row = jnp.where(col < SWA_QB, sink_ref[base],
                        jnp.where(col < 2 * SWA_QB, sink_ref[base + 1],
                                  jnp.where(col < 3 * SWA_QB, sink_ref[base + 2], sink_ref[base + 3])))
        return row * LOG2E

    def finish(q0, gl, ot, rden):
        ot = ot * rden
        pa = jnp.concatenate([ot[:, 0:SWA_QB], ot[:, SWA_QB:2 * SWA_QB]], axis=0).T
        pb = jnp.concatenate([ot[:, 2 * SWA_QB:3 * SWA_QB], ot[:, 3 * SWA_QB:4 * SWA_QB]], axis=0).T
        o_ref[pl.ds(q0, SWA_QB), gl * 2 * Q:(gl + 1) * 2 * Q] = jnp.concatenate([pa, pb], axis=1).astype(o_ref.dtype)

    def block_params(i):
        q0 = pl.multiple_of(i * SWA_QB, SWA_QB)
        ws = pl.multiple_of(jnp.clip((i - 1) * SWA_W, 0, L - SWA_KWIN), SWA_W)
        case = jnp.where(i >= 1, 1, 0) + jnp.where(i >= L // SWA_QB - 1, 1, 0)
        return q0, ws, case

    def qk(i, gl, s_ref):
        q0, ws, case = block_params(i)
        qa = rope(q_ref[pl.ds(q0, SWA_QB), gl * 2 * Q:gl * 2 * Q + Q].astype(F32), q0, SWA_QB) * QSCALE
        qb = rope(q_ref[pl.ds(q0, SWA_QB), gl * 2 * Q + Q:(gl + 1) * 2 * Q].astype(F32), q0, SWA_QB) * QSCALE
        qs = stack_q(qa.astype(BF16), qb.astype(BF16))
        a_loc = jnp.concatenate([krot_ref[pl.ds(ws, SWA_KWIN), gl * Q:(gl + 1) * Q], mask_ref[case]], axis=1)
        s_ref[0:SWA_KWIN, :] = _nt(a_loc, jnp.concatenate([qs, eye4], axis=1))
        s_ref[SWA_KWIN:SWA_KWIN + NC, :] = _nt(k_ref[L:T, gl * Q:(gl + 1) * Q], qs)

    def soft(gl, s_ref, p_ref, r_ref):
        s = s_ref[...]
        sink = sink_row(gl)
        mx = jnp.maximum(jnp.max(s, axis=0, keepdims=True), sink)
        e = jnp.exp2(s - mx)
        den = jnp.sum(e, axis=0, keepdims=True) + jnp.exp2(sink - mx)
        r_ref[...] = jnp.broadcast_to(1.0 / den, r_ref.shape)
        p_ref[...] = e.astype(BF16)

    def pv(i, gl, p_ref, r_ref):
        q0, ws, _ = block_params(i)
        ot = (_dot(vt_ref[gl, :, pl.ds(ws, SWA_KWIN)], p_ref[0:SWA_KWIN, :])
              + _dot(vt_ref[gl, :, L:T], p_ref[SWA_KWIN:SWA_KWIN + NC, :]))
        finish(q0, gl, ot, r_ref[0:1, :])

    nblk = L // SWA_QB
    qk(0, 0, sa_ref)
    qk(0, 1, sb_ref)
    soft(0, sa_ref, pa_ref, ra_ref)

    def body(i, carry):
        pv(i - 1, 0, pa_ref, ra_ref)
        qk(i, 0, sa_ref)
        soft(1, sb_ref, pb_ref, rb_ref)
        pv(i - 1, 1, pb_ref, rb_ref)
        qk(i, 1, sb_ref)
        soft(0, sa_ref, pa_ref, ra_ref)
        return carry

    lax.fori_loop(1, nblk, body, 0)
    pv(nblk - 1, 0, pa_ref, ra_ref)
    soft(1, sb_ref, pb_ref, rb_ref)
    pv(nblk - 1, 1, pb_ref, rb_ref)

    for cbk in range(NC // SWA_QB if emit_ctx else 0):
        q0 = L + cbk * SWA_QB
        scores = []
        for gl in range(2):
            qa = q_ref[q0:q0 + SWA_QB, gl * 2 * Q:gl * 2 * Q + Q].astype(F32) * QSCALE
            qb = q_ref[q0:q0 + SWA_QB, gl * 2 * Q + Q:(gl + 1) * 2 * Q].astype(F32) * QSCALE
            scores.append(_nt(k_ref[L:T, gl * Q:(gl + 1) * Q], stack_q(qa.astype(BF16), qb.astype(BF16))))
        soft = [_softmax_t([s], sink_row(gl)) for gl, s in enumerate(scores)]
        for gl, (es, rden) in enumerate(soft):
            finish(q0, gl, _pv_t([vt_ref[gl, :, L:T]], es), rden)


def _rope_tables():
    pos = np.arange(L)
    quarter = HEAD_DIM // 4
    lane = np.arange(Q) % HEAD_DIM
    inv = ROPE_BASE ** (-(lane % quarter).astype(np.float64) / quarter)
    p = np.where(lane[None, :] < HEAD_DIM // 2, (pos // GRID_W)[:, None], (pos % GRID_W)[:, None]).astype(np.float64)
    ang = p * inv[None, :]
    sign = np.where((lane % (2 * quarter)) < quarter, -1.0, 1.0)[None, :]
    return jnp.asarray(np.cos(ang), F32), jnp.asarray(np.sin(ang) * sign, F32)


def _swa_mask_table():
    key = np.arange(SWA_KWIN)[:, None]
    qry = np.arange(SWA_QB)[None, :]
    tabs = [np.where(np.abs(key - qry + delta) <= SWA_W, 0.0, NEG) for delta in (0, -SWA_W, -2 * SWA_W)]
    return jnp.asarray(np.stack(tabs), BF16)


def _swa(p, sink, cos_t, sin_t, mask_t, emit_ctx):
    b = p.shape[0]
    t_out = T if emit_ctx else L
    qo, ko, vo = C_SWAQ // 512, C_SWAK // 256, C_SWAV // 256
    return pl.pallas_call(
        functools.partial(_swa_kernel, emit_ctx=emit_ctx),
        out_shape=jax.ShapeDtypeStruct((b, t_out, D), BF16),
        grid=(b, SWA_KV // 2),
        in_specs=[pl.BlockSpec(memory_space=pltpu.SMEM),
                  pl.BlockSpec((None, T, 512), lambda b_, h: (b_, 0, qo + h)),
                  pl.BlockSpec((None, T, 256), lambda b_, h: (b_, 0, ko + h)),
                  pl.BlockSpec((None, T, 256), lambda b_, h: (b_, 0, vo + h)),
                  pl.BlockSpec((L, Q), lambda b_, h: (0, 0)),
                  pl.BlockSpec((L, Q), lambda b_, h: (0, 0)),
                  pl.BlockSpec((3, SWA_KWIN, SWA_QB), lambda b_, h: (0, 0, 0))],
        out_specs=pl.BlockSpec((None, t_out, 512), lambda b_, h: (b_, 0, h)),
        scratch_shapes=[pltpu.VMEM((L, 2 * Q), BF16),
                        pltpu.VMEM((2, HEAD_DIM, T), BF16),
                        pltpu.VMEM((SWA_KWIN + NC, SWA_STACK), F32),
                        pltpu.VMEM((SWA_KWIN + NC, SWA_STACK), F32),
                        pltpu.VMEM((SWA_KWIN + NC, SWA_STACK), BF16),
                        pltpu.VMEM((SWA_KWIN + NC, SWA_STACK), BF16),
                        pltpu.VMEM((8, SWA_STACK), F32),
                        pltpu.VMEM((8, SWA_STACK), F32)],
        compiler_params=_cparams(2),
        name="swa_attn",
    )(sink, p, p, p, cos_t, sin_t, mask_t)


def _merge_kernel(x_ref, ya_ref, yb_ref, yc_ref, g0_ref, g1_ref, g2_ref, mgl_ref, mgc_ref,
                  wa_ref, wb_ref, wc_ref, wo_ref, o_ref):
    m = (_sigmoid(g0_ref[...].astype(F32)) * _dot(ya_ref[...], wa_ref[...])
         + _sigmoid(g1_ref[...].astype(F32)) * _dot(yb_ref[...], wb_ref[...])
         + _sigmoid(g2_ref[...].astype(F32)) * _dot(yc_ref[...], wc_ref[...]))
    mg = _pick_mod(mgl_ref, mgc_ref, x_ref.shape[0])
    o_ref[...] = x_ref[...] + mg * _dot(m.astype(BF16), wo_ref[...])


def _const_spec(shape, single=True):
    mode = {"pipeline_mode": pl.Buffered(1)} if single else {}
    return pl.BlockSpec(shape, lambda b, j: (0,) * len(shape), **mode)


def _merge(xs, ya, yb, yc, p, ms, wa, wb, wc, wo, tr, n_tiles):
    b = xs.shape[0]
    go = C_GATE // D
    row = lambda w: pl.BlockSpec((None, tr, w), lambda b_, j: (b_, j, 0))
    gate = lambda k: pl.BlockSpec((None, tr, D), lambda b_, j: (b_, j, go + k))
    return pl.pallas_call(
        _merge_kernel,
        out_shape=jax.ShapeDtypeStruct((b, n_tiles * tr, D), F32),
        grid=(b, n_tiles),
        in_specs=[row(D), row(SSD_INNER), row(D), row(D), gate(0), gate(1), gate(2), *_mod_pair(2, tr),
                  _const_spec((SSD_INNER, D)), _const_spec((D, D)), _const_spec((D, D)), _const_spec((D, D))],
        out_specs=row(D),
        compiler_params=_cparams(2),
        name="merge",
    )(xs, ya, yb, yc, p, p, p, ms, ms, wa, wb, wc, wo)


FF_CH = 1024


def _ffn_kernel(*refs, final):
    if final:
        x_ref, g_ref, shl, shc, scl, scc, mgl, mgc, w1_ref, w2_ref, gn_ref, o_ref = refs
    else:
        (x_ref, g_ref, shl, shc, scl, scc, mgl, mgc, w1_ref, w2_ref, gn_ref, shnl, shnc, scnl, scnc,
         o_ref, h_ref) = refs
    tr = x_ref.shape[0]
    x = x_ref[...]
    h = (_rms(x, g_ref[...]) * (1.0 + _pick_mod(scl, scc, tr)) + _pick_mod(shl, shc, tr)).astype(BF16)
    acc = None
    for kf in range(D_FF // FF_CH):
        a = jnp.maximum(_dot(h, w1_ref[:, kf * FF_CH:(kf + 1) * FF_CH]), 0.0)
        o = _dot((a * a).astype(BF16), w2_ref[kf * FF_CH:(kf + 1) * FF_CH, :])
        acc = o if acc is None else acc + o
    y = x + _pick_mod(mgl, mgc, tr) * acc
    if final:
        o_ref[...] = _rms(y, gn_ref[...])
    else:
        o_ref[...] = y
        h_ref[...] = (_rms(y, gn_ref[...]) * (1.0 + _pick_mod(scnl, scnc, tr))
                      + _pick_mod(shnl, shnc, tr)).astype(h_ref.dtype)


def _ffn(xs, g, ms, w1, w2, tr, n_tiles, g_next, ms_next=None):
    b = xs.shape[0]
    final = ms_next is None
    row = pl.BlockSpec((None, tr, D), lambda b_, j: (b_, j, 0))
    in_specs = [row, _const_spec((1, D)), *_mod_pair(3, tr), *_mod_pair(4, tr), *_mod_pair(5, tr),
                _const_spec((D, D_FF), single=False), _const_spec((D_FF, D), single=False)]
    in_specs.append(_const_spec((1, D)))
    args = [xs, g.reshape(1, D), ms, ms, ms, ms, ms, ms, w1, w2, g_next.reshape(1, D)]
    out_shape = jax.ShapeDtypeStruct((b, n_tiles * tr, D), F32)
    out_specs = row
    if not final:
        in_specs += [*_mod_pair(0, tr), *_mod_pair(1, tr)]
        args += [ms_next, ms_next, ms_next, ms_next]
        out_shape = [out_shape, jax.ShapeDtypeStruct((b, n_tiles * tr, D), BF16)]
        out_specs = [row, row]
    return pl.pallas_call(
        functools.partial(_ffn_kernel, final=final),
        out_shape=out_shape,
        grid=(b, n_tiles),
        in_specs=in_specs,
        out_specs=out_specs,
        compiler_params=_cparams(2),
        name="ffn",
    )(*args)


def _prep_w_in(w):
    def dup(a):
        a = a.reshape(D, SWA_KV, 1, HEAD_DIM)
        return jnp.concatenate([a, a], axis=2).reshape(D, SWA_KV * 128)
    cat = jnp.concatenate([
        w[:, R_XBC:R_DT], w[:, R_NAK:R_NAV], w[:, R_NAV:R_SWAK], dup(w[:, R_SWAK:R_SWAV]),
        dup(w[:, R_SWAV:R_Z]), w[:, R_Z:R_NAQ], w[:, R_NAQ:R_SWAQ], w[:, R_SWAQ:R_GATE],
        w[:, R_GATE:R_END]], axis=1).astype(BF16)
    wd = w[:, R_DT:R_NAK].reshape(D, 2, SSD_GROUPS, SSD_HPG).transpose(0, 2, 1, 3).reshape(D, 2 * SSD_HEADS)
    wdt = jnp.concatenate([wd, jnp.zeros((D, Q - 2 * SSD_HEADS), F32)], axis=1).astype(BF16)
    return cat, wdt


def kernel(x, c, ctx, c_ctx, ada_w, ada_b, norm1_g, norm2_g, w_in, conv_w, conv_b, dt_bias, a_log, ssd_d,
           ssd_norm_g, na_rpb, swa_sink, w_o_ssd, w_o_na, w_o_swa, w_out, w_ff1, w_ff2, final_g):
    b = x.shape[0]
    depth = ada_w.shape[0]
    xs = jnp.concatenate([x, ctx], axis=1)
    nrow = -(-(b + 1) // 8) * 8
    cvec = jnp.concatenate([c, c_ctx[None, :], jnp.zeros((nrow - b - 1, D), F32)], axis=0)
    mod = _ada_mod(cvec, ada_w, ada_b)
    lat = mod[:, :b].reshape(depth, b, 1, N_MOD, 1, D)
    cx = jnp.broadcast_to(mod[:, b].reshape(depth, 1, 1, N_MOD, 1, D), lat.shape)
    ms_all = jnp.concatenate([lat, cx], axis=2)
    cos_t, sin_t = _rope_tables()
    mask_t = _swa_mask_table()
    na_bias = _na_bias_source(na_rpb)

    h = _normmod(xs, norm1_g[0], ms_all[0])
    for l in range(depth):
        last = l + 1 == depth
        ms = ms_all[l]
        wcat, wdt = _prep_w_in(w_in[l])
        p = _matmul(h, wcat, BF16, 1024, "in_proj")
        dtr = _matmul(h, wdt, F32, Q, "dt_proj")
        xa = _conv_silu(p, conv_w[l], conv_b[l])
        ya = _ssd(xa, p, dtr, dt_bias[l], a_log[l], ssd_d[l], ssd_norm_g[l], not last)
        yb = _na(p, na_bias[l], not last)
        yc = _swa(p, swa_sink[l], cos_t, sin_t, mask_t, not last)
        rows, tr = MERGE_TILES[last]
        xs = _merge(xs, ya, yb, yc, p, ms, w_o_ssd[l].astype(BF16), w_o_na[l].astype(BF16),
                    w_o_swa[l].astype(BF16), w_out[l].astype(BF16), tr, rows // tr)
        w1, w2 = w_ff1[l].astype(BF16), w_ff2[l].astype(BF16)
        rows, tr = FFN_TILES[last]
        if last:
            return _ffn(xs, norm2_g[l], ms, w1, w2, tr, rows // tr, final_g)
        xs, h = _ffn(xs, norm2_g[l], ms, w1, w2, tr, rows // tr, norm1_g[l + 1], ms_all[l + 1])
```

```python
import functools

import numpy as np
import jax
import jax.numpy as jnp
from jax import lax
from jax.experimental import pallas as pl
from jax.experimental.pallas import tpu as pltpu

F32 = jnp.float32
BF16 = jnp.bfloat16
HIGHEST = lax.Precision.HIGHEST

D = 1024
L = 2048
NC = 256
T = L + NC
GRID_W = 64
HEAD_DIM = 64
ROWS = L // GRID_W
EPS = 1e-6
N_MOD = 6

SSD_INNER = 2 * D
SSD_HEADS = 32
SSD_GROUPS = 8
SSD_HPG = 4
SSD_STATE = 128
SSD_XBC = SSD_INNER + 2 * SSD_GROUPS * SSD_STATE
Q = 128
NCH = T // Q

NA_HEADS = 16
NA_KH = 8
NA_KW = 16
NA_QROWS = 2
NA_KROWS = 10
NA_NQ = NA_QROWS * GRID_W
NA_NK = NA_KROWS * GRID_W
NA_CASES = 5
NA_NOFF = 2 * NA_KH - 1

SWA_HEADS = 16
SWA_KV = 4
SWA_W = 128
ROPE_BASE = 10000.0
D_FF = 4 * D

R_XBC, R_DT, R_NAK, R_NAV, R_SWAK, R_SWAV, R_Z, R_NAQ, R_SWAQ, R_GATE, R_END = (
    0, 4096, 4160, 5184, 6208, 6464, 6720, 8768, 9792, 10816, 13888)
C_XBC, C_NAK, C_NAV, C_SWAK, C_SWAV, C_Z, C_NAQ, C_SWAQ, C_GATE, NCAT = (
    0, 4096, 5120, 6144, 6656, 7168, 9216, 10240, 11264, 14336)

NEG = -1e30
LOG2E = float(np.log2(np.e))
QSCALE = 0.125 * LOG2E
VMEM_LIMIT = 48 * 1024 * 1024


def _cparams(n_axes):
    return pltpu.CompilerParams(dimension_semantics=("parallel",) * n_axes,
                                vmem_limit_bytes=VMEM_LIMIT)


def _sigmoid(x):
    return 0.5 * jnp.tanh(0.5 * x) + 0.5


def _silu(x):
    h = 0.5 * x
    return h * jnp.tanh(h) + h


def _softplus(x):
    return jnp.maximum(x, 0.0) + jnp.log1p(jnp.exp(-jnp.abs(x)))


def _nt(a, b):
    return lax.dot_general(a, b, (((1,), (1,)), ((), ())), preferred_element_type=F32)


def _tn(a, b):
    return lax.dot_general(a, b, (((0,), (0,)), ((), ())), preferred_element_type=F32)


def _dot(a, b):
    return jnp.dot(a, b, preferred_element_type=F32)


def _chunk_loop(n, unroll, body):
    def wrapped(c, carry):
        body(c, pl.multiple_of(c * Q, Q))
        return carry
    lax.fori_loop(0, n, wrapped, 0, unroll=unroll)


def _ada_kernel(c_ref, w_ref, b_ref, o_ref):
    c = c_ref[...]
    s = _silu(c).astype(BF16)
    o_ref[...] = _dot(s, w_ref[...].astype(BF16)) + b_ref[...]


def _ada_mod(cvec, ada_w, ada_b):
    depth = ada_w.shape[0]
    r = cvec.shape[0]
    return pl.pallas_call(
        _ada_kernel,
        out_shape=jax.ShapeDtypeStruct((depth, r, N_MOD * D), F32),
        grid=(depth, N_MOD),
        in_specs=[pl.BlockSpec((r, D), lambda l, n: (0, 0)),
                  pl.BlockSpec((None, D, D), lambda l, n: (l, 0, n)),
                  pl.BlockSpec((None, 1, D), lambda l, n: (l, 0, n))],
        out_specs=pl.BlockSpec((None, r, D), lambda l, n: (l, 0, n)),
        compiler_params=_cparams(2),
        name="ada_mod",
    )(cvec, ada_w, ada_b.reshape(depth, 1, N_MOD * D))


TR = 256
NRT = T // TR
MERGE_TILES = ((T, 384), (L, 512))
FFN_TILES = ((T, 256), (L, 512))


def _mod_spec(which):
    return pl.BlockSpec((None, None, None, 1, D), lambda b, j: (b, j // (L // TR), which, 0, 0))


def _mod_pair(which, tr):
    if L % tr == 0:
        kinds = [lambda j: j * tr // L] * 2
    else:
        kinds = [lambda j: 0, lambda j: 1]
    return [pl.BlockSpec((None, None, None, 1, D), lambda b, j, kind=kind: (b, kind(j), which, 0, 0))
            for kind in kinds]


def _pick_mod(lat_ref, ctx_ref, tr):
    if L % tr == 0:
        return lat_ref[...]
    row = lax.broadcasted_iota(jnp.int32, (tr, 1), 0) + pl.program_id(1) * tr
    return jnp.where(row >= L, ctx_ref[...], lat_ref[...])


def _rms(x, g):
    return x * lax.rsqrt(jnp.mean(x * x, axis=-1, keepdims=True) + EPS) * g


def _normmod_kernel(x_ref, g_ref, sh_ref, sc_ref, o_ref):
    y = _rms(x_ref[...], g_ref[...])
    o_ref[...] = (y * (1.0 + sc_ref[...]) + sh_ref[...]).astype(o_ref.dtype)


def _normmod(xs, g, ms):
    b = xs.shape[0]
    return pl.pallas_call(
        _normmod_kernel,
        out_shape=jax.ShapeDtypeStruct((b, T, D), BF16),
        grid=(b, NRT),
        in_specs=[pl.BlockSpec((None, TR, D), lambda b, j: (b, j, 0)),
                  pl.BlockSpec((1, D), lambda b, j: (0, 0)),
                  _mod_spec(0), _mod_spec(1)],
        out_specs=pl.BlockSpec((None, TR, D), lambda b, j: (b, j, 0)),
        compiler_params=_cparams(2),
        name="normmod",
    )(xs, g.reshape(1, D), ms, ms)


def _mm_kernel(x_ref, w_ref, o_ref):
    o_ref[...] = _dot(x_ref[...], w_ref[...]).astype(o_ref.dtype)


def _matmul(x, w, out_dtype, tn, name):
    b, t, k = x.shape
    n = w.shape[1]
    return pl.pallas_call(
        _mm_kernel,
        out_shape=jax.ShapeDtypeStruct((b, t, n), out_dtype),
        grid=(n // tn, b),
        in_specs=[pl.BlockSpec((None, t, k), lambda n_, b_: (b_, 0, 0)),
                  pl.BlockSpec((k, tn), lambda n_, b_: (0, n_))],
        out_specs=pl.BlockSpec((None, t, tn), lambda n_, b_: (b_, 0, n_)),
        compiler_params=_cparams(2),
        name=name,
    )(x, w)


CW = 512
CONV_RC = 256
PAD_LAT = 8
PAD_CTX = 16 + L


def _conv_kernel(x_ref, w_ref, b_ref, o_ref, pad_ref):
    z8 = jnp.zeros((8, CW), F32)
    pad_ref[0:8, :] = z8
    pad_ref[PAD_LAT + L:PAD_CTX, :] = z8
    pad_ref[PAD_CTX + NC:PAD_CTX + NC + 8, :] = z8
    pad_ref[PAD_LAT:PAD_LAT + L, :] = x_ref[0:L, :].astype(F32)
    pad_ref[PAD_CTX:PAD_CTX + NC, :] = x_ref[L:T, :].astype(F32)
    w = w_ref[...]
    bias = b_ref[...]
    for c in range(T // CONV_RC):
        r0 = c * CONV_RC
        base = (PAD_LAT if r0 < L else PAD_CTX - L) + r0
        win = pad_ref[base - 8:base + CONV_RC + 8, :]
        acc = bias + win[8:8 + CONV_RC] * w[2:3, :]
        for k in (0, 1, 3, 4):
            shifted = pltpu.roll(win, (2 - k) % (CONV_RC + 16), axis=0)
            acc = acc + shifted[8:8 + CONV_RC] * w[k:k + 1, :]
        o_ref[r0:r0 + CONV_RC, :] = (acc * jnp.tanh(acc) + acc).astype(o_ref.dtype)


def _conv_silu(p, conv_w, conv_b):
    b = p.shape[0]
    w8 = 0.5 * jnp.concatenate([conv_w, jnp.zeros((3, SSD_XBC), F32)], axis=0)
    return pl.pallas_call(
        _conv_kernel,
        out_shape=jax.ShapeDtypeStruct((b, T, SSD_XBC), BF16),
        grid=(b, SSD_XBC // CW),
        in_specs=[pl.BlockSpec((None, T, CW), lambda b_, n: (b_, 0, n)),
                  pl.BlockSpec((8, CW), lambda b_, n: (0, n)),
                  pl.BlockSpec((1, CW), lambda b_, n: (0, n))],
        out_specs=pl.BlockSpec((None, T, CW), lambda b_, n: (b_, 0, n)),
        scratch_shapes=[pltpu.VMEM((T + 24, CW), F32)],
        compiler_params=_cparams(2),
        name="conv_silu",
    )(p, w8, 0.5 * conv_b.reshape(1, SSD_XBC))


GW = SSD_HPG * HEAD_DIM


def _split3(x):
    hi = x.astype(BF16)
    r = x - hi.astype(F32)
    mid = r.astype(BF16)
    lo = (r - mid.astype(F32)).astype(BF16)
    return hi, mid, lo


def _cum_constants():
    ii = lax.broadcasted_iota(jnp.int32, (Q, Q), 0)
    jj = lax.broadcasted_iota(jnp.int32, (Q, Q), 1)
    tril = jnp.where(ii >= jj, 1.0, 0.0).astype(BF16)
    triu = jnp.where(ii <= jj, 1.0, 0.0).astype(BF16)
    return jnp.concatenate([jnp.concatenate([tril] * 3, axis=1),
                            jnp.concatenate([triu] * 3, axis=1)], axis=0)


def _ssd_prep_kernel(dt_ref, bias_ref, alog_ref, accg_o, wts_o, eacc_o, rowt_o, dtt_o):
    lcat = _cum_constants()
    isf = (lax.broadcasted_iota(jnp.int32, (1, Q), 1) % 8) < 4
    bias = bias_ref[...]
    a_r = -jnp.exp(alog_ref[...])

    def body(c, t0):
        rows = pl.ds(t0, Q)
        dt = _softplus(dt_ref[rows, :] + bias)
        both = _dot(lcat, jnp.concatenate(_split3(dt * a_r), axis=0))
        acc = jnp.where(isf, both[0:Q], both[Q:2 * Q])
        tot = jnp.where(isf, both[Q - 1:Q], both[Q:Q + 1])
        for g in range(SSD_GROUPS):
            accg_o[g, rows, :] = acc if g == 0 else pltpu.roll(acc, Q - 8 * g, axis=1)
        wts_o[rows, :] = dt * jnp.exp(tot - acc)
        eacc_o[rows, :] = jnp.exp(acc)
        dt_t = dt.T
        rowt_o[:, rows] = acc.T - jnp.log(dt_t)
        dtt_o[:, rows] = dt_t

    _chunk_loop(NCH, 3, body)


def _ssd_kernel(x_ref, b_ref, c_ref, z_ref, acc_ref, wts_ref, eacc_ref, rowt_ref, dtt_ref, dsk_ref, ng_ref,
                o_ref, xw_s, ec_s, big_s, sin_s, cb_s, y_s, *, n_out):
    g = pl.program_id(1)
    ii = lax.broadcasted_iota(jnp.int32, (Q, Q), 0)
    jj = lax.broadcasted_iota(jnp.int32, (Q, Q), 1)
    low = ii > jj
    up = ii < jj
    lo = jj < HEAD_DIM
    ek = lax.broadcasted_iota(jnp.int32, (Q, 2 * GW), 0)
    ech = lax.broadcasted_iota(jnp.int32, (Q, 2 * GW), 1)
    esel = jnp.where(ek == g * 8 + ech // HEAD_DIM, 1.0, 0.0).astype(BF16)
    esel2 = jnp.concatenate([esel] * 2, axis=0)
    zero_x = jnp.zeros((Q, Q), BF16)
    rows = lambda t0: pl.ds(t0, Q)

    def stage_expand(c, t0):
        w2 = _split3(wts_ref[rows(t0), :])[:2]
        e2 = _split3(eacc_ref[rows(t0), :])[:2]
        lhs = jnp.concatenate([jnp.concatenate(w2, axis=1), jnp.concatenate(e2, axis=1)], axis=0)
        both = _dot(lhs, esel2)
        xs = x_ref[rows(t0), :].astype(F32)
        xw_s[rows(t0), :] = (jnp.concatenate([xs, xs], axis=1) * both[0:Q]).astype(BF16)
        ec_s[rows(t0), :] = both[Q:2 * Q]

    _chunk_loop(NCH, 3, stage_expand)

    def stage_upd(c, t0):
        big_s[c] = _tn(b_ref[rows(t0), :], xw_s[rows(t0), :])

    _chunk_loop(NCH, 6, stage_upd)

    s = jnp.zeros((SSD_STATE, GW), F32)
    for c in list(range(L // Q, NCH)) + list(range(L // Q)):
        sin_s[c, :, 0:GW] = s.astype(BF16)
        s = s * ec_s[c * Q + Q - 1:c * Q + Q, 0:GW] + big_s[c, :, 0:GW]
    s = jnp.zeros((SSD_STATE, GW), F32)
    for c in reversed(range(NCH)):
        sin_s[c, :, GW:2 * GW] = s.astype(BF16)
        s = s * ec_s[c * Q:c * Q + 1, GW:2 * GW] + big_s[c, :, GW:2 * GW]

    def stage_cb(c, t0):
        cc = c_ref[rows(t0), :]
        cb_s[rows(t0), :] = _nt(cc, b_ref[rows(t0), :])
        big_s[c] = _dot(cc, sin_s[c])

    _chunk_loop(n_out, 6 if n_out % 6 == 0 else 4, stage_cb)

    dsk = dsk_ref[...]
    ng = ng_ref[...]

    def colb(v, k):
        return jnp.broadcast_to(v[:, k:k + 1], (Q, Q))

    def stage_y(c, t0):
        xb = x_ref[rows(t0), :]
        acc = acc_ref[rows(t0), :]
        rowt = rowt_ref[:, rows(t0)]
        dt_t = dtt_ref[:, rows(t0)]
        ec = ec_s[rows(t0), :]
        cb = cb_s[rows(t0), :]
        yoff = big_s[c]
        ys = []
        for pr in range(2):
            ms = []
            for hh in range(2):
                r = 2 * pr + hh
                diag = jnp.log(dt_t[r:r + 1, :] + dt_t[4 + r:5 + r, :])
                seg = jnp.where(low, colb(acc, r) - rowt[r:r + 1, :],
                                jnp.where(up, colb(acc, 4 + r) - rowt[4 + r:5 + r, :], diag))
                ms.append((cb * jnp.exp(seg)).astype(BF16))
            xp = xb[:, pr * Q:(pr + 1) * Q]
            xstack = jnp.concatenate([jnp.where(lo, xp, zero_x), jnp.where(lo, zero_x, xp)], axis=0)
            y = _dot(jnp.concatenate(ms, axis=1), xstack)
            y = (y + ec[:, pr * Q:(pr + 1) * Q] * yoff[:, pr * Q:(pr + 1) * Q]
                 + ec[:, GW + pr * Q:GW + (pr + 1) * Q] * yoff[:, GW + pr * Q:GW + (pr + 1) * Q])
            ys.append(y)
        y_s[rows(t0), :] = jnp.concatenate(ys, axis=1)

    _chunk_loop(n_out, 2, stage_y)

    def stage_out(c, t0):
        y = y_s[rows(t0), :] + dsk * x_ref[rows(t0), :].astype(F32)
        zf = z_ref[rows(t0), :].astype(F32)
        u = y * _silu(zf)
        o_ref[rows(t0), :] = _rms(u, ng).astype(o_ref.dtype)

    _chunk_loop(n_out, 3 if n_out % 3 == 0 else 4, stage_out)


def _slot_order(v):
    s = v.reshape(2, SSD_GROUPS, SSD_HPG).transpose(1, 0, 2).reshape(1, 2 * SSD_HEADS)
    return jnp.pad(s, ((0, 0), (0, Q - 2 * SSD_HEADS)))


def _ssd(xa, p, dtr, dt_bias, a_log, ssd_d, ssd_norm_g, emit_ctx):
    b = xa.shape[0]
    t_out = T if emit_ctx else L
    row = pl.BlockSpec((None, T, Q), lambda b_: (b_, 0, 0))
    col = pl.BlockSpec((None, Q, T), lambda b_: (b_, 0, 0))
    one = pl.BlockSpec((1, Q), lambda b_: (0, 0))
    accg, wts, eacc, rowt, dtt = pl.pallas_call(
        _ssd_prep_kernel,
        out_shape=([jax.ShapeDtypeStruct((b, SSD_GROUPS, T, Q), F32)] + [jax.ShapeDtypeStruct((b, T, Q), F32)] * 2
                   + [jax.ShapeDtypeStruct((b, Q, T), F32)] * 2),
        grid=(b,),
        in_specs=[row, one, one],
        out_specs=[pl.BlockSpec((None, SSD_GROUPS, T, Q), lambda b_: (b_, 0, 0, 0)), row, row, col, col],
        compiler_params=_cparams(1),
        name="ssd_prep",
    )(dtr, _slot_order(dt_bias), _slot_order(a_log))

    dsk = jnp.repeat(ssd_d, HEAD_DIM).reshape(1, SSD_INNER)
    ng = ssd_norm_g.reshape(1, SSD_INNER)
    xoff, boff, coff, zoff = 0, SSD_INNER // Q, (SSD_INNER + SSD_GROUPS * SSD_STATE) // Q, C_Z // GW
    full = pl.BlockSpec((None, T, Q), lambda b_, g: (b_, 0, 0))
    slot = pl.BlockSpec((None, 8, T), lambda b_, g: (b_, g, 0))
    return pl.pallas_call(
        functools.partial(_ssd_kernel, n_out=t_out // Q),
        out_shape=jax.ShapeDtypeStruct((b, t_out, SSD_INNER), BF16),
        grid=(b, SSD_GROUPS),
        in_specs=[pl.BlockSpec((None, T, GW), lambda b_, g: (b_, 0, xoff + g)),
                  pl.BlockSpec((None, T, Q), lambda b_, g: (b_, 0, boff + g)),
                  pl.BlockSpec((None, T, Q), lambda b_, g: (b_, 0, coff + g)),
                  pl.BlockSpec((None, T, GW), lambda b_, g: (b_, 0, zoff + g)),
                  pl.BlockSpec((None, None, T, Q), lambda b_, g: (b_, g, 0, 0)), full, full, slot, slot,
                  pl.BlockSpec((1, GW), lambda b_, g: (0, g)),
                  pl.BlockSpec((1, GW), lambda b_, g: (0, g))],
        out_specs=pl.BlockSpec((None, t_out, GW), lambda b_, g: (b_, 0, g)),
        scratch_shapes=[pltpu.VMEM((T, 2 * GW), BF16),
                        pltpu.VMEM((T, 2 * GW), F32),
                        pltpu.VMEM((NCH, SSD_STATE, 2 * GW), F32),
                        pltpu.VMEM((NCH, SSD_STATE, 2 * GW), BF16),
                        pltpu.VMEM((T, Q), F32),
                        pltpu.VMEM((T, GW), F32)],
        compiler_params=_cparams(2),
        name="ssd",
    )(xa, xa, xa, p, accg, wts, eacc, rowt, dtt, dsk, ng)


def _softmax_t(parts, extra=None):
    mx = None
    for s in parts:
        r = jnp.max(s, axis=0, keepdims=True)
        mx = r if mx is None else jnp.maximum(mx, r)
    if extra is not None:
        mx = jnp.maximum(mx, extra)
    den = None
    es = []
    for s in parts:
        e = jnp.exp2(s - mx)
        d = jnp.sum(e, axis=0, keepdims=True)
        den = d if den is None else den + d
        es.append(e.astype(BF16))
    if extra is not None:
        den = den + jnp.exp2(extra - mx)
    return es, 1.0 / den


def _pv_t(vts, es):
    out = None
    for vt, e in zip(vts, es):
        o = _dot(vt, e)
        out = o if out is None else out + o
    return out


def _na_row_offsets():
    table = []
    for j in (0, 1, 2, ROWS // NA_QROWS - 2, ROWS // NA_QROWS - 1):
        start = int(np.clip(NA_QROWS * j - NA_KH // 2, 0, ROWS - NA_KROWS))
        per_row = []
        for i in range(NA_KROWS):
            kr = start + i
            offs = []
            for qr in range(NA_QROWS):
                r = NA_QROWS * j + qr
                rs = int(np.clip(r - NA_KH // 2, 0, ROWS - NA_KH))
                offs.append(kr - r + NA_KH - 1 if rs <= kr < rs + NA_KH else NA_NOFF)
            per_row.append(tuple(offs))
        table.append(per_row)
    return table


def _na_kernel(q_ref, k_ref, v_ref, src_ref, o_ref, vt_ref, bias_ref, sa_ref, sb_ref, pa_ref, pb_ref, ra_ref,
               rb_ref, *, emit_ctx):
    @pl.when(pl.program_id(1) == 0)
    def _():
        lo_tile = lax.broadcasted_iota(jnp.int32, (GRID_W, Q), 1) < GRID_W
        for ci, per_row in enumerate(_na_row_offsets()):
            for i, (off0, off1) in enumerate(per_row):
                for hh in range(2):
                    bias_ref[ci, i * GRID_W:(i + 1) * GRID_W, hh * Q:(hh + 1) * Q] = jnp.where(
                        lo_tile, src_ref[hh, off0], src_ref[hh, off1])

    lane = lax.broadcasted_iota(jnp.int32, (NA_NQ, Q), 1)
    lo = lane < HEAD_DIM
    top = lax.broadcasted_iota(jnp.int32, (Q, NA_NQ), 0) < HEAD_DIM
    zero = jnp.zeros((NA_NQ, Q), BF16)

    def vt_body(c, t0):
        vt_ref[:, pl.ds(t0, Q)] = v_ref[pl.ds(t0, Q), :].astype(F32).T.astype(BF16)

    _chunk_loop(NCH, 6, vt_body)

    kc = k_ref[L:T, :]
    vtc = vt_ref[:, L:T]

    def qpair(q0):
        q = (q_ref[pl.ds(q0, NA_NQ), :].astype(F32) * QSCALE).astype(BF16)
        return jnp.concatenate([jnp.where(lo, q, zero), jnp.where(lo, zero, q)], axis=0)

    def finish(q0, ot, rden):
        ot = ot * rden
        w = jnp.where(top, ot[:, 0:NA_NQ], ot[:, NA_NQ:2 * NA_NQ])
        o_ref[pl.ds(q0, NA_NQ), :] = w.T.astype(o_ref.dtype)

    def block_params(j):
        q0 = pl.multiple_of(j * NA_NQ, NA_NQ)
        start = jnp.clip(NA_QROWS * j - NA_KH // 2, 0, ROWS - NA_KROWS)
        koff = pl.multiple_of(start * GRID_W, 2 * GRID_W)
        last = ROWS // NA_QROWS - 1
        case = (jnp.where(j >= 1, 1, 0) + jnp.where(j >= 2, 1, 0)
                + jnp.where(j >= last - 1, 1, 0) + jnp.where(j >= last, 1, 0))
        return q0, koff, case

    def qk(j, s_ref):
        q0, koff, case = block_params(j)
        qp = qpair(q0)
        s_ref[0:NA_NK, :] = _nt(k_ref[pl.ds(koff, NA_NK), :], qp) + bias_ref[case]
        s_ref[NA_NK:NA_NK + NC, :] = _nt(kc, qp)

    def soft(s_ref, p_ref, r_ref):
        s = s_ref[...]
        e = jnp.exp2(s - jnp.max(s, axis=0, keepdims=True))
        r_ref[...] = jnp.broadcast_to(1.0 / jnp.sum(e, axis=0, keepdims=True), r_ref.shape)
        p_ref[...] = e.astype(BF16)

    def pv(j, p_ref, r_ref):
        q0, koff, _ = block_params(j)
        ot = (_dot(vt_ref[:, pl.ds(koff, NA_NK)], p_ref[0:NA_NK, :])
              + _dot(vtc, p_ref[NA_NK:NA_NK + NC, :]))
        finish(q0, ot, r_ref[0:1, :])

    nblk = L // NA_NQ
    qk(0, sa_ref)
    qk(1, sb_ref)
    soft(sa_ref, pa_ref, ra_ref)

    def body(m, carry):
        pv(2 * m - 2, pa_ref, ra_ref)
        qk(2 * m, sa_ref)
        soft(sb_ref, pb_ref, rb_ref)
        pv(2 * m - 1, pb_ref, rb_ref)
        qk(2 * m + 1, sb_ref)
        soft(sa_ref, pa_ref, ra_ref)
        return carry

    lax.fori_loop(1, nblk // 2, body, 0)
    pv(nblk - 2, pa_ref, ra_ref)
    soft(sb_ref, pb_ref, rb_ref)
    pv(nblk - 1, pb_ref, rb_ref)

    if emit_ctx:
        ctx_blocks = []
        for cbk in range(NC // NA_NQ):
            q0 = L + cbk * NA_NQ
            ctx_blocks.append((q0, _nt(kc, qpair(q0))))
        ctx_soft = [(q0,) + _softmax_t([s]) for q0, s in ctx_blocks]
        for q0, es, rden in ctx_soft:
            finish(q0, _pv_t([vtc], es), rden)


def _na_bias_source(rpb_all):
    depth = rpb_all.shape[0]
    ck = np.arange(GRID_W)[:, None]
    cq = (np.arange(Q) % GRID_W)[None, :]
    col_oh = ((ck - cq + NA_KW - 1)[None] == np.arange(2 * NA_KW - 1)[:, None, None]).astype(np.float32)
    cs = np.clip(cq - NA_KW // 2, 0, GRID_W - NA_KW)
    col_ok = (ck >= cs) & (ck < cs + NA_KW)
    rp = rpb_all.reshape(depth, NA_HEADS // 2, 2, NA_NOFF, 2 * NA_KW - 1)
    src = jnp.einsum("lperd,dbz->lperbz", rp, col_oh, precision=HIGHEST)
    src = jnp.where(col_ok, src * LOG2E, NEG)
    return jnp.concatenate([src, jnp.full_like(src[:, :, :, :1], NEG)], axis=3)


def _na(p, bias_src, emit_ctx):
    b = p.shape[0]
    t_out = T if emit_ctx else L
    qo, ko, vo = C_NAQ // Q, C_NAK // Q, C_NAV // Q
    return pl.pallas_call(
        functools.partial(_na_kernel, emit_ctx=emit_ctx),
        out_shape=jax.ShapeDtypeStruct((b, t_out, D), BF16),
        grid=(NA_HEADS // 2, b),
        in_specs=[pl.BlockSpec((None, T, Q), lambda h, b_: (b_, 0, qo + h)),
                  pl.BlockSpec((None, T, Q), lambda h, b_: (b_, 0, ko + h)),
                  pl.BlockSpec((None, T, Q), lambda h, b_: (b_, 0, vo + h)),
                  pl.BlockSpec((None, 2, NA_NOFF + 1, GRID_W, Q), lambda h, b_: (h, 0, 0, 0, 0))],
        out_specs=pl.BlockSpec((None, t_out, Q), lambda h, b_: (b_, 0, h)),
        scratch_shapes=[pltpu.VMEM((Q, T), BF16),
                        pltpu.VMEM((NA_CASES, NA_NK, 2 * NA_NQ), F32),
                        pltpu.VMEM((NA_NK + NC, 2 * NA_NQ), F32),
                        pltpu.VMEM((NA_NK + NC, 2 * NA_NQ), F32),
                        pltpu.VMEM((NA_NK + NC, 2 * NA_NQ), BF16),
                        pltpu.VMEM((NA_NK + NC, 2 * NA_NQ), BF16),
                        pltpu.VMEM((8, 2 * NA_NQ), F32),
                        pltpu.VMEM((8, 2 * NA_NQ), F32)],
        compiler_params=pltpu.CompilerParams(dimension_semantics=("parallel", "arbitrary"),
                                             vmem_limit_bytes=VMEM_LIMIT),
        name="na_attn",
    )(p, p, p, bias_src)


SWA_KWIN = 3 * SWA_W
SWA_QB = 128
SWA_STACK = 4 * SWA_QB
ROPE_RC = 256


def _swap_rotary_halves(t, lane):
    quarter = HEAD_DIM // 4
    a = pltpu.roll(t, Q - quarter, axis=1)
    b = pltpu.roll(t, quarter, axis=1)
    return jnp.where((lane % (2 * quarter)) < quarter, a, b)


def _swa_kernel(sink_ref, q_ref, k_ref, v_ref, cos_ref, sin_ref, mask_ref, o_ref, krot_ref, vt_ref,
                sa_ref, sb_ref, pa_ref, pb_ref, ra_ref, rb_ref, *, emit_ctx):
    kp = pl.program_id(1)
    lane = lax.broadcasted_iota(jnp.int32, (SWA_QB, Q), 1)
    lo = lane < HEAD_DIM
    zero = jnp.zeros((SWA_QB, Q), BF16)
    col = lax.broadcasted_iota(jnp.int32, (1, SWA_STACK), 1)
    eye4 = jnp.where(lax.broadcasted_iota(jnp.int32, (SWA_STACK, Q), 0) % SWA_QB
                     == lax.broadcasted_iota(jnp.int32, (SWA_STACK, Q), 1), 1.0, 0.0).astype(BF16)

    def rope(t, r0, n):
        lane_n = lax.broadcasted_iota(jnp.int32, (n, Q), 1)
        return t * cos_ref[pl.ds(r0, n), :] + _swap_rotary_halves(t, lane_n) * sin_ref[pl.ds(r0, n), :]

    def krot_body(i, carry):
        r0 = pl.multiple_of(i * ROPE_RC, ROPE_RC)
        for gl in range(2):
            t = k_ref[pl.ds(r0, ROPE_RC), gl * Q:(gl + 1) * Q].astype(F32)
            krot_ref[pl.ds(r0, ROPE_RC), gl * Q:(gl + 1) * Q] = rope(t, r0, ROPE_RC).astype(BF16)
        return carry

    lax.fori_loop(0, L // ROPE_RC, krot_body, 0, unroll=2)

    def vt_body(c, t0):
        for gl in range(2):
            vt = v_ref[pl.ds(t0, Q), gl * Q:(gl + 1) * Q].astype(F32).T
            vt_ref[gl, :, pl.ds(t0, Q)] = vt[0:HEAD_DIM].astype(BF16)

    _chunk_loop(NCH, 6, vt_body)

    def stack_q(qa, qb):
        return jnp.concatenate([jnp.where(lo, qa, zero), jnp.where(lo, zero, qa),
                                jnp.where(lo, qb, zero), jnp.where(lo, zero, qb)], axis=0)

    def sink_row(gl):
        base = kp * 8 + gl * 4
        row = jnp.where(col < SWA_QB, sink_ref[base],
                        jnp.where(col < 2 * SWA_QB, sink_ref[base + 1],
                                  jnp.where(col < 3 * SWA_QB, sink_ref[base + 2], sink_ref[base + 3])))
        return row * LOG2E

    def finish(q0, gl, ot, rden):
        ot = ot * rden
        pa = jnp.concatenate([ot[:, 0:SWA_QB], ot[:, SWA_QB:2 * SWA_QB]], axis=0).T
        pb = jnp.concatenate([ot[:, 2 * SWA_QB:3 * SWA_QB], ot[:, 3 * SWA_QB:4 * SWA_QB]], axis=0).T
        o_ref[pl.ds(q0, SWA_QB), gl * 2 * Q:(gl + 1) * 2 * Q] = jnp.concatenate([pa, pb], axis=1).astype(o_ref.dtype)

    def block_params(i):
        q0 = pl.multiple_of(i * SWA_QB, SWA_QB)
        ws = pl.multiple_of(jnp.clip((i - 1) * SWA_W, 0, L - SWA_KWIN), SWA_W)
        case = jnp.where(i >= 1, 1, 0) + jnp.where(i >= L // SWA_QB - 1, 1, 0)
        return q0, ws, case

    def qk(i, gl, s_ref):
        q0, ws, case = block_params(i)
        qa = rope(q_ref[pl.ds(q0, SWA_QB), gl * 2 * Q:gl * 2 * Q + Q].astype(F32), q0, SWA_QB) * QSCALE
        qb = rope(q_ref[pl.ds(q0, SWA_QB), gl * 2 * Q + Q:(gl + 1) * 2 * Q].astype(F32), q0, SWA_QB) * QSCALE
        qs = stack_q(qa.astype(BF16), qb.astype(BF16))
        a_loc = jnp.concatenate([krot_ref[pl.ds(ws, SWA_KWIN), gl * Q:(gl + 1) * Q], mask_ref[case]], axis=1)
        s_ref[0:SWA_KWIN, :] = _nt(a_loc, jnp.concatenate([qs, eye4], axis=1))
        s_ref[SWA_KWIN:SWA_KWIN + NC, :] = _nt(k_ref[L:T, gl * Q:(gl + 1) * Q], qs)

    def soft(gl, s_ref, p_ref, r_ref):
        s = s_ref[...]
        sink = sink_row(gl)
        mx = jnp.maximum(jnp.max(s, axis=0, keepdims=True), sink)
        e = jnp.exp2(s - mx)
        den = jnp.sum(e, axis=0, keepdims=True) + jnp.exp2(sink - mx)
        r_ref[...] = jnp.broadcast_to(1.0 / den, r_ref.shape)
        p_ref[...] = e.astype(BF16)

    def pv(i, gl, p_ref, r_ref):
        q0, ws, _ = block_params(i)
        ot = (_dot(vt_ref[gl, :, pl.ds(ws, SWA_KWIN)], p_ref[0:SWA_KWIN, :])
              + _dot(vt_ref[gl, :, L:T], p_ref[SWA_KWIN:SWA_KWIN + NC, :]))
        finish(q0, gl, ot, r_ref[0:1, :])

    nblk = L // SWA_QB
    qk(0, 0, sa_ref)
    qk(0, 1, sb_ref)
    soft(0, sa_ref, pa_ref, ra_ref)

    def body(i, carry):
        pv(i - 1, 0, pa_ref, ra_ref)
        qk(i, 0, sa_ref)
        soft(1, sb_ref, pb_ref, rb_ref)
        pv(i - 1, 1, pb_ref, rb_ref)
        qk(i, 1, sb_ref)
        soft(0, sa_ref, pa_ref, ra_ref)
        return carry

    lax.fori_loop(1, nblk, body, 0)
    pv(nblk - 1, 0, pa_ref, ra_ref)
    soft(1, sb_ref, pb_ref, rb_ref)
    pv(nblk - 1, 1, pb_ref, rb_ref)

    for cbk in range(NC // SWA_QB if emit_ctx else 0):
        q0 = L + cbk * SWA_QB
        scores = []
        for gl in range(2):
            qa = q_ref[q0:q0 + SWA_QB, gl * 2 * Q:gl * 2 * Q + Q].astype(F32) * QSCALE
            qb = q_ref[q0:q0 + SWA_QB, gl * 2 * Q + Q:(gl + 1) * 2 * Q].astype(F32) * QSCALE
            scores.append(_nt(k_ref[L:T, gl * Q:(gl + 1) * Q], stack_q(qa.astype(BF16), qb.astype(BF16))))
        soft = [_softmax_t([s], sink_row(gl)) for gl, s in enumerate(scores)]
        for gl, (es, rden) in enumerate(soft):
            finish(q0, gl, _pv_t([vt_ref[gl, :, L:T]], es), rden)


def _rope_tables():
    pos = np.arange(L)
    quarter = HEAD_DIM // 4
    lane = np.arange(Q) % HEAD_DIM
    inv = ROPE_BASE ** (-(lane % quarter).astype(np.float64) / quarter)
    p = np.where(lane[None, :] < HEAD_DIM // 2, (pos // GRID_W)[:, None], (pos % GRID_W)[:, None]).astype(np.float64)
    ang = p * inv[None, :]
    sign = np.where((lane % (2 * quarter)) < quarter, -1.0, 1.0)[None, :]
    return jnp.asarray(np.cos(ang), F32), jnp.asarray(np.sin(ang) * sign, F32)


def _swa_mask_table():
    key = np.arange(SWA_KWIN)[:, None]
    qry = np.arange(SWA_QB)[None, :]
    tabs = [np.where(np.abs(key - qry + delta) <= SWA_W, 0.0, NEG) for delta in (0, -SWA_W, -2 * SWA_W)]
    return jnp.asarray(np.stack(tabs), BF16)


def _swa(p, sink, cos_t, sin_t, mask_t, emit_ctx):
    b = p.shape[0]
    t_out = T if emit_ctx else L
    qo, ko, vo = C_SWAQ // 512, C_SWAK // 256, C_SWAV // 256
    return pl.pallas_call(
        functools.partial(_swa_kernel, emit_ctx=emit_ctx),
        out_shape=jax.ShapeDtypeStruct((b, t_out, D), BF16),
        grid=(b, SWA_KV // 2),
        in_specs=[pl.BlockSpec(memory_space=pltpu.SMEM),
                  pl.BlockSpec((None, T, 512), lambda b_, h: (b_, 0, qo + h)),
                  pl.BlockSpec((None, T, 256), lambda b_, h: (b_, 0, ko + h)),
                  pl.BlockSpec((None, T, 256), lambda b_, h: (b_, 0, vo + h)),
                  pl.BlockSpec((L, Q), lambda b_, h: (0, 0)),
                  pl.BlockSpec((L, Q), lambda b_, h: (0, 0)),
                  pl.BlockSpec((3, SWA_KWIN, SWA_QB), lambda b_, h: (0, 0, 0))],
        out_specs=pl.BlockSpec((None, t_out, 512), lambda b_, h: (b_, 0, h)),
        scratch_shapes=[pltpu.VMEM((L, 2 * Q), BF16),
                        pltpu.VMEM((2, HEAD_DIM, T), BF16),
                        pltpu.VMEM((SWA_KWIN + NC, SWA_STACK), F32),
                        pltpu.VMEM((SWA_KWIN + NC, SWA_STACK), F32),
                        pltpu.VMEM((SWA_KWIN + NC, SWA_STACK), BF16),
                        pltpu.VMEM((SWA_KWIN + NC, SWA_STACK), BF16),
                        pltpu.VMEM((8, SWA_STACK), F32),
                        pltpu.VMEM((8, SWA_STACK), F32)],
        compiler_params=_cparams(2),
        name="swa_attn",
    )(sink, p, p, p, cos_t, sin_t, mask_t)


def _merge_kernel(x_ref, ya_ref, yb_ref, yc_ref, g0_ref, g1_ref, g2_ref, mgl_ref, mgc_ref,
                  wa_ref, wb_ref, wc_ref, wo_ref, o_ref):
    m = (_sigmoid(g0_ref[...].astype(F32)) * _dot(ya_ref[...], wa_ref[...])
         + _sigmoid(g1_ref[...].astype(F32)) * _dot(yb_ref[...], wb_ref[...])
         + _sigmoid(g2_ref[...].astype(F32)) * _dot(yc_ref[...], wc_ref[...]))
    mg = _pick_mod(mgl_ref, mgc_ref, x_ref.shape[0])
    o_ref[...] = x_ref[...] + mg * _dot(m.astype(BF16), wo_ref[...])


def _const_spec(shape, single=True):
    mode = {"pipeline_mode": pl.Buffered(1)} if single else {}
    return pl.BlockSpec(shape, lambda b, j: (0,) * len(shape), **mode)


def _merge(xs, ya, yb, yc, p, ms, wa, wb, wc, wo, tr, n_tiles):
    b = xs.shape[0]
    go = C_GATE // D
    row = lambda w: pl.BlockSpec((None, tr, w), lambda b_, j: (b_, j, 0))
    gate = lambda k: pl.BlockSpec((None, tr, D), lambda b_, j: (b_, j, go + k))
    return pl.pallas_call(
        _merge_kernel,
        out_shape=jax.ShapeDtypeStruct((b, n_tiles * tr, D), F32),
        grid=(b, n_tiles),
        in_specs=[row(D), row(SSD_INNER), row(D), row(D), gate(0), gate(1), gate(2), *_mod_pair(2, tr),
                  _const_spec((SSD_INNER, D)), _const_spec((D, D)), _const_spec((D, D)), _const_spec((D, D))],
        out_specs=row(D),
        compiler_params=_cparams(2),
        name="merge",
    )(xs, ya, yb, yc, p, p, p, ms, ms, wa, wb, wc, wo)


FF_CH = 1024


def _ffn_kernel(*refs, final):
    if final:
        x_ref, g_ref, shl, shc, scl, scc, mgl, mgc, w1_ref, w2_ref, gn_ref, o_ref = refs
    else:
        (x_ref, g_ref, shl, shc, scl, scc, mgl, mgc, w1_ref, w2_ref, gn_ref, shnl, shnc, scnl, scnc,
         o_ref, h_ref) = refs
    tr = x_ref.shape[0]
    x = x_ref[...]
    h = (_rms(x, g_ref[...]) * (1.0 + _pick_mod(scl, scc, tr)) + _pick_mod(shl, shc, tr)).astype(BF16)
    acc = None
    for kf in range(D_FF // FF_CH):
        a = jnp.maximum(_dot(h, w1_ref[:, kf * FF_CH:(kf + 1) * FF_CH]), 0.0)
        o = _dot((a * a).astype(BF16), w2_ref[kf * FF_CH:(kf + 1) * FF_CH, :])
        acc = o if acc is None else acc + o
    y = x + _pick_mod(mgl, mgc, tr) * acc
    if final:
        o_ref[...] = _rms(y, gn_ref[...])
    else:
        o_ref[...] = y
        h_ref[...] = (_rms(y, gn_ref[...]) * (1.0 + _pick_mod(scnl, scnc, tr))
                      + _pick_mod(shnl, shnc, tr)).astype(h_ref.dtype)


def _ffn(xs, g, ms, w1, w2, tr, n_tiles, g_next, ms_next=None):
    b = xs.shape[0]
    final = ms_next is None
    row = pl.BlockSpec((None, tr, D), lambda b_, j: (b_, j, 0))
    in_specs = [row, _const_spec((1, D)), *_mod_pair(3, tr), *_mod_pair(4, tr), *_mod_pair(5, tr),
                _const_spec((D, D_FF), single=False), _const_spec((D_FF, D), single=False)]
    in_specs.append(_const_spec((1, D)))
    args = [xs, g.reshape(1, D), ms, ms, ms, ms, ms, ms, w1, w2, g_next.reshape(1, D)]
    out_shape = jax.ShapeDtypeStruct((b, n_tiles * tr, D), F32)
    out_specs = row
    if not final:
        in_specs += [*_mod_pair(0, tr), *_mod_pair(1, tr)]
        args += [ms_next, ms_next, ms_next, ms_next]
        out_shape = [out_shape, jax.ShapeDtypeStruct((b, n_tiles * tr, D), BF16)]
        out_specs = [row, row]
    return pl.pallas_call(
        functools.partial(_ffn_kernel, final=final),
        out_shape=out_shape,
        grid=(b, n_tiles),
        in_specs=in_specs,
        out_specs=out_specs,
        compiler_params=_cparams(2),
        name="ffn",
    )(*args)


def _prep_w_in(w):
    def dup(a):
        a = a.reshape(D, SWA_KV, 1, HEAD_DIM)
        return jnp.concatenate([a, a], axis=2).reshape(D, SWA_KV * 128)
    cat = jnp.concatenate([
        w[:, R_XBC:R_DT], w[:, R_NAK:R_NAV], w[:, R_NAV:R_SWAK], dup(w[:, R_SWAK:R_SWAV]),
        dup(w[:, R_SWAV:R_Z]), w[:, R_Z:R_NAQ], w[:, R_NAQ:R_SWAQ], w[:, R_SWAQ:R_GATE],
        w[:, R_GATE:R_END]], axis=1).astype(BF16)
    wd = w[:, R_DT:R_NAK].reshape(D, 2, SSD_GROUPS, SSD_HPG).transpose(0, 2, 1, 3).reshape(D, 2 * SSD_HEADS)
    wdt = jnp.concatenate([wd, jnp.zeros((D, Q - 2 * SSD_HEADS), F32)], axis=1).astype(BF16)
    return cat, wdt


def kernel(x, c, ctx, c_ctx, ada_w, ada_b, norm1_g, norm2_g, w_in, conv_w, conv_b, dt_bias, a_log, ssd_d,
           ssd_norm_g, na_rpb, swa_sink, w_o_ssd, w_o_na, w_o_swa, w_out, w_ff1, w_ff2, final_g):
    b = x.shape[0]
    depth = ada_w.shape[0]
    xs = jnp.concatenate([x, ctx], axis=1)
    nrow = -(-(b + 1) // 8) * 8
    cvec = jnp.concatenate([c, c_ctx[None, :], jnp.zeros((nrow - b - 1, D), F32)], axis=0)
    mod = _ada_mod(cvec, ada_w, ada_b)
    lat = mod[:, :b].reshape(depth, b, 1, N_MOD, 1, D)
    cx = jnp.broadcast_to(mod[:, b].reshape(depth, 1, 1, N_MOD, 1, D), lat.shape)
    ms_all = jnp.concatenate([lat, cx], axis=2)
    cos_t, sin_t = _rope_tables()
    mask_t = _swa_mask_table()
    na_bias = _na_bias_source(na_rpb)

    h = _normmod(xs, norm1_g[0], ms_all[0])
    for l in range(depth):
        last = l + 1 == depth
        ms = ms_all[l]
        wcat, wdt = _prep_w_in(w_in[l])
        p = _matmul(h, wcat, BF16, 1024, "in_proj")
        dtr = _matmul(h, wdt, F32, Q, "dt_proj")
        xa = _conv_silu(p, conv_w[l], conv_b[l])
        ya = _ssd(xa, p, dtr, dt_bias[l], a_log[l], ssd_d[l], ssd_norm_g[l], not last)
        yb = _na(p, na_bias[l], not last)
        yc = _swa(p, swa_sink[l], cos_t, sin_t, mask_t, not last)
        rows, tr = MERGE_TILES[last]
        xs = _merge(xs, ya, yb, yc, p, ms, w_o_ssd[l].astype(BF16), w_o_na[l].astype(BF16),
                    w_o_swa[l].astype(BF16), w_out[l].astype(BF16), tr, rows // tr)
        w1, w2 = w_ff1[l].astype(BF16), w_ff2[l].astype(BF16)
        rows, tr = FFN_TILES[last]
        if last:
            return _ffn(xs, norm2_g[l], ms, w1, w2, tr, rows // tr, final_g)
        xs, h = _ffn(xs, norm2_g[l], ms, w1, w2, tr, rows // tr, norm1_g[l + 1], ms_all[l + 1])
```

```python
import functools

import numpy as np
import jax
import jax.numpy as jnp
from jax import lax
from jax.experimental import pallas as pl
from jax.experimental.pallas import tpu as pltpu

F32 = jnp.float32
BF16 = jnp.bfloat16
HIGHEST = lax.Precision.HIGHEST

D = 1024
L = 2048
NC = 256
T = L + NC
GRID_W = 64
HEAD_DIM = 64
ROWS = L // GRID_W
EPS = 1e-6
N_MOD = 6

SSD_INNER = 2 * D
SSD_HEADS = 32
SSD_GROUPS = 8
SSD_HPG = 4
SSD_STATE = 128
SSD_XBC = SSD_INNER + 2 * SSD_GROUPS * SSD_STATE
Q = 128
NCH = T // Q

NA_HEADS = 16
NA_KH = 8
NA_KW = 16
NA_QROWS = 2
NA_KROWS = 10
NA_NQ = NA_QROWS * GRID_W
NA_NK = NA_KROWS * GRID_W
NA_CASES = 5
NA_NOFF = 2 * NA_KH - 1

SWA_HEADS = 16
SWA_KV = 4
SWA_W = 128
ROPE_BASE = 10000.0
D_FF = 4 * D

R_XBC, R_DT, R_NAK, R_NAV, R_SWAK, R_SWAV, R_Z, R_NAQ, R_SWAQ, R_GATE, R_END = (
    0, 4096, 4160, 5184, 6208, 6464, 6720, 8768, 9792, 10816, 13888)
C_XBC, C_NAK, C_NAV, C_SWAK, C_SWAV, C_Z, C_NAQ, C_SWAQ, C_GATE, NCAT = (
    0, 4096, 5120, 6144, 6656, 7168, 9216, 10240, 11264, 14336)

NEG = -1e30
LOG2E = float(np.log2(np.e))
QSCALE = 0.125 * LOG2E
VMEM_LIMIT = 48 * 1024 * 1024


def _cparams(n_axes):
    return pltpu.CompilerParams(dimension_semantics=("parallel",) * n_axes,
                                vmem_limit_bytes=VMEM_LIMIT)


def _sigmoid(x):
    return 0.5 * jnp.tanh(0.5 * x) + 0.5


def _silu(x):
    h = 0.5 * x
    return h * jnp.tanh(h) + h


def _softplus(x):
    return jnp.maximum(x, 0.0) + jnp.log1p(jnp.exp(-jnp.abs(x)))


def _nt(a, b):
    return lax.dot_general(a, b, (((1,), (1,)), ((), ())), preferred_element_type=F32)


def _tn(a, b):
    return lax.dot_general(a, b, (((0,), (0,)), ((), ())), preferred_element_type=F32)


def _dot(a, b):
    return jnp.dot(a, b, preferred_element_type=F32)


def _chunk_loop(n, unroll, body):
    def wrapped(c, carry):
        body(c, pl.multiple_of(c * Q, Q))
        return carry
    lax.fori_loop(0, n, wrapped, 0, unroll=unroll)


def _ada_kernel(c_ref, w_ref, b_ref, o_ref):
    c = c_ref[...]
    s = _silu(c).astype(BF16)
    o_ref[...] = _dot(s, w_ref[...].astype(BF16)) + b_ref[...]


def _ada_mod(cvec, ada_w, ada_b):
    depth = ada_w.shape[0]
    r = cvec.shape[0]
    return pl.pallas_call(
        _ada_kernel,
        out_shape=jax.ShapeDtypeStruct((depth, r, N_MOD * D), F32),
        grid=(depth, N_MOD),
        in_specs=[pl.BlockSpec((r, D), lambda l, n: (0, 0)),
                  pl.BlockSpec((None, D, D), lambda l, n: (l, 0, n)),
                  pl.BlockSpec((None, 1, D), lambda l, n: (l, 0, n))],
        out_specs=pl.BlockSpec((None, r, D), lambda l, n: (l, 0, n)),
        compiler_params=_cparams(2),
        name="ada_mod",
    )(cvec, ada_w, ada_b.reshape(depth, 1, N_MOD * D))


TR = 256
NRT = T // TR
MERGE_TILES = ((T, 384), (L, 512))
FFN_TILES = ((T, 256), (L, 512))


def _mod_spec(which):
    return pl.BlockSpec((None, None, None, 1, D), lambda b, j: (b, j // (L // TR), which, 0, 0))


def _mod_pair(which, tr):
    if L % tr == 0:
        kinds = [lambda j: j * tr // L] * 2
    else:
        kinds = [lambda j: 0, lambda j: 1]
    return [pl.BlockSpec((None, None, None, 1, D), lambda b, j, kind=kind: (b, kind(j), which, 0, 0))
            for kind in kinds]


def _pick_mod(lat_ref, ctx_ref, tr):
    if L % tr == 0:
        return lat_ref[...]
    row = lax.broadcasted_iota(jnp.int32, (tr, 1), 0) + pl.program_id(1) * tr
    return jnp.where(row >= L, ctx_ref[...], lat_ref[...])


def _rms(x, g):
    return x * lax.rsqrt(jnp.mean(x * x, axis=-1, keepdims=True) + EPS) * g


def _normmod_kernel(x_ref, g_ref, sh_ref, sc_ref, o_ref):
    y = _rms(x_ref[...], g_ref[...])
    o_ref[...] = (y * (1.0 + sc_ref[...]) + sh_ref[...]).astype(o_ref.dtype)


def _normmod(xs, g, ms):
    b = xs.shape[0]
    return pl.pallas_call(
        _normmod_kernel,
        out_shape=jax.ShapeDtypeStruct((b, T, D), BF16),
        grid=(b, NRT),
        in_specs=[pl.BlockSpec((None, TR, D), lambda b, j: (b, j, 0)),
                  pl.BlockSpec((1, D), lambda b, j: (0, 0)),
                  _mod_spec(0), _mod_spec(1)],
        out_specs=pl.BlockSpec((None, TR, D), lambda b, j: (b, j, 0)),
        compiler_params=_cparams(2),
        name="normmod",
    )(xs, g.reshape(1, D), ms, ms)


def _mm_kernel(x_ref, w_ref, o_ref):
    o_ref[...] = _dot(x_ref[...], w_ref[...]).astype(o_ref.dtype)


def _matmul(x, w, out_dtype, tn, name):
    b, t, k = x.shape
    n = w.shape[1]
    return pl.pallas_call(
        _mm_kernel,
        out_shape=jax.ShapeDtypeStruct((b, t, n), out_dtype),
        grid=(n // tn, b),
        in_specs=[pl.BlockSpec((None, t, k), lambda n_, b_: (b_, 0, 0)),
                  pl.BlockSpec((k, tn), lambda n_, b_: (0, n_))],
        out_specs=pl.BlockSpec((None, t, tn), lambda n_, b_: (b_, 0, n_)),
        compiler_params=_cparams(2),
        name=name,
    )(x, w)


CW = 512
CONV_RC = 256
PAD_LAT = 8
PAD_CTX = 16 + L


def _conv_kernel(x_ref, w_ref, b_ref, o_ref, pad_ref):
    z8 = jnp.zeros((8, CW), F32)
    pad_ref[0:8, :] = z8
    pad_ref[PAD_LAT + L:PAD_CTX, :] = z8
    pad_ref[PAD_CTX + NC:PAD_CTX + NC + 8, :] = z8
    pad_ref[PAD_LAT:PAD_LAT + L, :] = x_ref[0:L, :].astype(F32)
    pad_ref[PAD_CTX:PAD_CTX + NC, :] = x_ref[L:T, :].astype(F32)
    w = w_ref[...]
    bias = b_ref[...]
    for c in range(T // CONV_RC):
        r0 = c * CONV_RC
        base = (PAD_LAT if r0 < L else PAD_CTX - L) + r0
        win = pad_ref[base - 8:base + CONV_RC + 8, :]
        acc = bias + win[8:8 + CONV_RC] * w[2:3, :]
        for k in (0, 1, 3, 4):
            shifted = pltpu.roll(win, (2 - k) % (CONV_RC + 16), axis=0)
            acc = acc + shifted[8:8 + CONV_RC] * w[k:k + 1, :]
        o_ref[r0:r0 + CONV_RC, :] = _silu(acc).astype(o_ref.dtype)


def _conv_silu(p, conv_w, conv_b):
    b = p.shape[0]
    w8 = jnp.concatenate([conv_w, jnp.zeros((3, SSD_XBC), F32)], axis=0)
    return pl.pallas_call(
        _conv_kernel,
        out_shape=jax.ShapeDtypeStruct((b, T, SSD_XBC), BF16),
        grid=(b, SSD_XBC // CW),
        in_specs=[pl.BlockSpec((None, T, CW), lambda b_, n: (b_, 0, n)),
                  pl.BlockSpec((8, CW), lambda b_, n: (0, n)),
                  pl.BlockSpec((1, CW), lambda b_, n: (0, n))],
        out_specs=pl.BlockSpec((None, T, CW), lambda b_, n: (b_, 0, n)),
        scratch_shapes=[pltpu.VMEM((T + 24, CW), F32)],
        compiler_params=_cparams(2),
        name="conv_silu",
    )(p, w8, conv_b.reshape(1, SSD_XBC))


GW = SSD_HPG * HEAD_DIM


def _split3(x):
    hi = x.astype(BF16)
    r = x - hi.astype(F32)
    mid = r.astype(BF16)
    lo = (r - mid.astype(F32)).astype(BF16)
    return hi, mid, lo


def _cum_constants():
    ii = lax.broadcasted_iota(jnp.int32, (Q, Q), 0)
    jj = lax.broadcasted_iota(jnp.int32, (Q, Q), 1)
    tril = jnp.where(ii >= jj, 1.0, 0.0).astype(BF16)
    triu = jnp.where(ii <= jj, 1.0, 0.0).astype(BF16)
    return jnp.concatenate([jnp.concatenate([tril] * 3, axis=1),
                            jnp.concatenate([triu] * 3, axis=1)], axis=0)


def _ssd_prep_kernel(h_ref, wdt_ref, bias_ref, alog_ref, accg_o, wts_o, eacc_o, rowt_o, dtt_o):
    lcat = _cum_constants()
    isf = (lax.broadcasted_iota(jnp.int32, (1, Q), 1) % 8) < 4
    bias = bias_ref[...]
    a_r = -jnp.exp(alog_ref[...])

    def body(c, t0):
        rows = pl.ds(t0, Q)
        dt = _softplus(_dot(h_ref[rows, :], wdt_ref[...]) + bias)
        both = _dot(lcat, jnp.concatenate(_split3(dt * a_r), axis=0))
        acc = jnp.where(isf, both[0:Q], both[Q:2 * Q])
        tot = jnp.where(isf, both[Q - 1:Q], both[Q:Q + 1])
        for g in range(SSD_GROUPS):
            accg_o[g, rows, :] = acc if g == 0 else pltpu.roll(acc, Q - 8 * g, axis=1)
        wts_o[rows, :] = dt * jnp.exp(tot - acc)
        eacc_o[rows, :] = jnp.exp(acc)
        dt_t = dt.T
        rowt_o[:, rows] = acc.T - jnp.log(dt_t)
        dtt_o[:, rows] = dt_t

    _chunk_loop(NCH, 3, body)


def _ssd_kernel(x_ref, b_ref, c_ref, z_ref, acc_ref, wts_ref, eacc_ref, rowt_ref, dtt_ref, dsk_ref, ng_ref,
                o_ref, xw_s, ec_s, big_s, sin_s, cb_s, y_s, *, n_out):
    g = pl.program_id(1)
    ii = lax.broadcasted_iota(jnp.int32, (Q, Q), 0)
    jj = lax.broadcasted_iota(jnp.int32, (Q, Q), 1)
    low = ii > jj
    up = ii < jj
    lo = jj < HEAD_DIM
    ek = lax.broadcasted_iota(jnp.int32, (Q, 2 * GW), 0)
    ech = lax.broadcasted_iota(jnp.int32, (Q, 2 * GW), 1)
    esel = jnp.where(ek == g * 8 + ech // HEAD_DIM, 1.0, 0.0).astype(BF16)
    esel2 = jnp.concatenate([esel] * 2, axis=0)
    zero_x = jnp.zeros((Q, Q), BF16)
    rows = lambda t0: pl.ds(t0, Q)

    def stage_expand(c, t0):
        w2 = _split3(wts_ref[rows(t0), :])[:2]
        e2 = _split3(eacc_ref[rows(t0), :])[:2]
        lhs = jnp.concatenate([jnp.concatenate(w2, axis=1), jnp.concatenate(e2, axis=1)], axis=0)
        both = _dot(lhs, esel2)
        xs = x_ref[rows(t0), :].astype(F32)
        xw_s[rows(t0), :] = (jnp.concatenate([xs, xs], axis=1) * both[0:Q]).astype(BF16)
        ec_s[rows(t0), :] = both[Q:2 * Q]

    _chunk_loop(NCH, 3, stage_expand)

    def stage_upd(c, t0):
        big_s[c] = _tn(b_ref[rows(t0), :], xw_s[rows(t0), :])

    _chunk_loop(NCH, 6, stage_upd)

    s = jnp.zeros((SSD_STATE, GW), F32)
    for c in list(range(L // Q, NCH)) + list(range(L // Q)):
        sin_s[c, :, 0:GW] = s.astype(BF16)
        s = s * ec_s[c * Q + Q - 1:c * Q + Q, 0:GW] + big_s[c, :, 0:GW]
    s = jnp.zeros((SSD_STATE, GW), F32)
    for c in reversed(range(NCH)):
        sin_s[c, :, GW:2 * GW] = s.astype(BF16)
        s = s * ec_s[c * Q:c * Q + 1, GW:2 * GW] + big_s[c, :, GW:2 * GW]

    def stage_cb(c, t0):
        cc = c_ref[rows(t0), :]
        cb_s[rows(t0), :] = _nt(cc, b_ref[rows(t0), :])
        big_s[c] = _dot(cc, sin_s[c])

    _chunk_loop(n_out, 6 if n_out % 6 == 0 else 4, stage_cb)

    dsk = dsk_ref[...]
    ng = ng_ref[...]

    def colb(v, k):
        return jnp.broadcast_to(v[:, k:k + 1], (Q, Q))

    def stage_y(c, t0):
        xb = x_ref[rows(t0), :]
        acc = acc_ref[rows(t0), :]
        rowt = rowt_ref[:, rows(t0)]
        dt_t = dtt_ref[:, rows(t0)]
        ec = ec_s[rows(t0), :]
        cb = cb_s[rows(t0), :]
        yoff = big_s[c]
        ys = []
        for pr in range(2):
            ms = []
            for hh in range(2):
                r = 2 * pr + hh
                diag = jnp.log(dt_t[r:r + 1, :] + dt_t[4 + r:5 + r, :])
                seg = jnp.where(low, colb(acc, r) - rowt[r:r + 1, :],
                                jnp.where(up, colb(acc, 4 + r) - rowt[4 + r:5 + r, :], diag))
                ms.append((cb * jnp.exp(seg)).astype(BF16))
            xp = xb[:, pr * Q:(pr + 1) * Q]
            xstack = jnp.concatenate([jnp.where(lo, xp, zero_x), jnp.where(lo, zero_x, xp)], axis=0)
            y = _dot(jnp.concatenate(ms, axis=1), xstack)
            y = (y + ec[:, pr * Q:(pr + 1) * Q] * yoff[:, pr * Q:(pr + 1) * Q]
                 + ec[:, GW + pr * Q:GW + (pr + 1) * Q] * yoff[:, GW + pr * Q:GW + (pr + 1) * Q])
            ys.append(y)
        y_s[rows(t0), :] = jnp.concatenate(ys, axis=1)

    _chunk_loop(n_out, 2, stage_y)

    def stage_out(c, t0):
        y = y_s[rows(t0), :] + dsk * x_ref[rows(t0), :].astype(F32)
        zf = z_ref[rows(t0), :].astype(F32)
        u = y * _silu(zf)
        o_ref[rows(t0), :] = _rms(u, ng).astype(o_ref.dtype)

    _chunk_loop(n_out, 3 if n_out % 3 == 0 else 4, stage_out)


def _slot_order(v):
    s = v.reshape(2, SSD_GROUPS, SSD_HPG).transpose(1, 0, 2).reshape(1, 2 * SSD_HEADS)
    return jnp.pad(s, ((0, 0), (0, Q - 2 * SSD_HEADS)))


def _ssd(xa, p, h, wdt, dt_bias, a_log, ssd_d, ssd_norm_g, emit_ctx):
    b = xa.shape[0]
    t_out = T if emit_ctx else L
    row = pl.BlockSpec((None, T, Q), lambda b_: (b_, 0, 0))
    col = pl.BlockSpec((None, Q, T), lambda b_: (b_, 0, 0))
    one = pl.BlockSpec((1, Q), lambda b_: (0, 0))
    accg, wts, eacc, rowt, dtt = pl.pallas_call(
        _ssd_prep_kernel,
        out_shape=([jax.ShapeDtypeStruct((b, SSD_GROUPS, T, Q), F32)] + [jax.ShapeDtypeStruct((b, T, Q), F32)] * 2
                   + [jax.ShapeDtypeStruct((b, Q, T), F32)] * 2),
        grid=(b,),
        in_specs=[pl.BlockSpec((None, T, D), lambda b_: (b_, 0, 0)), pl.BlockSpec((D, Q), lambda b_: (0, 0)),
                  one, one],
        out_specs=[pl.BlockSpec((None, SSD_GROUPS, T, Q), lambda b_: (b_, 0, 0, 0)), row, row, col, col],
        compiler_params=_cparams(1),
        name="ssd_prep",
    )(h, wdt, _slot_order(dt_bias), _slot_order(a_log))

    dsk = jnp.repeat(ssd_d, HEAD_DIM).reshape(1, SSD_INNER)
    ng = ssd_norm_g.reshape(1, SSD_INNER)
    xoff, boff, coff, zoff = 0, SSD_INNER // Q, (SSD_INNER + SSD_GROUPS * SSD_STATE) // Q, C_Z // GW
    full = pl.BlockSpec((None, T, Q), lambda b_, g: (b_, 0, 0))
    slot = pl.BlockSpec((None, 8, T), lambda b_, g: (b_, g, 0))
    return pl.pallas_call(
        functools.partial(_ssd_kernel, n_out=t_out // Q),
        out_shape=jax.ShapeDtypeStruct((b, t_out, SSD_INNER), BF16),
        grid=(b, SSD_GROUPS),
        in_specs=[pl.BlockSpec((None, T, GW), lambda b_, g: (b_, 0, xoff + g)),
                  pl.BlockSpec((None, T, Q), lambda b_, g: (b_, 0, boff + g)),
                  pl.BlockSpec((None, T, Q), lambda b_, g: (b_, 0, coff + g)),
                  pl.BlockSpec((None, T, GW), lambda b_, g: (b_, 0, zoff + g)),
                  pl.BlockSpec((None, None, T, Q), lambda b_, g: (b_, g, 0, 0)), full, full, slot, slot,
                  pl.BlockSpec((1, GW), lambda b_, g: (0, g)),
                  pl.BlockSpec((1, GW), lambda b_, g: (0, g))],
        out_specs=pl.BlockSpec((None, t_out, GW), lambda b_, g: (b_, 0, g)),
        scratch_shapes=[pltpu.VMEM((T, 2 * GW), BF16),
                        pltpu.VMEM((T, 2 * GW), F32),
                        pltpu.VMEM((NCH, SSD_STATE, 2 * GW), F32),
                        pltpu.VMEM((NCH, SSD_STATE, 2 * GW), BF16),
                        pltpu.VMEM((T, Q), F32),
                        pltpu.VMEM((T, GW), F32)],
        compiler_params=_cparams(2),
        name="ssd",
    )(xa, xa, xa, p, accg, wts, eacc, rowt, dtt, dsk, ng)


def _softmax_t(parts, extra=None):
    mx = None
    for s in parts:
        r = jnp.max(s, axis=0, keepdims=True)
        mx = r if mx is None else jnp.maximum(mx, r)
    if extra is not None:
        mx = jnp.maximum(mx, extra)
    den = None
    es = []
    for s in parts:
        e = jnp.exp2(s - mx)
        d = jnp.sum(e, axis=0, keepdims=True)
        den = d if den is None else den + d
        es.append(e.astype(BF16))
    if extra is not None:
        den = den + jnp.exp2(extra - mx)
    return es, 1.0 / den


def _pv_t(vts, es):
    out = None
    for vt, e in zip(vts, es):
        o = _dot(vt, e)
        out = o if out is None else out + o
    return out


def _na_row_offsets():
    table = []
    for j in (0, 1, 2, ROWS // NA_QROWS - 2, ROWS // NA_QROWS - 1):
        start = int(np.clip(NA_QROWS * j - NA_KH // 2, 0, ROWS - NA_KROWS))
        per_row = []
        for i in range(NA_KROWS):
            kr = start + i
            offs = []
            for qr in range(NA_QROWS):
                r = NA_QROWS * j + qr
                rs = int(np.clip(r - NA_KH // 2, 0, ROWS - NA_KH))
                offs.append(kr - r + NA_KH - 1 if rs <= kr < rs + NA_KH else NA_NOFF)
            per_row.append(tuple(offs))
        table.append(per_row)
    return table


def _na_kernel(q_ref, k_ref, v_ref, src_ref, o_ref, vt_ref, bias_ref, sa_ref, sb_ref, pa_ref, pb_ref, ra_ref,
               rb_ref, *, emit_ctx):
    @pl.when(pl.program_id(1) == 0)
    def _():
        lo_tile = lax.broadcasted_iota(jnp.int32, (GRID_W, Q), 1) < GRID_W
        for ci, per_row in enumerate(_na_row_offsets()):
            for i, (off0, off1) in enumerate(per_row):
                for hh in range(2):
                    bias_ref[ci, i * GRID_W:(i + 1) * GRID_W, hh * Q:(hh + 1) * Q] = jnp.where(
                        lo_tile, src_ref[hh, off0], src_ref[hh, off1])

    lane = lax.broadcasted_iota(jnp.int32, (NA_NQ, Q), 1)
    lo = lane < HEAD_DIM
    top = lax.broadcasted_iota(jnp.int32, (Q, NA_NQ), 0) < HEAD_DIM
    zero = jnp.zeros((NA_NQ, Q), BF16)

    def vt_body(c, t0):
        vt_ref[:, pl.ds(t0, Q)] = v_ref[pl.ds(t0, Q), :].astype(F32).T.astype(BF16)

    _chunk_loop(NCH, 6, vt_body)

    kc = k_ref[L:T, :]
    vtc = vt_ref[:, L:T]

    def qpair(q0):
        q = (q_ref[pl.ds(q0, NA_NQ), :].astype(F32) * QSCALE).astype(BF16)
        return jnp.concatenate([jnp.where(lo, q, zero), jnp.where(lo, zero, q)], axis=0)

    def finish(q0, ot, rden):
        ot = ot * rden
        w = jnp.where(top, ot[:, 0:NA_NQ], ot[:, NA_NQ:2 * NA_NQ])
        o_ref[pl.ds(q0, NA_NQ), :] = w.T.astype(o_ref.dtype)

    def block_params(j):
        q0 = pl.multiple_of(j * NA_NQ, NA_NQ)
        start = jnp.clip(NA_QROWS * j - NA_KH // 2, 0, ROWS - NA_KROWS)
        koff = pl.multiple_of(start * GRID_W, 2 * GRID_W)
        last = ROWS // NA_QROWS - 1
        case = (jnp.where(j >= 1, 1, 0) + jnp.where(j >= 2, 1, 0)
                + jnp.where(j >= last - 1, 1, 0) + jnp.where(j >= last, 1, 0))
        return q0, koff, case

    def qk(j, s_ref):
        q0, koff, case = block_params(j)
        qp = qpair(q0)
        s_ref[0:NA_NK, :] = _nt(k_ref[pl.ds(koff, NA_NK), :], qp) + bias_ref[case]
        s_ref[NA_NK:NA_NK + NC, :] = _nt(kc, qp)

    def soft(s_ref, p_ref, r_ref):
        s = s_ref[...]
        e = jnp.exp2(s - jnp.max(s, axis=0, keepdims=True))
        r_ref[...] = jnp.broadcast_to(1.0 / jnp.sum(e, axis=0, keepdims=True), r_ref.shape)
        p_ref[...] = e.astype(BF16)

    def pv(j, p_ref, r_ref):
        q0, koff, _ = block_params(j)
        ot = (_dot(vt_ref[:, pl.ds(koff, NA_NK)], p_ref[0:NA_NK, :])
              + _dot(vtc, p_ref[NA_NK:NA_NK + NC, :]))
        finish(q0, ot, r_ref[0:1, :])

    nblk = L // NA_NQ
    qk(0, sa_ref)
    qk(1, sb_ref)
    soft(sa_ref, pa_ref, ra_ref)

    def body(m, carry):
        pv(2 * m - 2, pa_ref, ra_ref)
        qk(2 * m, sa_ref)
        soft(sb_ref, pb_ref, rb_ref)
        pv(2 * m - 1, pb_ref, rb_ref)
        qk(2 * m + 1, sb_ref)
        soft(sa_ref, pa_ref, ra_ref)
        return carry

    lax.fori_loop(1, nblk // 2, body, 0)
    pv(nblk - 2, pa_ref, ra_ref)
    soft(sb_ref, pb_ref, rb_ref)
    pv(nblk - 1, pb_ref, rb_ref)

    if emit_ctx:
        ctx_blocks = []
        for cbk in range(NC // NA_NQ):
            q0 = L + cbk * NA_NQ
            ctx_blocks.append((q0, _nt(kc, qpair(q0))))
        ctx_soft = [(q0,) + _softmax_t([s]) for q0, s in ctx_blocks]
        for q0, es, rden in ctx_soft:
            finish(q0, _pv_t([vtc], es), rden)


def _na_bias_source(rpb_all):
    depth = rpb_all.shape[0]
    ck = np.arange(GRID_W)[:, None]
    cq = (np.arange(Q) % GRID_W)[None, :]
    col_oh = ((ck - cq + NA_KW - 1)[None] == np.arange(2 * NA_KW - 1)[:, None, None]).astype(np.float32)
    cs = np.clip(cq - NA_KW // 2, 0, GRID_W - NA_KW)
    col_ok = (ck >= cs) & (ck < cs + NA_KW)
    rp = rpb_all.reshape(depth, NA_HEADS // 2, 2, NA_NOFF, 2 * NA_KW - 1)
    src = jnp.einsum("lperd,dbz->lperbz", rp, col_oh, precision=HIGHEST)
    src = jnp.where(col_ok, src * LOG2E, NEG)
    return jnp.concatenate([src, jnp.full_like(src[:, :, :, :1], NEG)], axis=3)


def _na(p, bias_src, emit_ctx):
    b = p.shape[0]
    t_out = T if emit_ctx else L
    qo, ko, vo = C_NAQ // Q, C_NAK // Q, C_NAV // Q
    return pl.pallas_call(
        functools.partial(_na_kernel, emit_ctx=emit_ctx),
        out_shape=jax.ShapeDtypeStruct((b, t_out, D), BF16),
        grid=(NA_HEADS // 2, b),
        in_specs=[pl.BlockSpec((None, T, Q), lambda h, b_: (b_, 0, qo + h)),
                  pl.BlockSpec((None, T, Q), lambda h, b_: (b_, 0, ko + h)),
                  pl.BlockSpec((None, T, Q), lambda h, b_: (b_, 0, vo + h)),
                  pl.BlockSpec((None, 2, NA_NOFF + 1, GRID_W, Q), lambda h, b_: (h, 0, 0, 0, 0))],
        out_specs=pl.BlockSpec((None, t_out, Q), lambda h, b_: (b_, 0, h)),
        scratch_shapes=[pltpu.VMEM((Q, T), BF16),
                        pltpu.VMEM((NA_CASES, NA_NK, 2 * NA_NQ), F32),
                        pltpu.VMEM((NA_NK + NC, 2 * NA_NQ), F32),
                        pltpu.VMEM((NA_NK + NC, 2 * NA_NQ), F32),
                        pltpu.VMEM((NA_NK + NC, 2 * NA_NQ), BF16),
                        pltpu.VMEM((NA_NK + NC, 2 * NA_NQ), BF16),
                        pltpu.VMEM((8, 2 * NA_NQ), F32),
                        pltpu.VMEM((8, 2 * NA_NQ), F32)],
        compiler_params=pltpu.CompilerParams(dimension_semantics=("parallel", "arbitrary"),
                                             vmem_limit_bytes=VMEM_LIMIT),
        name="na_attn",
    )(p, p, p, bias_src)


SWA_KWIN = 3 * SWA_W
SWA_QB = 128
SWA_STACK = 4 * SWA_QB
ROPE_RC = 256


def _swap_rotary_halves(t, lane):
    quarter = HEAD_DIM // 4
    a = pltpu.roll(t, Q - quarter, axis=1)
    b = pltpu.roll(t, quarter, axis=1)
    return jnp.where((lane % (2 * quarter)) < quarter, a, b)


def _swa_kernel(sink_ref, q_ref, k_ref, v_ref, cos_ref, sin_ref, mask_ref, o_ref, krot_ref, vt_ref,
                sa_ref, sb_ref, pa_ref, pb_ref, ra_ref, rb_ref, *, emit_ctx):
    kp = pl.program_id(1)
    lane = lax.broadcasted_iota(jnp.int32, (SWA_QB, Q), 1)
    lo = lane < HEAD_DIM
    zero = jnp.zeros((SWA_QB, Q), BF16)
    col = lax.broadcasted_iota(jnp.int32, (1, SWA_STACK), 1)
    eye4 = jnp.where(lax.broadcasted_iota(jnp.int32, (SWA_STACK, Q), 0) % SWA_QB
                     == lax.broadcasted_iota(jnp.int32, (SWA_STACK, Q), 1), 1.0, 0.0).astype(BF16)

    def rope(t, r0, n):
        lane_n = lax.broadcasted_iota(jnp.int32, (n, Q), 1)
        return t * cos_ref[pl.ds(r0, n), :] + _swap_rotary_halves(t, lane_n) * sin_ref[pl.ds(r0, n), :]

    def krot_body(i, carry):
        r0 = pl.multiple_of(i * ROPE_RC, ROPE_RC)
        for gl in range(2):
            t = k_ref[pl.ds(r0, ROPE_RC), gl * Q:(gl + 1) * Q].astype(F32)
            krot_ref[pl.ds(r0, ROPE_RC), gl * Q:(gl + 1) * Q] = rope(t, r0, ROPE_RC).astype(BF16)
        return carry

    lax.fori_loop(0, L // ROPE_RC, krot_body, 0, unroll=2)

    def vt_body(c, t0):
        for gl in range(2):
            vt = v_ref[pl.ds(t0, Q), gl * Q:(gl + 1) * Q].astype(F32).T
            vt_ref[gl, :, pl.ds(t0, Q)] = vt[0:HEAD_DIM].astype(BF16)

    _chunk_loop(NCH, 6, vt_body)

    def stack_q(qa, qb):
        return jnp.concatenate([jnp.where(lo, qa, zero), jnp.where(lo, zero, qa),
                                jnp.where(lo, qb, zero), jnp.where(lo, zero, qb)], axis=0)

    def sink_row(gl):
        base = kp * 8 + gl * 4
        row = jnp.where(col < SWA_QB, sink_ref[base],
                        jnp.where(col < 2 * SWA_QB, sink_ref[base + 1],
                                  jnp.where(col < 3 * SWA_QB, sink_ref[base + 2], sink_ref[base + 3])))
        return row * LOG2E

    def finish(q0, gl, ot, rden):
        ot = ot * rden
        pa = jnp.concatenate([ot[:, 0:SWA_QB], ot[:, SWA_QB:2 * SWA_QB]], axis=0).T
        pb = jnp.concatenate([ot[:, 2 * SWA_QB:3 * SWA_QB], ot[:, 3 * SWA_QB:4 * SWA_QB]], axis=0).T
        o_ref[pl.ds(q0, SWA_QB), gl * 2 * Q:(gl + 1) * 2 * Q] = jnp.concatenate([pa, pb], axis=1).astype(o_ref.dtype)

    def block_params(i):
        q0 = pl.multiple_of(i * SWA_QB, SWA_QB)
        ws = pl.multiple_of(jnp.clip((i - 1) * SWA_W, 0, L - SWA_KWIN), SWA_W)
        case = jnp.where(i >= 1, 1, 0) + jnp.where(i >= L // SWA_QB - 1, 1, 0)
        return q0, ws, case

    def qk(i, gl, s_ref):
        q0, ws, case = block_params(i)
        qa = rope(q_ref[pl.ds(q0, SWA_QB), gl * 2 * Q:gl * 2 * Q + Q].astype(F32), q0, SWA_QB) * QSCALE
        qb = rope(q_ref[pl.ds(q0, SWA_QB), gl * 2 * Q + Q:(gl + 1) * 2 * Q].astype(F32), q0, SWA_QB) * QSCALE
        qs = stack_q(qa.astype(BF16), qb.astype(BF16))
        a_loc = jnp.concatenate([krot_ref[pl.ds(ws, SWA_KWIN), gl * Q:(gl + 1) * Q], mask_ref[case]], axis=1)
        s_ref[0:SWA_KWIN, :] = _nt(a_loc, jnp.concatenate([qs, eye4], axis=1))
        s_ref[SWA_KWIN:SWA_KWIN + NC, :] = _nt(k_ref[L:T, gl * Q:(gl + 1) * Q], qs)

    def soft(gl, s_ref, p_ref, r_ref):
        s = s_ref[...]
        sink = sink_row(gl)
        mx = jnp.maximum(jnp.max(s, axis=0, keepdims=True), sink)
        e = jnp.exp2(s - mx)
        den = jnp.sum(e, axis=0, keepdims=True) + jnp.exp2(sink - mx)
        r_ref[...] = jnp.broadcast_to(1.0 / den, r_ref.shape)
        p_ref[...] = e.astype(BF16)

    def pv(i, gl, p_ref, r_ref):
        q0, ws, _ = block_params(i)
        ot = (_dot(vt_ref[gl, :, pl.ds(ws, SWA_KWIN)], p_ref[0:SWA_KWIN, :])
              + _dot(vt_ref[gl, :, L:T], p_ref[SWA_KWIN:SWA_KWIN + NC, :]))
        finish(q0, gl, ot, r_ref[0:1, :])

    nblk = L // SWA_QB
    qk(0, 0, sa_ref)
    qk(0, 1, sb_ref)
    soft(0, sa_ref, pa_ref, ra_ref)

    def body(i, carry):
        pv(i - 1, 0, pa_ref, ra_ref)
        qk(i, 0, sa_ref)
        soft(1, sb_ref, pb_ref, rb_ref)
        pv(i - 1, 1, pb_ref, rb_ref)
        qk(i, 1, sb_ref)
        soft(0, sa_ref, pa_ref, ra_ref)
        return carry

    lax.fori_loop(1, nblk, body, 0)
    pv(nblk - 1, 0, pa_ref, ra_ref)
    soft(1, sb_ref, pb_ref, rb_ref)
    pv(nblk - 1, 1, pb_ref, rb_ref)

    for cbk in range(NC // SWA_QB if emit_ctx else 0):
        q0 = L + cbk * SWA_QB
        scores = []
        for gl in range(2):
            qa = q_ref[q0:q0 + SWA_QB, gl * 2 * Q:gl * 2 * Q + Q].astype(F32) * QSCALE
            qb = q_ref[q0:q0 + SWA_QB, gl * 2 * Q + Q:(gl + 1) * 2 * Q].astype(F32) * QSCALE
            scores.append(_nt(k_ref[L:T, gl * Q:(gl + 1) * Q], stack_q(qa.astype(BF16), qb.astype(BF16))))
        soft = [_softmax_t([s], sink_row(gl)) for gl, s in enumerate(scores)]
        for gl, (es, rden) in enumerate(soft):
            finish(q0, gl, _pv_t([vt_ref[gl, :, L:T]], es), rden)


def _rope_tables():
    pos = np.arange(L)
    quarter = HEAD_DIM // 4
    lane = np.arange(Q) % HEAD_DIM
    inv = ROPE_BASE ** (-(lane % quarter).astype(np.float64) / quarter)
    p = np.where(lane[None, :] < HEAD_DIM // 2, (pos // GRID_W)[:, None], (pos % GRID_W)[:, None]).astype(np.float64)
    ang = p * inv[None, :]
    sign = np.where((lane % (2 * quarter)) < quarter, -1.0, 1.0)[None, :]
    return jnp.asarray(np.cos(ang), F32), jnp.asarray(np.sin(ang) * sign, F32)


def _swa_mask_table():
    key = np.arange(SWA_KWIN)[:, None]
    qry = np.arange(SWA_QB)[None, :]
    tabs = [np.where(np.abs(key - qry + delta) <= SWA_W, 0.0, NEG) for delta in (0, -SWA_W, -2 * SWA_W)]
    return jnp.asarray(np.stack(tabs), BF16)


def _swa(p, sink, cos_t, sin_t, mask_t, emit_ctx):
    b = p.shape[0]
    t_out = T if emit_ctx else L
    qo, ko, vo = C_SWAQ // 512, C_SWAK // 256, C_SWAV // 256
    return pl.pallas_call(
        functools.partial(_swa_kernel, emit_ctx=emit_ctx),
        out_shape=jax.ShapeDtypeStruct((b, t_out, D), BF16),
        grid=(b, SWA_KV // 2),
        in_specs=[pl.BlockSpec(memory_space=pltpu.SMEM),
                  pl.BlockSpec((None, T, 512), lambda b_, h: (b_, 0, qo + h)),
                  pl.BlockSpec((None, T, 256), lambda b_, h: (b_, 0, ko + h)),
                  pl.BlockSpec((None, T, 256), lambda b_, h: (b_, 0, vo + h)),
                  pl.BlockSpec((L, Q), lambda b_, h: (0, 0)),
                  pl.BlockSpec((L, Q), lambda b_, h: (0, 0)),
                  pl.BlockSpec((3, SWA_KWIN, SWA_QB), lambda b_, h: (0, 0, 0))],
        out_specs=pl.BlockSpec((None, t_out, 512), lambda b_, h: (b_, 0, h)),
        scratch_shapes=[pltpu.VMEM((L, 2 * Q), BF16),
                        pltpu.VMEM((2, HEAD_DIM, T), BF16),
                        pltpu.VMEM((SWA_KWIN + NC, SWA_STACK), F32),
                        pltpu.VMEM((SWA_KWIN + NC, SWA_STACK), F32),
                        pltpu.VMEM((SWA_KWIN + NC, SWA_STACK), BF16),
                        pltpu.VMEM((SWA_KWIN + NC, SWA_STACK), BF16),
                        pltpu.VMEM((8, SWA_STACK), F32),
                        pltpu.VMEM((8, SWA_STACK), F32)],
        compiler_params=_cparams(2),
        name="swa_attn",
    )(sink, p, p, p, cos_t, sin_t, mask_t)


def _merge_kernel(x_ref, ya_ref, yb_ref, yc_ref, g0_ref, g1_ref, g2_ref, mgl_ref, mgc_ref,
                  wa_ref, wb_ref, wc_ref, wo_ref, o_ref):
    m = (_sigmoid(g0_ref[...].astype(F32)) * _dot(ya_ref[...], wa_ref[...])
         + _sigmoid(g1_ref[...].astype(F32)) * _dot(yb_ref[...], wb_ref[...])
         + _sigmoid(g2_ref[...].astype(F32)) * _dot(yc_ref[...], wc_ref[...]))
    mg = _pick_mod(mgl_ref, mgc_ref, x_ref.shape[0])
    o_ref[...] = x_ref[...] + mg * _dot(m.astype(BF16), wo_ref[...])


def _const_spec(shape, single=True):
    mode = {"pipeline_mode": pl.Buffered(1)} if single else {}
    return pl.BlockSpec(shape, lambda b, j: (0,) * len(shape), **mode)


def _merge(xs, ya, yb, yc, p, ms, wa, wb, wc, wo, tr, n_tiles):
    b = xs.shape[0]
    go = C_GATE // D
    row = lambda w: pl.BlockSpec((None, tr, w), lambda b_, j: (b_, j, 0))
    gate = lambda k: pl.BlockSpec((None, tr, D), lambda b_, j: (b_, j, go + k))
    return pl.pallas_call(
        _merge_kernel,
        out_shape=jax.ShapeDtypeStruct((b, n_tiles * tr, D), F32),
        grid=(b, n_tiles),
        in_specs=[row(D), row(SSD_INNER), row(D), row(D), gate(0), gate(1), gate(2), *_mod_pair(2, tr),
                  _const_spec((SSD_INNER, D)), _const_spec((D, D)), _const_spec((D, D)), _const_spec((D, D))],
        out_specs=row(D),
        compiler_params=_cparams(2),
        name="merge",
    )(xs, ya, yb, yc, p, p, p, ms, ms, wa, wb, wc, wo)


FF_CH = 1024


def _ffn_kernel(*refs, final):
    if final:
        x_ref, g_ref, shl, shc, scl, scc, mgl, mgc, w1_ref, w2_ref, gn_ref, o_ref = refs
    else:
        (x_ref, g_ref, shl, shc, scl, scc, mgl, mgc, w1_ref, w2_ref, gn_ref, shnl, shnc, scnl, scnc,
         o_ref, h_ref) = refs
    tr = x_ref.shape[0]
    x = x_ref[...]
    h = (_rms(x, g_ref[...]) * (1.0 + _pick_mod(scl, scc, tr)) + _pick_mod(shl, shc, tr)).astype(BF16)
    acc = None
    for kf in range(D_FF // FF_CH):
        a = jnp.maximum(_dot(h, w1_ref[:, kf * FF_CH:(kf + 1) * FF_CH]), 0.0)
        o = _dot((a * a).astype(BF16), w2_ref[kf * FF_CH:(kf + 1) * FF_CH, :])
        acc = o if acc is None else acc + o
    y = x + _pick_mod(mgl, mgc, tr) * acc
    if final:
        o_ref[...] = _rms(y, gn_ref[...])
    else:
        o_ref[...] = y
        h_ref[...] = (_rms(y, gn_ref[...]) * (1.0 + _pick_mod(scnl, scnc, tr))
                      + _pick_mod(shnl, shnc, tr)).astype(h_ref.dtype)


def _ffn(xs, g, ms, w1, w2, tr, n_tiles, g_next, ms_next=None):
    b = xs.shape[0]
    final = ms_next is None
    row = pl.BlockSpec((None, tr, D), lambda b_, j: (b_, j, 0))
    in_specs = [row, _const_spec((1, D)), *_mod_pair(3, tr), *_mod_pair(4, tr), *_mod_pair(5, tr),
                _const_spec((D, D_FF), single=False), _const_spec((D_FF, D), single=False)]
    in_specs.append(_const_spec((1, D)))
    args = [xs, g.reshape(1, D), ms, ms, ms, ms, ms, ms, w1, w2, g_next.reshape(1, D)]
    out_shape = jax.ShapeDtypeStruct((b, n_tiles * tr, D), F32)
    out_specs = row
    if not final:
        in_specs += [*_mod_pair(0, tr), *_mod_pair(1, tr)]
        args += [ms_next, ms_next, ms_next, ms_next]
        out_shape = [out_shape, jax.ShapeDtypeStruct((b, n_tiles * tr, D), BF16)]
        out_specs = [row, row]
    return pl.pallas_call(
        functools.partial(_ffn_kernel, final=final),
        out_shape=out_shape,
        grid=(b, n_tiles),
        in_specs=in_specs,
        out_specs=out_specs,
        compiler_params=_cparams(2),
        name="ffn",
    )(*args)


def _prep_w_in(w):
    def dup(a):
        a = a.reshape(D, SWA_KV, 1, HEAD_DIM)
        return jnp.concatenate([a, a], axis=2).reshape(D, SWA_KV * 128)
    cat = jnp.concatenate([
        w[:, R_XBC:R_DT], w[:, R_NAK:R_NAV], w[:, R_NAV:R_SWAK], dup(w[:, R_SWAK:R_SWAV]),
        dup(w[:, R_SWAV:R_Z]), w[:, R_Z:R_NAQ], w[:, R_NAQ:R_SWAQ], w[:, R_SWAQ:R_GATE],
        w[:, R_GATE:R_END]], axis=1).astype(BF16)
    wd = w[:, R_DT:R_NAK].reshape(D, 2, SSD_GROUPS, SSD_HPG).transpose(0, 2, 1, 3).reshape(D, 2 * SSD_HEADS)
    wdt = jnp.concatenate([wd, jnp.zeros((D, Q - 2 * SSD_HEADS), F32)], axis=1).astype(BF16)
    return cat, wdt


def kernel(x, c, ctx, c_ctx, ada_w, ada_b, norm1_g, norm2_g, w_in, conv_w, conv_b, dt_bias, a_log, ssd_d,
           ssd_norm_g, na_rpb, swa_sink, w_o_ssd, w_o_na, w_o_swa, w_out, w_ff1, w_ff2, final_g):
    b = x.shape[0]
    depth = ada_w.shape[0]
    xs = jnp.concatenate([x, ctx], axis=1)
    nrow = -(-(b + 1) // 8) * 8
    cvec = jnp.concatenate([c, c_ctx[None, :], jnp.zeros((nrow - b - 1, D), F32)], axis=0)
    mod = _ada_mod(cvec, ada_w, ada_b)
    lat = mod[:, :b].reshape(depth, b, 1, N_MOD, 1, D)
    cx = jnp.broadcast_to(mod[:, b].reshape(depth, 1, 1, N_MOD, 1, D), lat.shape)
    ms_all = jnp.concatenate([lat, cx], axis=2)
    cos_t, sin_t = _rope_tables()
    mask_t = _swa_mask_table()
    na_bias = _na_bias_source(na_rpb)

    h = _normmod(xs, norm1_g[0], ms_all[0])
    for l in range(depth):
        last = l + 1 == depth
        ms = ms_all[l]
        wcat, wdt = _prep_w_in(w_in[l])
        p = _matmul(h, wcat, BF16, 1024, "in_proj")
        xa = _conv_silu(p, conv_w[l], conv_b[l])
        ya = _ssd(xa, p, h, wdt, dt_bias[l], a_log[l], ssd_d[l], ssd_norm_g[l], not last)
        yb = _na(p, na_bias[l], not last)
        yc = _swa(p, swa_sink[l], cos_t, sin_t, mask_t, not last)
        rows, tr = MERGE_TILES[last]
        xs = _merge(xs, ya, yb, yc, p, ms, w_o_ssd[l].astype(BF16), w_o_na[l].astype(BF16),
                    w_o_swa[l].astype(BF16), w_out[l].astype(BF16), tr, rows // tr)
        w1, w2 = w_ff1[l].astype(BF16), w_ff2[l].astype(BF16)
        rows, tr = FFN_TILES[last]
        if last:
            return _ffn(xs, norm2_g[l], ms, w1, w2, tr, rows // tr, final_g)
        xs, h = _ffn(xs, norm2_g[l], ms, w1, w2, tr, rows // tr, norm1_g[l + 1], ms_all[l + 1])
```
